```python
import math
import jax, jax.numpy as jnp
from jax import lax
import numpy as np

D_MODEL = 2048
BATCH = 8
SEQ = 8192
DEPTH = 1

D_MIX = D_MODEL
HEAD_DIM = 64
D_ATTN = D_MIX // 2
D_CONV = D_MIX - D_ATTN
N_Q_HEADS = D_ATTN // HEAD_DIM
N_KV_HEADS = 4
GQA_GROUP = N_Q_HEADS // N_KV_HEADS
D_KV = N_KV_HEADS * HEAD_DIM
WINDOW = 128
BLOCK = 128
CONV_WIDTH = 31
CONV_GROUPS = D_CONV // HEAD_DIM
N_BUCKETS = 32
MAX_DISTANCE = 128
LN_EPS = 1e-5
ALPHA = (2.0 * DEPTH) ** 0.25
BETA = (8.0 * DEPTH) ** -0.25

SPLIT_SIZES = (D_ATTN, D_KV, D_KV, D_ATTN, D_CONV, D_CONV, D_CONV)
D_IN = sum(SPLIT_SIZES)
SPLIT_POINTS = [int(s) for s in np.cumsum(SPLIT_SIZES)[:-1]]

kernel_name = "hybrid_conformer_swa_sink_deepnorm_adaln"


def layer_norm(x, g, b):
    xf = x.astype(jnp.float32)
    mu = jnp.mean(xf, axis=-1, keepdims=True)
    var = jnp.mean(jnp.square(xf - mu), axis=-1, keepdims=True)
    y = (xf - mu) * lax.rsqrt(var + LN_EPS)
    return (y * g.astype(jnp.float32) + b.astype(jnp.float32)).astype(x.dtype)


def t5_bucket(dist):
    max_exact = N_BUCKETS // 2
    d = jnp.maximum(dist, 1).astype(jnp.float32)
    large = max_exact + (jnp.log(d / max_exact) / math.log(MAX_DISTANCE / max_exact)
                         * (N_BUCKETS - max_exact)).astype(jnp.int32)
    large = jnp.minimum(large, N_BUCKETS - 1)
    return jnp.where(dist < max_exact, dist, large)


def banded_sink_attention(q, k, v, rel_bias, sinks):
    B, S = q.shape[0], q.shape[1]
    nb = S // BLOCK
    f32 = jnp.float32
    qb = q.astype(f32).reshape(B, nb, BLOCK, N_KV_HEADS, GQA_GROUP, HEAD_DIM)
    kb = k.astype(f32).reshape(B, nb, BLOCK, N_KV_HEADS, HEAD_DIM)
    vb = v.astype(f32).reshape(B, nb, BLOCK, N_KV_HEADS, HEAD_DIM)
    prev = lambda t: jnp.concatenate([jnp.zeros_like(t[:, :1]), t[:, :-1]], axis=1)
    kw = jnp.concatenate([prev(kb), kb], axis=2)
    vw = jnp.concatenate([prev(vb), vb], axis=2)
    scores = jnp.einsum('bnqhgd,bnkhd->bnhgqk', qb, kw) * (HEAD_DIM ** -0.5)

    qi = jnp.arange(BLOCK, dtype=jnp.int32)[:, None]
    kj = jnp.arange(2 * BLOCK, dtype=jnp.int32)[None, :]
    dist = qi + BLOCK - kj
    in_window = (dist >= 0) & (dist < WINDOW)
    bias = rel_bias.astype(f32)[t5_bucket(jnp.maximum(dist, 0))]
    bias = bias.transpose(2, 0, 1).reshape(N_KV_HEADS, GQA_GROUP, BLOCK, 2 * BLOCK)
    key_pos = jnp.arange(nb, dtype=jnp.int32)[:, None] * BLOCK - BLOCK + kj
    valid = in_window[None] & (key_pos[:, None, :] >= 0)

    scores = jnp.where(valid[None, :, None, None], scores + bias, jnp.finfo(f32).min)
    sink = jnp.broadcast_to(sinks.astype(f32).reshape(1, 1, N_KV_HEADS, GQA_GROUP, 1, 1),
                            scores.shape[:-1] + (1,))
    probs = jax.nn.softmax(jnp.concatenate([scores, sink], axis=-1), axis=-1)[..., :-1]
    out = jnp.einsum('bnhgqk,bnkhd->bnqhgd', probs, vw)
    return out.reshape(B, S, N_Q_HEADS * HEAD_DIM).astype(q.dtype)


def conformer_conv(glu_a, glu_b, conv_w, conv_b, ln_g, ln_b, w_pw, b_pw):
    u = glu_a * jax.nn.sigmoid(glu_b)
    u = lax.conv_general_dilated(
        u, conv_w.reshape(CONV_WIDTH, 1, D_CONV).astype(u.dtype),
        window_strides=(1,), padding=[(CONV_WIDTH - 1, 0)],
        dimension_numbers=('NWC', 'WIO', 'NWC'), feature_group_count=D_CONV) + conv_b
    u = jax.nn.silu(layer_norm(u, ln_g, ln_b))
    return u @ w_pw + b_pw


def _fwd_setup_inputs(seed: int = 0) -> dict:
    key = jax.random.key(seed)
    ks = jax.random.split(key, 20)
    n = jax.random.normal
    f32 = jnp.float32
    x = n(ks[0], (BATCH, SEQ, D_MODEL), f32)
    c = n(ks[1], (BATCH, D_MODEL), f32)
    w_ada = 0.5 * D_MODEL ** -0.5 * n(ks[2], (DEPTH, D_MODEL, 3 * D_MODEL), f32)
    b_ada = 0.01 * n(ks[3], (DEPTH, 3 * D_MODEL), f32)
    col_scale = jnp.ones((D_IN,), f32).at[D_ATTN + D_KV:D_ATTN + 2 * D_KV].set(BETA)
    w_in = D_MODEL ** -0.5 * n(ks[4], (DEPTH, D_MODEL, D_IN), f32) * col_scale
    rel_bias = 0.5 * n(ks[5], (N_BUCKETS, N_Q_HEADS), f32)
    sinks = n(ks[6], (DEPTH, N_Q_HEADS), f32)
    conv_w = CONV_WIDTH ** -0.5 * n(ks[7], (DEPTH, CONV_WIDTH, D_CONV), f32)
    conv_b = 0.01 * n(ks[8], (DEPTH, D_CONV), f32)
    conv_ln_g = 1.0 + 0.01 * n(ks[9], (DEPTH, D_CONV), f32)
    conv_ln_b = 0.01 * n(ks[10], (DEPTH, D_CONV), f32)
    w_pw = BETA * D_CONV ** -0.5 * n(ks[11], (DEPTH, D_CONV, D_CONV), f32)
    b_pw = 0.01 * n(ks[12], (DEPTH, D_CONV), f32)
    w_out = BETA * D_MIX ** -0.5 * n(ks[13], (DEPTH, D_MIX, D_MODEL), f32)
    ln_g = 1.0 + 0.01 * n(ks[14], (DEPTH, D_MODEL), f32)
    ln_b = 0.01 * n(ks[15], (DEPTH, D_MODEL), f32)
    return {"x": x, "c": c, "w_ada": w_ada, "b_ada": b_ada, "w_in": w_in,
            "rel_bias": rel_bias, "sinks": sinks, "conv_w": conv_w, "conv_b": conv_b,
            "conv_ln_g": conv_ln_g, "conv_ln_b": conv_ln_b, "w_pw": w_pw, "b_pw": b_pw,
            "w_out": w_out, "ln_g": ln_g, "ln_b": ln_b}


def _fwd_reference(x, c, w_ada, b_ada, w_in, rel_bias, sinks, conv_w, conv_b, conv_ln_g,
              conv_ln_b, w_pw, b_pw, w_out, ln_g, ln_b):
    c_act = jax.nn.silu(c)
    for l in range(DEPTH):
        mod = (c_act @ w_ada[l] + b_ada[l])[:, None, :]
        shift, scale, gate = jnp.split(mod, 3, axis=-1)
        h = x * (1.0 + scale) + shift
        proj = h @ w_in[l]
        q, k, v, g_attn, glu_a, glu_b, g_conv = jnp.split(proj, SPLIT_POINTS, axis=-1)
        y_attn = banded_sink_attention(q, k, v, rel_bias, sinks[l]) * jax.nn.silu(g_attn)
        y_conv = conformer_conv(glu_a, glu_b, conv_w[l], conv_b[l], conv_ln_g[l], conv_ln_b[l],
                                w_pw[l], b_pw[l]) * jax.nn.silu(g_conv)
        y = jnp.concatenate([y_attn, y_conv], axis=-1) @ w_out[l]
        x = layer_norm(ALPHA * x + gate * y, ln_g[l], ln_b[l])
    return x


import jax as _jax
import jax.numpy as _jnp

TWIN_FORMAT = 'train_step'
FWD_PARAMS = ['x', 'c', 'w_ada', 'b_ada', 'w_in', 'rel_bias', 'sinks', 'conv_w', 'conv_b', 'conv_ln_g', 'conv_ln_b', 'w_pw', 'b_pw', 'w_out', 'ln_g', 'ln_b']
TWIN_WEIGHTS = ['w_ada', 'b_ada', 'w_in', 'rel_bias', 'sinks', 'conv_w', 'conv_b', 'conv_ln_g', 'conv_ln_b', 'w_pw', 'b_pw', 'w_out', 'ln_g', 'ln_b']
TWIN_DIFF_INPUT = 'x'
TWIN_INPUTS = ['x', 'c', 'w_ada', 'b_ada', 'w_in', 'rel_bias', 'sinks', 'conv_w', 'conv_b', 'conv_ln_g', 'conv_ln_b', 'w_pw', 'b_pw', 'w_out', 'ln_g', 'ln_b', 'loss_target', 'm_w_ada', 'm_b_ada', 'm_w_in', 'm_rel_bias', 'm_sinks', 'm_conv_w', 'm_conv_b', 'm_conv_ln_g', 'm_conv_ln_b', 'm_w_pw', 'm_b_pw', 'm_w_out', 'm_ln_g', 'm_ln_b', 'v_w_ada', 'v_b_ada', 'v_w_in', 'v_rel_bias', 'v_sinks', 'v_conv_w', 'v_conv_b', 'v_conv_ln_g', 'v_conv_ln_b', 'v_w_pw', 'v_b_pw', 'v_w_out', 'v_ln_g', 'v_ln_b']
TWIN_OUTPUTS = ['loss', 'grad_x', 'grad_w_ada', 'grad_b_ada', 'grad_w_in', 'grad_rel_bias', 'grad_sinks', 'grad_conv_w', 'grad_conv_b', 'grad_conv_ln_g', 'grad_conv_ln_b', 'grad_w_pw', 'grad_b_pw', 'grad_w_out', 'grad_ln_g', 'grad_ln_b', 'delta_w_ada', 'delta_b_ada', 'delta_w_in', 'delta_rel_bias', 'delta_sinks', 'delta_conv_w', 'delta_conv_b', 'delta_conv_ln_g', 'delta_conv_ln_b', 'delta_w_pw', 'delta_b_pw', 'delta_w_out', 'delta_ln_g', 'delta_ln_b', 'new_m_w_ada', 'new_m_b_ada', 'new_m_w_in', 'new_m_rel_bias', 'new_m_sinks', 'new_m_conv_w', 'new_m_conv_b', 'new_m_conv_ln_g', 'new_m_conv_ln_b', 'new_m_w_pw', 'new_m_b_pw', 'new_m_w_out', 'new_m_ln_g', 'new_m_ln_b', 'new_v_w_ada', 'new_v_b_ada', 'new_v_w_in', 'new_v_rel_bias', 'new_v_sinks', 'new_v_conv_w', 'new_v_conv_b', 'new_v_conv_ln_g', 'new_v_conv_ln_b', 'new_v_w_pw', 'new_v_b_pw', 'new_v_w_out', 'new_v_ln_g', 'new_v_ln_b']
TWIN_LEAF_KINDS = {'loss': 'loss', 'grad_x': 'grad_x', 'grad_w_ada': 'grad_w', 'grad_b_ada': 'grad_w', 'grad_w_in': 'grad_w', 'grad_rel_bias': 'grad_w', 'grad_sinks': 'grad_w', 'grad_conv_w': 'grad_w', 'grad_conv_b': 'grad_w', 'grad_conv_ln_g': 'grad_w', 'grad_conv_ln_b': 'grad_w', 'grad_w_pw': 'grad_w', 'grad_b_pw': 'grad_w', 'grad_w_out': 'grad_w', 'grad_ln_g': 'grad_w', 'grad_ln_b': 'grad_w', 'delta_w_ada': 'delta_w', 'delta_b_ada': 'delta_w', 'delta_w_in': 'delta_w', 'delta_rel_bias': 'delta_w', 'delta_sinks': 'delta_w', 'delta_conv_w': 'delta_w', 'delta_conv_b': 'delta_w', 'delta_conv_ln_g': 'delta_w', 'delta_conv_ln_b': 'delta_w', 'delta_w_pw': 'delta_w', 'delta_b_pw': 'delta_w', 'delta_w_out': 'delta_w', 'delta_ln_g': 'delta_w', 'delta_ln_b': 'delta_w', 'new_m_w_ada': 'new_m', 'new_m_b_ada': 'new_m', 'new_m_w_in': 'new_m', 'new_m_rel_bias': 'new_m', 'new_m_sinks': 'new_m', 'new_m_conv_w': 'new_m', 'new_m_conv_b': 'new_m', 'new_m_conv_ln_g': 'new_m', 'new_m_conv_ln_b': 'new_m', 'new_m_w_pw': 'new_m', 'new_m_b_pw': 'new_m', 'new_m_w_out': 'new_m', 'new_m_ln_g': 'new_m', 'new_m_ln_b': 'new_m', 'new_v_w_ada': 'new_v', 'new_v_b_ada': 'new_v', 'new_v_w_in': 'new_v', 'new_v_rel_bias': 'new_v', 'new_v_sinks': 'new_v', 'new_v_conv_w': 'new_v', 'new_v_conv_b': 'new_v', 'new_v_conv_ln_g': 'new_v', 'new_v_conv_ln_b': 'new_v', 'new_v_w_pw': 'new_v', 'new_v_b_pw': 'new_v', 'new_v_w_out': 'new_v', 'new_v_ln_g': 'new_v', 'new_v_ln_b': 'new_v'}


def _forward(args):
    return _fwd_reference(*[args[k] for k in FWD_PARAMS])


def _output_shape():
    def fwd():
        inp = _fwd_setup_inputs(0)
        return _fwd_reference(*[inp[k] for k in FWD_PARAMS])
    out = _jax.eval_shape(fwd)
    return out.shape, out.dtype

N_MICROBATCH = 1
ADAM_LR = 0.001
ADAM_B1 = 0.9
ADAM_B2 = 0.999
ADAM_EPS = 1e-08
ADAM_WD = 0.01
ADAM_STEP = 10
PER_EXAMPLE_BATCH_AXIS = {'x': 0, 'c': 0, 'loss_target': 0}
SHARED_INPUTS = []
_WEIGHT_DTYPES = {'w_ada': _jnp.float32, 'b_ada': _jnp.float32, 'w_in': _jnp.float32, 'rel_bias': _jnp.float32, 'sinks': _jnp.float32, 'conv_w': _jnp.float32, 'conv_b': _jnp.float32, 'conv_ln_g': _jnp.float32, 'conv_ln_b': _jnp.float32, 'w_pw': _jnp.float32, 'b_pw': _jnp.float32, 'w_out': _jnp.float32, 'ln_g': _jnp.float32, 'ln_b': _jnp.float32}
MOMENT_SCALE = {'w_ada': 5.251338e-03, 'b_ada': 8.866382e-03, 'w_in': 3.856120e-03, 'rel_bias': 2.181777e-03, 'sinks': 1.627087e-03, 'conv_w': 4.583410e-03, 'conv_b': 8.895482e-03, 'conv_ln_g': 5.256681e-03, 'conv_ln_b': 4.574276e-03, 'w_pw': 7.389274e-03, 'b_pw': 1.281179e-02, 'w_out': 6.197076e-03, 'ln_g': 3.194600e+01, 'ln_b': 2.634796e-01}


def _to_microbatches(a, axis):
    t = _jnp.moveaxis(a, axis, 0)
    t = t.reshape((N_MICROBATCH, t.shape[0] // N_MICROBATCH) + t.shape[1:])
    return _jnp.moveaxis(t, 1, axis + 1)


def setup_inputs(seed: int = 0) -> dict:
    inp = _fwd_setup_inputs(seed)
    key = _jax.random.fold_in(_jax.random.key(seed), 7919)
    shape, _ = _output_shape()
    out = dict(inp)
    out["loss_target"] = _jax.random.normal(_jax.random.fold_in(key, 0), shape, _jnp.float32)
    for i, name in enumerate(TWIN_WEIGHTS):
        w = inp[name].astype(_jnp.float32)
        if MOMENT_SCALE is None:
            s = _jnp.sqrt(_jnp.mean(_jnp.square(w)) + 1e-30)
        else:
            s = MOMENT_SCALE[name]
        km, kv = _jax.random.split(_jax.random.fold_in(key, i + 1))
        out[name] = w
        out["m_" + name] = s * _jax.random.normal(km, w.shape, _jnp.float32)
        out["v_" + name] = (s * s) * _jax.random.uniform(kv, w.shape, _jnp.float32, 0.5, 1.5)
    if N_MICROBATCH > 1:
        for name, axis in PER_EXAMPLE_BATCH_AXIS.items():
            out[name] = _to_microbatches(out[name], axis)
    return {'x': out['x'], 'c': out['c'], 'w_ada': out['w_ada'], 'b_ada': out['b_ada'], 'w_in': out['w_in'], 'rel_bias': out['rel_bias'], 'sinks': out['sinks'], 'conv_w': out['conv_w'], 'conv_b': out['conv_b'], 'conv_ln_g': out['conv_ln_g'], 'conv_ln_b': out['conv_ln_b'], 'w_pw': out['w_pw'], 'b_pw': out['b_pw'], 'w_out': out['w_out'], 'ln_g': out['ln_g'], 'ln_b': out['ln_b'], 'loss_target': out['loss_target'], 'm_w_ada': out['m_w_ada'], 'm_b_ada': out['m_b_ada'], 'm_w_in': out['m_w_in'], 'm_rel_bias': out['m_rel_bias'], 'm_sinks': out['m_sinks'], 'm_conv_w': out['m_conv_w'], 'm_conv_b': out['m_conv_b'], 'm_conv_ln_g': out['m_conv_ln_g'], 'm_conv_ln_b': out['m_conv_ln_b'], 'm_w_pw': out['m_w_pw'], 'm_b_pw': out['m_b_pw'], 'm_w_out': out['m_w_out'], 'm_ln_g': out['m_ln_g'], 'm_ln_b': out['m_ln_b'], 'v_w_ada': out['v_w_ada'], 'v_b_ada': out['v_b_ada'], 'v_w_in': out['v_w_in'], 'v_rel_bias': out['v_rel_bias'], 'v_sinks': out['v_sinks'], 'v_conv_w': out['v_conv_w'], 'v_conv_b': out['v_conv_b'], 'v_conv_ln_g': out['v_conv_ln_g'], 'v_conv_ln_b': out['v_conv_ln_b'], 'v_w_pw': out['v_w_pw'], 'v_b_pw': out['v_b_pw'], 'v_w_out': out['v_w_out'], 'v_ln_g': out['v_ln_g'], 'v_ln_b': out['v_ln_b']}


def _loss(weights, diff, rest, loss_target):
    with _jax.named_scope("forward"):
        args = {**rest, TWIN_DIFF_INPUT: diff, **{k: w.astype(_WEIGHT_DTYPES[k]) for k, w in weights.items()}}
        y = _forward(args)
    with _jax.named_scope("loss_head"):
        err = _jnp.square(y.astype(_jnp.float32) - loss_target)
        return 0.5 * _jnp.sum(_jnp.mean(err, axis=-1)) if err.ndim else 0.5 * err


def _adamw(w, g, m, v):
    m = ADAM_B1 * m + (1.0 - ADAM_B1) * g
    v = ADAM_B2 * v + (1.0 - ADAM_B2) * _jnp.square(g)
    m_hat = m / (1.0 - ADAM_B1 ** ADAM_STEP)
    v_hat = v / (1.0 - ADAM_B2 ** ADAM_STEP)
    delta = -ADAM_LR * (m_hat / (_jnp.sqrt(v_hat) + ADAM_EPS) + ADAM_WD * w)
    return delta, m, v


def reference(x, c, w_ada, b_ada, w_in, rel_bias, sinks, conv_w, conv_b, conv_ln_g, conv_ln_b, w_pw, b_pw, w_out, ln_g, ln_b, loss_target, m_w_ada, m_b_ada, m_w_in, m_rel_bias, m_sinks, m_conv_w, m_conv_b, m_conv_ln_g, m_conv_ln_b, m_w_pw, m_b_pw, m_w_out, m_ln_g, m_ln_b, v_w_ada, v_b_ada, v_w_in, v_rel_bias, v_sinks, v_conv_w, v_conv_b, v_conv_ln_g, v_conv_ln_b, v_w_pw, v_b_pw, v_w_out, v_ln_g, v_ln_b):
    given = dict(x=x, c=c, w_ada=w_ada, b_ada=b_ada, w_in=w_in, rel_bias=rel_bias, sinks=sinks, conv_w=conv_w, conv_b=conv_b, conv_ln_g=conv_ln_g, conv_ln_b=conv_ln_b, w_pw=w_pw, b_pw=b_pw, w_out=w_out, ln_g=ln_g, ln_b=ln_b, loss_target=loss_target, m_w_ada=m_w_ada, m_b_ada=m_b_ada, m_w_in=m_w_in, m_rel_bias=m_rel_bias, m_sinks=m_sinks, m_conv_w=m_conv_w, m_conv_b=m_conv_b, m_conv_ln_g=m_conv_ln_g, m_conv_ln_b=m_conv_ln_b, m_w_pw=m_w_pw, m_b_pw=m_b_pw, m_w_out=m_w_out, m_ln_g=m_ln_g, m_ln_b=m_ln_b, v_w_ada=v_w_ada, v_b_ada=v_b_ada, v_w_in=v_w_in, v_rel_bias=v_rel_bias, v_sinks=v_sinks, v_conv_w=v_conv_w, v_conv_b=v_conv_b, v_conv_ln_g=v_conv_ln_g, v_conv_ln_b=v_conv_ln_b, v_w_pw=v_w_pw, v_b_pw=v_b_pw, v_w_out=v_w_out, v_ln_g=v_ln_g, v_ln_b=v_ln_b)
    weights = {n: given[n] for n in TWIN_WEIGHTS}
    shared = {n: given[n] for n in SHARED_INPUTS}
    per_example = {n: given[n] for n in ['x', 'c']}
    grad_fn = _jax.value_and_grad(_loss, argnums=(0, 1))

    def one_microbatch(ex, loss_target):
        ex = dict(ex)
        diff = ex.pop(TWIN_DIFF_INPUT)
        return grad_fn(weights, diff, {**shared, **ex}, loss_target)

    if N_MICROBATCH == 1:
        loss, (grad_w, grad_x) = one_microbatch(per_example, given["loss_target"])
    else:
        def body(carry, xs):
            loss_sum, grad_sum = carry
            l_k, (gw_k, gx_k) = one_microbatch(xs[0], xs[1])
            with _jax.named_scope("update"):
                return (loss_sum + l_k, _jax.tree.map(_jnp.add, grad_sum, gw_k)), gx_k

        init = (_jnp.zeros((), _jnp.float32), _jax.tree.map(_jnp.zeros_like, weights))
        (loss, grad_w), grad_x = _jax.lax.scan(body, init, (per_example, given["loss_target"]))
    with _jax.named_scope("update"):
        delta_w, new_m, new_v = {}, {}, {}
        for n in TWIN_WEIGHTS:
            delta_w[n], new_m[n], new_v[n] = _adamw(weights[n], grad_w[n], given["m_" + n], given["v_" + n])
    return (loss, grad_x, *[grad_w[n] for n in TWIN_WEIGHTS], *[delta_w[n] for n in TWIN_WEIGHTS],
            *[new_m[n] for n in TWIN_WEIGHTS], *[new_v[n] for n in TWIN_WEIGHTS])
```

```python
import functools
import math

import jax
import jax.numpy as jnp
from jax import lax
from jax.experimental import pallas as pl
from jax.experimental.pallas import tpu as pltpu

F32, BF16 = jnp.float32, jnp.bfloat16
S = jax.ShapeDtypeStruct
MESH = pl.DeviceIdType.MESH

D = 2048
DA = 1024
HD = 64
NQ, NKV, GRP = 16, 4, 4
BLK = 128
CW = 31
HALO = 32
D_IN = 5632
NCHIP = 4
NB = D_IN // NCHIP
N_BUCKETS, MAX_DIST = 32, 128
LN_EPS = 1e-5
ALPHA = 2.0 ** 0.25
SCALE = HD ** -0.5
NEG = -1e30
LR, B1, B2, EPS, WD, STEP = 0.001, 0.9, 0.999, 1e-08, 0.01, 10

_VMEM = pl.BlockSpec(memory_space=pltpu.VMEM)
_SMEM = pl.BlockSpec(memory_space=pltpu.SMEM)
_ANY = pl.BlockSpec(memory_space=pl.ANY)


def _cp(sem=None, vmem_mb=None):
    kw = {}
    if sem is not None:
        kw["dimension_semantics"] = sem
    if vmem_mb is not None:
        kw["vmem_limit_bytes"] = vmem_mb * 1024 * 1024
    return pltpu.CompilerParams(**kw)


def _sig(v):
    return jax.nn.sigmoid(v)


def _dsilu(g, sg):
    return sg * (1.0 + g * (1.0 - sg))


def _me():
    return lax.axis_index("x"), lax.axis_index("y"), lax.axis_index("c")


def _allgather8(x_shard, name):
    m_per, n = x_shard.shape

    def body(x_ref, out_ref, send_sems, recv_sems, local_sem):
        x, y, c = _me()
        me, sibling = (x, y, c), (x, y, 1 - c)
        chips = [(1 - x, y), (x, 1 - y), (1 - x, 1 - y)]

        def rows(px, py, pc):
            return out_ref.at[pl.ds((4 * px + 2 * py + pc) * m_per, m_per), :]

        def copy(k, block, to, src=None):
            return pltpu.make_async_remote_copy(
                src_ref=rows(*block) if src is None else src, dst_ref=rows(*block),
                send_sem=send_sems.at[k], recv_sem=recv_sems.at[k], device_id=to, device_id_type=MESH)

        mine = pltpu.make_async_copy(x_ref, rows(*me), local_sem)
        mine.start()
        first = [copy(0, me, sibling, src=x_ref)]
        first += [copy(1 + j, me, (*chip, c), src=x_ref) for j, chip in enumerate(chips)]
        for cp in first:
            cp.start()
        passed = [copy(4 + j, (*chip, c), sibling) for j, chip in enumerate(chips)]
        for j, chip in enumerate(chips):
            copy(1 + j, (*chip, c), me).wait_recv()
            passed[j].start()
        copy(0, sibling, me).wait_recv()
        for j, chip in enumerate(chips):
            copy(4 + j, (*chip, 1 - c), me).wait_recv()
        for cp in first + passed:
            cp.wait_send()
        mine.wait()

    return pl.pallas_call(
        body, name=name, out_shape=S((8 * m_per, n), x_shard.dtype),
        in_specs=[_VMEM], out_specs=_VMEM,
        scratch_shapes=[pltpu.SemaphoreType.DMA((7,)), pltpu.SemaphoreType.DMA((7,)), pltpu.SemaphoreType.DMA],
    )(x_shard)


def _gather_weights(wi, wo, wp):
    shards = (wi, wo, wp)
    halves = [s.shape[0] // 2 for s in shards]
    nw = len(shards)

    def body(*refs):
        srcs, dsts = refs[:nw], refs[nw:2 * nw]
        send_sems, recv_sems, local_sems = refs[2 * nw:]
        x, y, c = _me()
        p = 2 * x + y
        sibling = (x, y, 1 - c)
        chips = [(1 - x, y), (x, 1 - y), (1 - x, 1 - y)]

        def half(ref, w, hc):
            return ref.at[pl.ds(hc * halves[w], halves[w])]

        def copy(k, src, dst, to):
            return pltpu.make_async_remote_copy(src_ref=src, dst_ref=dst, send_sem=send_sems.at[k],
                                                recv_sem=recv_sems.at[k], device_id=to, device_id_type=MESH)

        local = [pltpu.make_async_copy(srcs[w], dsts[w].at[p], local_sems.at[w]) for w in range(nw)]
        for cp in local:
            cp.start()
        first = []
        for w in range(nw):
            for j, chip in enumerate(chips):
                first.append(copy(3 * w + j, half(srcs[w], w, c), half(dsts[w].at[p], w, c), (*chip, c)))
        for cp in first:
            cp.start()
        passed = []
        for w in range(nw):
            for j, chip in enumerate(chips):
                q = 2 * chip[0] + chip[1]
                landed = half(dsts[w].at[q], w, c)
                copy(3 * w + j, landed, landed, (*chip, c)).wait_recv()
                fwd = copy(3 * nw + 3 * w + j, landed, landed, sibling)
                fwd.start()
                passed.append(fwd)
        for w in range(nw):
            for j, chip in enumerate(chips):
                q = 2 * chip[0] + chip[1]
                other = half(dsts[w].at[q], w, 1 - c)
                copy(3 * nw + 3 * w + j, other, other, sibling).wait_recv()
        for cp in first + passed:
            cp.wait_send()
        for cp in local:
            cp.wait()

    return pl.pallas_call(
        body, name="gather_weights",
        out_shape=[S((NCHIP,) + s.shape, s.dtype) for s in shards],
        in_specs=[_ANY] * nw, out_specs=[_ANY] * nw,
        scratch_shapes=[pltpu.SemaphoreType.DMA((6 * nw,)), pltpu.SemaphoreType.DMA((6 * nw,)),
                        pltpu.SemaphoreType.DMA((nw,))],
    )(*shards)


def _pair_exchange(g_in, g_out, g_pw):
    def body(gi, go, gp, li, lo, lp, send_sems, recv_sems):
        x, y, c = _me()
        sibling = (x, y, 1 - c)
        copies = [(gi.at[1 - c], li)]
        for j in range(NCHIP):
            copies.append((go.at[2 * j + 1 - c], lo.at[j]))
        for j in range(NCHIP):
            copies.append((gp.at[2 * j + 1 - c], lp.at[j]))
        cps = [pltpu.make_async_remote_copy(src_ref=s_, dst_ref=d_, send_sem=send_sems.at[k], recv_sem=recv_sems.at[k],
                                            device_id=sibling, device_id_type=MESH) for k, (s_, d_) in enumerate(copies)]
        for cp in cps:
            cp.start()
        for cp in cps:
            cp.wait_recv()
        for cp in cps:
            cp.wait_send()

    n = 1 + 2 * NCHIP
    return pl.pallas_call(
        body, name="grad_pair_exchange",
        out_shape=[S(g_in.shape[1:], g_in.dtype), S((NCHIP,) + g_out.shape[1:], g_out.dtype),
                   S((NCHIP,) + g_pw.shape[1:], g_pw.dtype)],
        in_specs=[_ANY] * 3, out_specs=[_ANY] * 3,
        scratch_shapes=[pltpu.SemaphoreType.DMA((n,)), pltpu.SemaphoreType.DMA((n,))],
    )(g_in, g_out, g_pw)


def _chip_exchange(p_in, p_out, p_pw):
    def body(pi, po, pp, li, lo, lp, send_sems, recv_sems):
        x, y, c = _me()
        chips = [(1 - x, y), (x, 1 - y), (1 - x, 1 - y)]
        cps = []
        for j, chip in enumerate(chips):
            q = 2 * chip[0] + chip[1]
            to = (*chip, c)
            for w, (src, dst) in enumerate(((pi.at[:, pl.ds(q * NB, NB)], li.at[j]), (po.at[q], lo.at[j]), (pp.at[q], lp.at[j]))):
                k = 3 * j + w
                cps.append(pltpu.make_async_remote_copy(src_ref=src, dst_ref=dst, send_sem=send_sems.at[k],
                                                        recv_sem=recv_sems.at[k], device_id=to, device_id_type=MESH))
        for cp in cps:
            cp.start()
        for cp in cps:
            cp.wait_recv()
        for cp in cps:
            cp.wait_send()

    return pl.pallas_call(
        body, name="grad_chip_exchange",
        out_shape=[S((3, p_in.shape[0], NB), p_in.dtype), S((3,) + p_out.shape[1:], p_out.dtype),
                   S((3,) + p_pw.shape[1:], p_pw.dtype)],
        in_specs=[_ANY] * 3, out_specs=[_ANY] * 3,
        scratch_shapes=[pltpu.SemaphoreType.DMA((9,)), pltpu.SemaphoreType.DMA((9,))],
    )(p_in, p_out, p_pw)


def _pair_share(r_in, r_out, r_pw):
    halves = (r_in, r_out, r_pw)

    def body(ri, ro, rp, fi, fo, fp, send_sems, recv_sems, local_sems):
        x, y, c = _me()
        sibling = (x, y, 1 - c)
        srcs, dsts = (ri, ro, rp), (fi, fo, fp)
        local = [pltpu.make_async_copy(srcs[w], dsts[w].at[c], local_sems.at[w]) for w in range(3)]
        for cp in local:
            cp.start()
        cps = [pltpu.make_async_remote_copy(src_ref=srcs[w], dst_ref=dsts[w].at[c], send_sem=send_sems.at[w],
                                            recv_sem=recv_sems.at[w], device_id=sibling, device_id_type=MESH) for w in range(3)]
        for cp in cps:
            cp.start()
        for w in range(3):
            pltpu.make_async_remote_copy(src_ref=srcs[w], dst_ref=dsts[w].at[1 - c], send_sem=send_sems.at[w],
                                         recv_sem=recv_sems.at[w], device_id=sibling, device_id_type=MESH).wait_recv()
        for cp in cps:
            cp.wait_send()
        for cp in local:
            cp.wait()

    return pl.pallas_call(
        body, name="grad_pair_share",
        out_shape=[S((2,) + h.shape, h.dtype) for h in halves],
        in_specs=[_ANY] * 3, out_specs=[_ANY] * 3,
        scratch_shapes=[pltpu.SemaphoreType.DMA((3,)), pltpu.SemaphoreType.DMA((3,)), pltpu.SemaphoreType.DMA((3,))],
    )(*halves)


def _pair_sum(g, l, sel, name):
    n, r, ccols = l.shape
    tr = min(256 if ccols <= D else 128, r)

    def body(sel_ref, g_ref, l_ref, o32_ref, o16_ref):
        v = g_ref[...] + l_ref[...]
        o32_ref[...] = v
        o16_ref[...] = v.astype(BF16)

    spec_l = pl.BlockSpec((None, tr, ccols), lambda j, i, s: (j, i, 0))
    return pl.pallas_call(
        body, name=name,
        grid_spec=pltpu.PrefetchScalarGridSpec(
            num_scalar_prefetch=1, grid=(n, r // tr),
            in_specs=[pl.BlockSpec((None, tr, ccols), lambda j, i, s: (2 * j + s[0], i, 0)), spec_l],
            out_specs=[spec_l, spec_l]),
        out_shape=[S(l.shape, F32), S(l.shape, BF16)],
        compiler_params=_cp(("parallel", "parallel"), 48),
    )(sel, g, l)


def _chip_sum(own, recv, name):
    r, ccols = own.shape
    tr = min(256, r)

    def body(o_ref, r_ref, out_ref):
        v = o_ref[...]
        for j in range(3):
            v = v + r_ref[j].astype(F32)
        out_ref[...] = v

    return pl.pallas_call(
        body, name=name, grid=(r // tr,),
        in_specs=[pl.BlockSpec((tr, ccols), lambda i: (i, 0)), pl.BlockSpec((3, tr, ccols), lambda i: (0, i, 0))],
        out_specs=pl.BlockSpec((tr, ccols), lambda i: (i, 0)),
        out_shape=S((r, ccols), F32),
        compiler_params=_cp(("parallel",), 48),
    )(own, recv)


def _ada_fwd(c_all, w_ada, b_ada_p):
    n = w_ada.shape[1]
    tn = 512

    def body(c_ref, w_ref, b_ref, o_ref):
        cv = c_ref[...]
        ca = cv * _sig(cv)
        o_ref[...] = jnp.dot(ca, w_ref[...], preferred_element_type=F32) + b_ref[...]

    return pl.pallas_call(
        body, name="ada_fwd", grid=(n // tn,),
        in_specs=[pl.BlockSpec((8, D), lambda j: (0, 0)), pl.BlockSpec((D, tn), lambda j: (0, j)),
                  pl.BlockSpec((1, tn), lambda j: (0, j))],
        out_specs=pl.BlockSpec((8, tn), lambda j: (0, j)),
        out_shape=S((8, n), F32),
        compiler_params=_cp(("parallel",), 32),
    )(c_all, w_ada, b_ada_p)


def _inproj(x, scale1p, shift, wg):
    t = x.shape[0]
    tm = min(512, t)

    def body(x_ref, sc_ref, sh_ref, w_ref, proj_ref, h_ref):
        @pl.when(pl.program_id(1) == 0)
        def _():
            h_ref[...] = (x_ref[...] * sc_ref[...] + sh_ref[...]).astype(BF16)

        proj_ref[...] = jnp.dot(h_ref[...], w_ref[...], preferred_element_type=F32).astype(BF16)

    return pl.pallas_call(
        body, name="inproj", grid=(t // tm, NCHIP),
        in_specs=[pl.BlockSpec((tm, D), lambda i, j: (i, 0)), pl.BlockSpec((1, D), lambda i, j: (0, 0)),
                  pl.BlockSpec((1, D), lambda i, j: (0, 0)), pl.BlockSpec((None, D, NB), lambda i, j: (j, 0, 0))],
        out_specs=[pl.BlockSpec((tm, NB), lambda i, j: (i, j)), pl.BlockSpec((tm, D), lambda i, j: (i, 0))],
        out_shape=[S((t, D_IN), BF16), S((t, D), BF16)],
        compiler_params=_cp(("parallel", "arbitrary"), 48),
    )(x, scale1p, shift, wg)


def _build_bias(rel_ref, bk_ref, bias_ref):
    bk = bk_ref[...]
    for hd in range(NQ):
        acc = jnp.full((BLK, 2 * BLK), NEG, F32)
        for b in range(N_BUCKETS):
            acc = jnp.where(bk == b, rel_ref[b, hd], acc)
        bias_ref[hd] = acc


def _first_block_mask(n):
    kj = lax.broadcasted_iota(jnp.int32, (BLK, 2 * BLK), 1)
    return jnp.where((n == 0) & (kj < BLK), NEG, 0.0).astype(F32)


def _attn_fwd(proj, rel_bias, sinks, bucket):
    t = proj.shape[0]

    def body(rel_ref, sink_ref, bk_ref, q_ref, kvc_ref, kvp_ref, glo_ref, ghi_ref, ya_ref, o_ref, lse_ref, bias_ref):
        n = pl.program_id(0)

        @pl.when(n == 0)
        def _():
            _build_bias(rel_ref, bk_ref, bias_ref)

        first = _first_block_mask(n)
        kv = jnp.concatenate([kvp_ref[...], kvc_ref[...]], axis=0)
        for h in range(NKV):
            k_h = kv[:, h * HD:(h + 1) * HD]
            v_h = kv[:, NKV * HD + h * HD:NKV * HD + (h + 1) * HD]
            for g in range(GRP):
                hd = GRP * h + g
                cols = slice(hd * HD, (hd + 1) * HD)
                s = lax.dot_general(q_ref[:, cols], k_h, (((1,), (1,)), ((), ())), preferred_element_type=F32)
                s = s * SCALE + bias_ref[hd] + first
                sink = sink_ref[0, hd]
                m = jnp.maximum(jnp.max(s, axis=-1, keepdims=True), sink)
                p = jnp.exp(s - m)
                l = jnp.sum(p, axis=-1, keepdims=True) + jnp.exp(sink - m)
                o = jnp.dot(p.astype(BF16), v_h, preferred_element_type=F32) / l
                lse_ref[:, hd:hd + 1] = m + jnp.log(l)
                g_ref = glo_ref if hd < NQ // 2 else ghi_ref
                gc = slice((hd % (NQ // 2)) * HD, (hd % (NQ // 2) + 1) * HD)
                gt = g_ref[:, gc].astype(F32)
                o_ref[:, cols] = o.astype(BF16)
                ya_ref[:, cols] = (o * (gt * _sig(gt))).astype(BF16)

    blk = lambda w, cidx: pl.BlockSpec((BLK, w), lambda n: (n, cidx))
    return pl.pallas_call(
        body, name="attn_fwd", grid=(t // BLK,),
        in_specs=[_SMEM, _SMEM, pl.BlockSpec((BLK, 2 * BLK), lambda n: (0, 0)),
                  blk(DA, 0), blk(512, 2), pl.BlockSpec((BLK, 512), lambda n: (jnp.maximum(n - 1, 0), 2)),
                  blk(512, 3), blk(512, 4)],
        out_specs=[blk(DA, 0), blk(DA, 0), pl.BlockSpec((BLK, NQ), lambda n: (n, 0))],
        out_shape=[S((t, DA), BF16), S((t, DA), BF16), S((t, NQ), F32)],
        scratch_shapes=[pltpu.VMEM((NQ, BLK, 2 * BLK), F32)],
        compiler_params=_cp(("arbitrary",), 32),
    )(rel_bias, sinks, bucket, proj, proj, proj, proj, proj)


def _conv_rows(t):
    return min(256, t)


def _glu_into(u_scr, i, a_refs, b_refs, ah_refs, bh_refs, tc):
    for cc in range(DA // 128):
        half, lc = cc // 4, slice((cc % 4) * 128, (cc % 4 + 1) * 128)
        uh = ah_refs[half][:, lc].astype(F32) * _sig(bh_refs[half][:, lc].astype(F32))
        u_scr[cc, 0:HALO, :] = jnp.where(i == 0, 0.0, uh)
        u_scr[cc, HALO:HALO + tc, :] = a_refs[half][:, lc].astype(F32) * _sig(b_refs[half][:, lc].astype(F32))


def _conv_fwd(proj, cwb, conv_b, ln_g, ln_b, wpw, b_pw):
    t = proj.shape[0]
    tc = _conv_rows(t)
    rc = min(128, tc)

    def body(alo, ahi, blo, bhi, alo_h, ahi_h, blo_h, bhi_h, glo, ghi, cw_ref, cb_ref, lg_ref, lb_ref, wpw_ref, bpw_ref,
             uc_ref, pw_ref, yc_ref, u_scr):
        i = pl.program_id(0)
        _glu_into(u_scr, i, (alo, ahi), (blo, bhi), (alo_h, ahi_h), (blo_h, bhi_h), tc)

        def rows(r, carry):
            r0 = pl.multiple_of(r * rc, rc)
            for cc in range(DA // 128):
                lanes = pl.ds(cc * 128, 128)
                acc = jnp.zeros((rc // 8, 8, 128), F32)
                for k in range(CW):
                    u = u_scr[cc, pl.ds(r0 + (HALO - CW + 1) + k, rc), :].reshape(rc // 8, 8, 128)
                    acc = acc + u * cw_ref[k, :, lanes]
                uc_ref[pl.ds(r0, rc), lanes] = acc.reshape(rc, 128) + cb_ref[:, lanes]
            return carry

        lax.fori_loop(0, tc // rc, rows, 0)

        uc = uc_ref[...]
        mu = jnp.mean(uc, axis=-1, keepdims=True)
        xc = uc - mu
        rstd = lax.rsqrt(jnp.mean(xc * xc, axis=-1, keepdims=True) + LN_EPS)
        ln = xc * rstd * lg_ref[...] + lb_ref[...]
        sw = (ln * _sig(ln)).astype(BF16)
        pw = jnp.dot(sw, wpw_ref[...], preferred_element_type=F32) + bpw_ref[...]
        pw_ref[...] = pw.astype(BF16)
        gt = jnp.concatenate([glo[...], ghi[...]], axis=1).astype(F32)
        yc_ref[...] = (pw * (gt * _sig(gt))).astype(BF16)

    nh = tc // HALO
    cur = lambda cidx: pl.BlockSpec((tc, 512), lambda i: (i, cidx))
    halo = lambda cidx: pl.BlockSpec((HALO, 512), lambda i: (jnp.maximum(i * nh - 1, 0), cidx))
    row = pl.BlockSpec((1, DA), lambda i: (0, 0))
    full = pl.BlockSpec((tc, DA), lambda i: (i, 0))
    return pl.pallas_call(
        body, name="conv_fwd", grid=(t // tc,),
        in_specs=[cur(5), cur(6), cur(7), cur(8), halo(5), halo(6), halo(7), halo(8), cur(9), cur(10),
                  pl.BlockSpec((CW, 8, DA), lambda i: (0, 0, 0)), row, row, row,
                  pl.BlockSpec((DA, DA), lambda i: (0, 0)), row],
        out_specs=[full, full, full],
        out_shape=[S((t, DA), F32), S((t, DA), BF16), S((t, DA), BF16)],
        scratch_shapes=[pltpu.VMEM((DA // 128, HALO + tc, 128), F32)],
        compiler_params=_cp(("arbitrary",), 48),
    )(*([proj] * 10), cwb, conv_b, ln_g, ln_b, wpw, b_pw)


def _outproj_loss(ya, yc, w_out, x, target, gate, ln_g, ln_b):
    t = x.shape[0]
    tm = min(256, t)

    def body(ya_ref, yc_ref, w_ref, x_ref, t_ref, g_ref, lg_ref, lb_ref, gx_ref, dy_ref, st_ref):
        @pl.when(pl.program_id(0) == 0)
        def _():
            st_ref[...] = jnp.zeros_like(st_ref)

        y = jnp.dot(ya_ref[...], w_ref[0:DA, :], preferred_element_type=F32)
        y = y + jnp.dot(yc_ref[...], w_ref[DA:2 * DA, :], preferred_element_type=F32)
        gate_v = g_ref[...]
        z = ALPHA * x_ref[...] + gate_v * y
        mu = jnp.mean(z, axis=-1, keepdims=True)
        zc = z - mu
        rstd = lax.rsqrt(jnp.mean(zc * zc, axis=-1, keepdims=True) + LN_EPS)
        zh = zc * rstd
        diff = zh * lg_ref[...] + lb_ref[...] - t_ref[...]
        dout = diff * (1.0 / D)
        dzh = dout * lg_ref[...]
        m1 = jnp.mean(dzh, axis=-1, keepdims=True)
        m2 = jnp.mean(dzh * zh, axis=-1, keepdims=True)
        dz = rstd * (dzh - m1 - zh * m2)
        gx_ref[...] = ALPHA * dz
        dy_ref[...] = (dz * gate_v).astype(BF16)
        st_ref[0:1, :] += jnp.sum(dout * zh, axis=0, keepdims=True)
        st_ref[1:2, :] += jnp.sum(dout, axis=0, keepdims=True)
        st_ref[2:3, :] += jnp.sum(dz * y, axis=0, keepdims=True)
        st_ref[3:4, :] += jnp.sum(diff * diff, axis=0, keepdims=True) * (0.5 / D)

        @pl.when(pl.program_id(0) == t // tm - 1)
        def _():
            st_ref[3:4, :] = jnp.broadcast_to(jnp.sum(st_ref[3:4, :], axis=-1, keepdims=True), (1, D))

    row = pl.BlockSpec((1, D), lambda i: (0, 0))
    half = pl.BlockSpec((tm, DA), lambda i: (i, 0))
    full = pl.BlockSpec((tm, D), lambda i: (i, 0))
    return pl.pallas_call(
        body, name="outproj_loss", grid=(t // tm,),
        in_specs=[half, half, pl.BlockSpec((D, D), lambda i: (0, 0)), full, full, row, row, row],
        out_specs=[full, full, pl.BlockSpec((8, D), lambda i: (0, 0))],
        out_shape=[S((t, D), F32), S((t, D), BF16), S((8, D), F32)],
        compiler_params=_cp(("arbitrary",), 56),
    )(ya, yc, w_out, x, target, gate, ln_g, ln_b)


def _dycat_gates(dy, w_out, attn_o, pw, proj):
    t = dy.shape[0]
    tm = min(256, t)

    def body(dy_ref, w_ref, o_ref, pw_ref, galo, gahi, gclo, gchi, dao_ref, dga_ref, dpw_ref, dgc_ref):
        dyc = lax.dot_general(dy_ref[...], w_ref[...], (((1,), (1,)), ((), ())), preferred_element_type=F32)
        da, dc = dyc[:, 0:DA], dyc[:, DA:2 * DA]
        ga = jnp.concatenate([galo[...], gahi[...]], axis=1).astype(F32)
        sa = _sig(ga)
        dao_ref[...] = (da * (ga * sa)).astype(BF16)
        dga_ref[...] = (da * o_ref[...].astype(F32) * _dsilu(ga, sa)).astype(BF16)
        gc = jnp.concatenate([gclo[...], gchi[...]], axis=1).astype(F32)
        sc = _sig(gc)
        dpw_ref[...] = (dc * (gc * sc)).astype(BF16)
        dgc_ref[...] = (dc * pw_ref[...].astype(F32) * _dsilu(gc, sc)).astype(BF16)

    half = pl.BlockSpec((tm, DA), lambda i: (i, 0))
    cur = lambda cidx: pl.BlockSpec((tm, 512), lambda i: (i, cidx))
    return pl.pallas_call(
        body, name="dycat_gates", grid=(t // tm,),
        in_specs=[pl.BlockSpec((tm, D), lambda i: (i, 0)), pl.BlockSpec((D, D), lambda i: (0, 0)), half, half,
                  cur(3), cur(4), cur(9), cur(10)],
        out_specs=[half] * 4,
        out_shape=[S((t, DA), BF16)] * 4,
        compiler_params=_cp(("parallel",), 48),
    )(dy, w_out, attn_o, pw, proj, proj, proj, proj)


def _conv_bwd_ln(dpw, wpw, uc, ln_g, ln_b):
    t = dpw.shape[0]
    tc = min(256, t)

    def body(dpw_ref, w_ref, uc_ref, lg_ref, lb_ref, duc_ref, sw_ref, st_ref):
        @pl.when(pl.program_id(0) == 0)
        def _():
            st_ref[...] = jnp.zeros_like(st_ref)

        dpw_v = dpw_ref[...]
        ds = lax.dot_general(dpw_v, w_ref[...], (((1,), (1,)), ((), ())), preferred_element_type=F32)
        uc = uc_ref[...]
        mu = jnp.mean(uc, axis=-1, keepdims=True)
        xc = uc - mu
        rstd = lax.rsqrt(jnp.mean(xc * xc, axis=-1, keepdims=True) + LN_EPS)
        uh = xc * rstd
        ln = uh * lg_ref[...] + lb_ref[...]
        sg = _sig(ln)
        sw_ref[...] = (ln * sg).astype(BF16)
        dln = ds * _dsilu(ln, sg)
        dxh = dln * lg_ref[...]
        m1 = jnp.mean(dxh, axis=-1, keepdims=True)
        m2 = jnp.mean(dxh * uh, axis=-1, keepdims=True)
        duc = rstd * (dxh - m1 - uh * m2)
        duc_ref[...] = duc
        st_ref[0:1, :] += jnp.sum(dln * uh, axis=0, keepdims=True)
        st_ref[1:2, :] += jnp.sum(dln, axis=0, keepdims=True)
        st_ref[2:3, :] += jnp.sum(duc, axis=0, keepdims=True)
        st_ref[3:4, :] += jnp.sum(dpw_v.astype(F32), axis=0, keepdims=True)

    row = pl.BlockSpec((1, DA), lambda i: (0, 0))
    full = pl.BlockSpec((tc, DA), lambda i: (i, 0))
    return pl.pallas_call(
        body, name="conv_bwd_ln", grid=(t // tc,),
        in_specs=[full, pl.BlockSpec((DA, DA), lambda i: (0, 0)), full, row, row],
        out_specs=[full, full, pl.BlockSpec((8, DA), lambda i: (0, 0))],
        out_shape=[S((t, DA), F32), S((t, DA), BF16), S((8, DA), F32)],
        compiler_params=_cp(("arbitrary",), 48),
    )(dpw, wpw, uc, ln_g, ln_b)


def _conv_bwd_dw(duc, proj, cwb):
    t = duc.shape[0]
    tc = _conv_rows(t)
    rc = min(128, tc)
    nt = t // tc
    off = HALO - CW + 1

    def body(dcur, dnext, alo, ahi, blo, bhi, alo_h, ahi_h, blo_h, bhi_h, cw_ref, da_ref, db_ref, gw_ref, u_scr, d_scr, g_scr):
        i = pl.program_id(0)

        @pl.when(i == 0)
        def _():
            g_scr[...] = jnp.zeros_like(g_scr)

        _glu_into(u_scr, i, (alo, ahi), (blo, bhi), (alo_h, ahi_h), (blo_h, bhi_h), tc)
        for cc in range(DA // 128):
            d_scr[cc, 0:tc, :] = dcur[:, cc * 128:(cc + 1) * 128]
            d_scr[cc, tc:tc + HALO, :] = jnp.where(i == nt - 1, 0.0, dnext[:, cc * 128:(cc + 1) * 128])

        def rows(r, carry):
            r0 = pl.multiple_of(r * rc, rc)
            for cc in range(DA // 128):
                lanes = pl.ds(cc * 128, 128)
                acc = jnp.zeros((rc // 8, 8, 128), F32)
                for j in range(CW):
                    dv = d_scr[cc, pl.ds(r0 + j, rc), :].reshape(rc // 8, 8, 128)
                    acc = acc + dv * cw_ref[CW - 1 - j, :, lanes]
                du = acc.reshape(rc, 128)
                d0 = d_scr[cc, pl.ds(r0, rc), :].reshape(rc // 8, 8, 128)
                for k in range(CW):
                    u = u_scr[cc, pl.ds(r0 + off + k, rc), :].reshape(rc // 8, 8, 128)
                    g_scr[k, :, lanes] += jnp.sum(d0 * u, axis=0)
                a_ref, b_ref = (alo, blo) if cc < 4 else (ahi, bhi)
                lc = pl.ds((cc % 4) * 128, 128)
                av = a_ref[pl.ds(r0, rc), lc].astype(F32)
                sb = _sig(b_ref[pl.ds(r0, rc), lc].astype(F32))
                da_ref[pl.ds(r0, rc), lanes] = (du * sb).astype(BF16)
                db_ref[pl.ds(r0, rc), lanes] = (du * av * sb * (1.0 - sb)).astype(BF16)
            return carry

        lax.fori_loop(0, tc // rc, rows, 0)

        @pl.when(i == nt - 1)
        def _():
            gw_ref[0:CW, :] = jnp.sum(g_scr[...], axis=1)
            gw_ref[CW:32, :] = jnp.zeros((32 - CW, DA), F32)

    nh = tc // HALO
    nhb = t // HALO
    cur = lambda cidx: pl.BlockSpec((tc, 512), lambda i: (i, cidx))
    halo = lambda cidx: pl.BlockSpec((HALO, 512), lambda i: (jnp.maximum(i * nh - 1, 0), cidx))
    full = pl.BlockSpec((tc, DA), lambda i: (i, 0))
    return pl.pallas_call(
        body, name="conv_bwd_dw", grid=(nt,),
        in_specs=[full, pl.BlockSpec((HALO, DA), lambda i: (jnp.minimum((i + 1) * nh, nhb - 1), 0)),
                  cur(5), cur(6), cur(7), cur(8), halo(5), halo(6), halo(7), halo(8),
                  pl.BlockSpec((CW, 8, DA), lambda i: (0, 0, 0))],
        out_specs=[full, full, pl.BlockSpec((32, DA), lambda i: (0, 0))],
        out_shape=[S((t, DA), BF16), S((t, DA), BF16), S((32, DA), F32)],
        scratch_shapes=[pltpu.VMEM((DA // 128, HALO + tc, 128), F32), pltpu.VMEM((DA // 128, tc + HALO, 128), F32),
                        pltpu.VMEM((CW, 8, DA), F32)],
        compiler_params=_cp(("arbitrary",), 48),
    )(duc, duc, *([proj] * 8), cwb)


def _attn_bwd(proj, dao, attn_o, lse, rel_bias, sinks, bucket):
    t = proj.shape[0]

    def body(rel_ref, sink_ref, bk_ref, q_ref, kvc_ref, kvp_ref, do_ref, o_ref, lse_ref,
             dq_ref, dcur_ref, dprev_ref, dsacc_ref, dsk_ref, bias_ref):
        n = pl.program_id(0)

        @pl.when(n == 0)
        def _():
            _build_bias(rel_ref, bk_ref, bias_ref)
            dsacc_ref[...] = jnp.zeros_like(dsacc_ref)
            dsk_ref[...] = jnp.zeros_like(dsk_ref)

        first = _first_block_mask(n)
        kv = jnp.concatenate([kvp_ref[...], kvc_ref[...]], axis=0)
        for h in range(NKV):
            k_h = kv[:, h * HD:(h + 1) * HD]
            v_h = kv[:, NKV * HD + h * HD:NKV * HD + (h + 1) * HD]
            dk = jnp.zeros((2 * BLK, HD), F32)
            dv = jnp.zeros((2 * BLK, HD), F32)
            for g in range(GRP):
                hd = GRP * h + g
                cols = slice(hd * HD, (hd + 1) * HD)
                q = q_ref[:, cols]
                do = do_ref[:, cols]
                lse_h = lse_ref[:, hd:hd + 1]
                s = lax.dot_general(q, k_h, (((1,), (1,)), ((), ())), preferred_element_type=F32)
                p = jnp.exp(s * SCALE + bias_ref[hd] + first - lse_h)
                dp = lax.dot_general(do, v_h, (((1,), (1,)), ((), ())), preferred_element_type=F32)
                delta = jnp.sum(do.astype(F32) * o_ref[:, cols].astype(F32), axis=-1, keepdims=True)
                ds = p * (dp - delta)
                dsacc_ref[hd] += ds
                dsk_ref[:, hd:hd + 1] += -jnp.exp(sink_ref[0, hd] - lse_h) * delta
                dsb = ds.astype(BF16)
                dq_ref[:, cols] = (jnp.dot(dsb, k_h, preferred_element_type=F32) * SCALE).astype(BF16)
                dk = dk + lax.dot_general(dsb, q, (((0,), (0,)), ((), ())), preferred_element_type=F32)
                dv = dv + lax.dot_general(p.astype(BF16), do, (((0,), (0,)), ((), ())), preferred_element_type=F32)
            dk = dk * SCALE
            kc = slice(h * HD, (h + 1) * HD)
            vc = slice(NKV * HD + h * HD, NKV * HD + (h + 1) * HD)
            dprev_ref[:, kc] = dk[0:BLK]
            dcur_ref[:, kc] = dk[BLK:2 * BLK]
            dprev_ref[:, vc] = dv[0:BLK]
            dcur_ref[:, vc] = dv[BLK:2 * BLK]

        @pl.when(n == t // BLK - 1)
        def _():
            dsk_ref[0:1, :] = jnp.sum(dsk_ref[...], axis=0, keepdims=True)

    blk = lambda w, cidx: pl.BlockSpec((BLK, w), lambda n: (n, cidx))
    return pl.pallas_call(
        body, name="attn_bwd", grid=(t // BLK,),
        in_specs=[_SMEM, _SMEM, pl.BlockSpec((BLK, 2 * BLK), lambda n: (0, 0)),
                  blk(DA, 0), blk(512, 2), pl.BlockSpec((BLK, 512), lambda n: (jnp.maximum(n - 1, 0), 2)),
                  blk(DA, 0), blk(DA, 0), pl.BlockSpec((BLK, NQ), lambda n: (n, 0))],
        out_specs=[blk(DA, 0), blk(512, 0), blk(512, 0),
                   pl.BlockSpec((NQ, BLK, 2 * BLK), lambda n: (0, 0, 0)), pl.BlockSpec((BLK, NQ), lambda n: (0, 0))],
        out_shape=[S((t, DA), BF16), S((t, 512), F32), S((t, 512), F32), S((NQ, BLK, 2 * BLK), F32), S((BLK, NQ), F32)],
        scratch_shapes=[pltpu.VMEM((NQ, BLK, 2 * BLK), F32)],
        compiler_params=_cp(("arbitrary",), 40),
    )(rel_bias, sinks, bucket, proj, proj, proj, dao, attn_o, lse)


def _dkv_fix(dcur, dprev):
    t = dcur.shape[0]
    nb = t // BLK

    def body(c_ref, p_ref, o_ref):
        nxt = jnp.where(pl.program_id(0) == nb - 1, 0.0, p_ref[...])
        o_ref[...] = (c_ref[...] + nxt).astype(BF16)

    return pl.pallas_call(
        body, name="dkv_fix", grid=(nb,),
        in_specs=[pl.BlockSpec((BLK, 512), lambda n: (n, 0)), pl.BlockSpec((BLK, 512), lambda n: (jnp.minimum(n + 1, nb - 1), 0))],
        out_specs=pl.BlockSpec((BLK, 512), lambda n: (n, 0)),
        out_shape=S((t, 512), BF16),
        compiler_params=_cp(("parallel",), 32),
    )(dcur, dprev)


def _bias_grad(dsacc, bucket):
    def body(ds_ref, bk_ref, o_ref, row_scr):
        bk = bk_ref[...]

        def head(hd, carry):
            dsv = ds_ref[hd]
            for b in range(N_BUCKETS):
                row_scr[hd, b:b + 1, :] = jnp.sum(jnp.where(bk == b, dsv, 0.0), axis=0, keepdims=True)
            return carry

        lax.fori_loop(0, NQ, head, 0)
        for hd in range(NQ):
            o_ref[hd] = jnp.sum(row_scr[hd], axis=-1, keepdims=True)

    return pl.pallas_call(
        body, name="bias_grad", out_shape=S((NQ, N_BUCKETS, 1), F32),
        in_specs=[_VMEM, _VMEM], out_specs=_VMEM,
        scratch_shapes=[pltpu.VMEM((NQ, N_BUCKETS, 2 * BLK), F32)],
        compiler_params=_cp(None, 32),
    )(dsacc, bucket)


def _dh_gradx(dproj, wg, gx0, x, scale1p):
    t = x.shape[0]
    tm = min(512, t)

    def body(dp_ref, w_ref, gx_ref, x_ref, sc_ref, out_ref, st_ref, acc_ref):
        i, j = pl.program_id(0), pl.program_id(1)

        @pl.when((i == 0) & (j == 0))
        def _():
            st_ref[...] = jnp.zeros_like(st_ref)

        part = lax.dot_general(dp_ref[...], w_ref[...], (((1,), (1,)), ((), ())), preferred_element_type=F32)

        @pl.when(j == 0)
        def _():
            acc_ref[...] = part

        @pl.when(j > 0)
        def _():
            acc_ref[...] += part

        @pl.when(j == NCHIP - 1)
        def _():
            dh = acc_ref[...]
            out_ref[...] = gx_ref[...] + dh * sc_ref[...]
            st_ref[0:1, :] += jnp.sum(dh, axis=0, keepdims=True)
            st_ref[1:2, :] += jnp.sum(dh * x_ref[...], axis=0, keepdims=True)

    full = pl.BlockSpec((tm, D), lambda i, j: (i, 0))
    return pl.pallas_call(
        body, name="dh_gradx", grid=(t // tm, NCHIP),
        in_specs=[pl.BlockSpec((tm, NB), lambda i, j: (i, j)), pl.BlockSpec((None, D, NB), lambda i, j: (j, 0, 0)),
                  full, full, pl.BlockSpec((1, D), lambda i, j: (0, 0))],
        out_specs=[full, pl.BlockSpec((8, D), lambda i, j: (0, 0))],
        out_shape=[S((t, D), F32), S((8, D), F32)],
        scratch_shapes=[pltpu.VMEM((tm, D), F32)],
        compiler_params=_cp(("arbitrary", "arbitrary"), 56),
    )(dproj, wg, gx0, x, scale1p)


def _atb(a, b, bn, name):
    t, m = a.shape
    n = b.shape[1]
    tk = min(512, t)
    nk = t // tk

    def body(a_ref, b_ref, o_ref):
        part = lax.dot_general(a_ref[...], b_ref[...], (((0,), (0,)), ((), ())), preferred_element_type=F32)

        @pl.when(pl.program_id(1) == 0)
        def _():
            o_ref[...] = part

        @pl.when(pl.program_id(1) > 0)
        def _():
            o_ref[...] += part

    return pl.pallas_call(
        body, name=name, grid=(n // bn, nk),
        in_specs=[pl.BlockSpec((tk, m), lambda j, k: (k, 0)), pl.BlockSpec((tk, bn), lambda j, k: (k, j))],
        out_specs=pl.BlockSpec((m, bn), lambda j, k: (0, j)),
        out_shape=S((m, n), F32),
        compiler_params=_cp(("parallel", "arbitrary"), 56),
    )(a, b)


def _sum8(parts):
    _, r, n = parts.shape

    def body(p_ref, o_ref):
        v = p_ref[0]
        for d in range(1, 8):
            v = v + p_ref[d]
        o_ref[...] = v

    return pl.pallas_call(body, name="sum8", out_shape=S((r, n), F32), in_specs=[_VMEM], out_specs=_VMEM,
                          compiler_params=_cp(None, 32))(parts)


def _adam_math(w, g, m, v):
    m = B1 * m + (1.0 - B1) * g
    v = B2 * v + (1.0 - B2) * (g * g)
    m_hat = m / (1.0 - B1 ** STEP)
    v_hat = v / (1.0 - B2 ** STEP)
    delta = -LR * (m_hat / (jnp.sqrt(v_hat) + EPS) + WD * w)
    return delta, m, v


def _adamw(w, g, m, v, name):
    r, n = w.shape
    tr = min(256, r)

    def body(w_ref, g_ref, m_ref, v_ref, d_ref, nm_ref, nv_ref):
        d_ref[...], nm_ref[...], nv_ref[...] = _adam_math(w_ref[...], g_ref[...], m_ref[...], v_ref[...])

    spec = pl.BlockSpec((tr, n), lambda i: (i, 0))
    return pl.pallas_call(
        body, name=name, grid=(r // tr,), in_specs=[spec] * 4, out_specs=[spec] * 3,
        out_shape=[S((r, n), F32)] * 3, compiler_params=_cp(("parallel",), 48),
    )(w, g, m, v)


def _adamw_ada(ct, dmod_p, w, m, v):
    r, n = w.shape
    tr = min(256, r)

    def body(ct_ref, dm_ref, w_ref, m_ref, v_ref, g_ref, d_ref, nm_ref, nv_ref):
        cv = ct_ref[...]
        g = jnp.dot(cv * _sig(cv), dm_ref[...], preferred_element_type=F32)
        g_ref[...] = g
        d_ref[...], nm_ref[...], nv_ref[...] = _adam_math(w_ref[...], g, m_ref[...], v_ref[...])

    spec = pl.BlockSpec((tr, n), lambda i: (i, 0))
    return pl.pallas_call(
        body, name="adamw_ada", grid=(r // tr,),
        in_specs=[pl.BlockSpec((tr, 8), lambda i: (i, 0)), pl.BlockSpec((8, n), lambda i: (0, 0)), spec, spec, spec],
        out_specs=[spec] * 4, out_shape=[S((r, n), F32)] * 4, compiler_params=_cp(("parallel",), 48),
    )(ct, dmod_p, w, m, v)


def _adamw_small(ws, gs, ms, vs):
    k = len(ws)

    def body(*refs):
        ins, outs = refs[:4 * k], refs[4 * k:]
        for i in range(k):
            d, nm, nv = _adam_math(ins[i][...], ins[k + i][...], ins[2 * k + i][...], ins[3 * k + i][...])
            outs[i][...] = d
            outs[k + i][...] = nm
            outs[2 * k + i][...] = nv

    shapes = [S(w.shape, F32) for w in ws]
    return pl.pallas_call(body, name="adamw_small", out_shape=shapes * 3, in_specs=[_VMEM] * (4 * k),
                          out_specs=[_VMEM] * (3 * k), compiler_params=_cp(None, 32))(*ws, *gs, *ms, *vs)


def _bucket_map():
    qi = jnp.arange(BLK, dtype=jnp.int32)[:, None]
    kj = jnp.arange(2 * BLK, dtype=jnp.int32)[None, :]
    dist = qi + BLK - kj
    dd = jnp.maximum(dist, 0)
    max_exact = N_BUCKETS // 2
    dfl = jnp.maximum(dd, 1).astype(F32)
    large = max_exact + (jnp.log(dfl / max_exact) / math.log(MAX_DIST / max_exact) * (N_BUCKETS - max_exact)).astype(jnp.int32)
    large = jnp.minimum(large, N_BUCKETS - 1)
    bucket = jnp.where(dd < max_exact, dd, large)
    return jnp.where((dist >= 0) & (dist < BLK), bucket, -1).astype(jnp.int32)


def _pad_rows(a, rows):
    return jnp.pad(a, ((0, rows - a.shape[0]), (0, 0)))


def kernel(x, c, w_ada, b_ada, w_in, rel_bias, sinks, conv_w, conv_b, conv_ln_g, conv_ln_b, w_pw, b_pw, w_out, ln_g, ln_b, loss_target, m_w_ada, m_b_ada, m_w_in, m_rel_bias, m_sinks, m_conv_w, m_conv_b, m_conv_ln_g, m_conv_ln_b, m_w_pw, m_b_pw, m_w_out, m_ln_g, m_ln_b, v_w_ada, v_b_ada, v_w_in, v_rel_bias, v_sinks, v_conv_w, v_conv_b, v_conv_ln_g, v_conv_ln_b, v_w_pw, v_b_pw, v_w_out, v_ln_g, v_ln_b):
    mx, my, mc = _me()
    chip = 2 * mx + my
    dev = 2 * chip + mc
    t = x.shape[1]
    x2 = x.reshape(t, D)
    tgt = loss_target.reshape(t, D)
    n_ada = w_ada.shape[2]
    cw_cols = conv_w.shape[2]

    pack0 = jnp.concatenate([_pad_rows(c, 8), _pad_rows(_pad_rows(conv_w[0], 32).reshape(-1, D), 8)], axis=0)
    g0 = _allgather8(pack0, "gather_c_convw").reshape(8, 16, D)
    c_all = g0[:, 0, :]
    cw_rows = 32 * cw_cols // D
    cw_full = jnp.concatenate([g0[2 * q, 8:8 + cw_rows, :].reshape(32, cw_cols) for q in range(NCHIP)], axis=1)[:CW]
    cwb = jnp.broadcast_to(cw_full[:, None, :], (CW, 8, DA))

    b_ada_p = lax.dynamic_slice(b_ada, (0, chip * n_ada), (1, n_ada))
    mod_part = _ada_fwd(c_all, w_ada[0], b_ada_p)
    mod_all = _allgather8(mod_part, "gather_mod").reshape(8, 8, n_ada)
    mod = jnp.concatenate([lax.dynamic_slice(mod_all[2 * q], (dev, 0), (1, n_ada)) for q in range(NCHIP)], axis=1)
    shift, scale, gate = mod[:, 0:D], mod[:, D:2 * D], mod[:, 2 * D:3 * D]
    scale1p = 1.0 + scale

    wg_in, wg_out, wg_pw = _gather_weights(w_in[0].astype(BF16), w_out[0].astype(BF16), w_pw[0].astype(BF16))
    w_out_f = wg_out.reshape(D, D)
    w_pw_f = wg_pw.reshape(DA, DA)

    bucket = _bucket_map()
    proj, h = _inproj(x2, scale1p, shift, wg_in)
    ya, attn_o, lse = _attn_fwd(proj, rel_bias, sinks, bucket)
    uc, pw, yc = _conv_fwd(proj, cwb, conv_b, conv_ln_g, conv_ln_b, w_pw_f, b_pw)
    gx0, dy, st_out = _outproj_loss(ya, yc, w_out_f, x2, tgt, gate, ln_g, ln_b)

    dao, dga, dpw, dgc = _dycat_gates(dy, w_out_f, attn_o, pw, proj)
    duc, sw, st_conv = _conv_bwd_ln(dpw, w_pw_f, uc, conv_ln_g, conv_ln_b)
    dglu_a, dglu_b, gw_conv = _conv_bwd_dw(duc, proj, cwb)
    dq, dkv_cur, dkv_prev, dsacc, dsk = _attn_bwd(proj, dao, attn_o, lse, rel_bias, sinks, bucket)
    dkv = _dkv_fix(dkv_cur, dkv_prev)
    g_rel = _bias_grad(dsacc, bucket).reshape(NQ, N_BUCKETS).T
    dproj = jnp.concatenate([dq, dkv, dga, dglu_a, dglu_b, dgc], axis=1)
    grad_x, st_in = _dh_gradx(dproj, wg_in, gx0, x2, scale1p)
    gp_in = _atb(h, dproj, NB, "grad_w_in")
    gp_out = _atb(jnp.concatenate([ya, yc], axis=1), dy, 1024, "grad_w_out")
    gp_pw = _atb(sw, dpw, 1024, "grad_w_pw")

    r_out, r_pw = D // NCHIP // 2, DA // NCHIP // 2
    gi3, go3, gp3 = gp_in.reshape(2, D // 2, D_IN), gp_out.reshape(8, r_out, D), gp_pw.reshape(8, r_pw, DA)
    l_in, l_out, l_pw = _pair_exchange(gi3, go3, gp3)
    sel = jnp.reshape(mc, (1,)).astype(jnp.int32)
    pi32, pi16 = _pair_sum(gi3, l_in.reshape(1, D // 2, D_IN), sel, "pair_sum_in")
    po32, po16 = _pair_sum(go3, l_out, sel, "pair_sum_out")
    pp32, pp16 = _pair_sum(gp3, l_pw, sel, "pair_sum_pw")
    rc_in, rc_out, rc_pw = _chip_exchange(pi16[0], po16, pp16)
    own_in = lax.dynamic_slice(pi32[0], (0, chip * NB), (D // 2, NB))
    own_out = lax.dynamic_index_in_dim(po32, chip, 0, keepdims=False)
    own_pw = lax.dynamic_index_in_dim(pp32, chip, 0, keepdims=False)
    h_in = _chip_sum(own_in, rc_in, "chip_sum_in")
    h_out = _chip_sum(own_out, rc_out, "chip_sum_out")
    h_pw = _chip_sum(own_pw, rc_pw, "chip_sum_pw")
    f_in, f_out, f_pw = _pair_share(h_in, h_out, h_pw)
    g_w_in = f_in.reshape(D, NB)
    g_w_out = f_out.reshape(D // NCHIP, D)
    g_w_pw = f_pw.reshape(DA // NCHIP, DA)

    dmod = jnp.concatenate([st_in[0:1], st_in[1:2], st_out[2:3]], axis=1)
    loss_row = st_out[3:4, 0:1]
    small = jnp.concatenate([
        dmod, st_out[0:1], st_out[1:2],
        st_conv[0:1], st_conv[1:2], st_conv[2:3], st_conv[3:4],
        g_rel.reshape(1, N_BUCKETS * NQ), dsk[0:1], loss_row,
        gw_conv[:CW].reshape(1, CW * DA)], axis=1)
    n_small = small.shape[1]
    rows_small = -(-n_small // (8 * D)) * 8
    small = jnp.pad(small, ((0, 0), (0, rows_small * D - n_small))).reshape(rows_small, D)
    parts = _allgather8(small, "gather_small").reshape(8, rows_small, D)
    tot = _sum8(parts).reshape(1, rows_small * D)
    dmod_all = parts.reshape(8, rows_small * D)[:, 0:3 * D]

    o = 3 * D
    def take(nn):
        nonlocal o
        v = tot[:, o:o + nn]
        o += nn
        return v
    g_b_ada = tot[:, 0:3 * D]
    g_ln_g, g_ln_b = take(D), take(D)
    g_cln_g, g_cln_b, g_conv_b, g_b_pw = take(DA), take(DA), take(DA), take(DA)
    g_rel_bias = take(N_BUCKETS * NQ).reshape(N_BUCKETS, NQ)
    g_sinks = take(NQ)
    loss = take(1).reshape(())
    g_conv_w_full = take(CW * DA).reshape(CW, DA)
    g_conv_w = lax.dynamic_slice(g_conv_w_full, (0, chip * cw_cols), (CW, cw_cols))

    dmod_p = lax.dynamic_slice(dmod_all, (0, chip * n_ada), (8, n_ada))
    g_w_ada, d_w_ada, nm_w_ada, nv_w_ada = _adamw_ada(c_all.T, dmod_p, w_ada[0], m_w_ada[0], v_w_ada[0])
    d_w_in, nm_w_in, nv_w_in = _adamw(w_in[0], g_w_in, m_w_in[0], v_w_in[0], "adamw_in")
    d_w_out, nm_w_out, nv_w_out = _adamw(w_out[0], g_w_out, m_w_out[0], v_w_out[0], "adamw_out")
    d_w_pw, nm_w_pw, nv_w_pw = _adamw(w_pw[0], g_w_pw, m_w_pw[0], v_w_pw[0], "adamw_pw")
    small_w = [b_ada, rel_bias, sinks, conv_w[0], conv_b, conv_ln_g, conv_ln_b, b_pw, ln_g, ln_b]
    small_g = [g_b_ada, g_rel_bias, g_sinks, g_conv_w, g_conv_b, g_cln_g, g_cln_b, g_b_pw, g_ln_g, g_ln_b]
    small_m = [m_b_ada, m_rel_bias, m_sinks, m_conv_w[0], m_conv_b, m_conv_ln_g, m_conv_ln_b, m_b_pw, m_ln_g, m_ln_b]
    small_v = [v_b_ada, v_rel_bias, v_sinks, v_conv_w[0], v_conv_b, v_conv_ln_g, v_conv_ln_b, v_b_pw, v_ln_g, v_ln_b]
    res = _adamw_small(small_w, small_g, small_m, small_v)
    ns = len(small_w)
    d_s, nm_s, nv_s = res[:ns], res[ns:2 * ns], res[2 * ns:]

    def ordered(w_ada_, w_in_, w_pw_, w_out_, sm):
        b_ada_, rel_, sinks_, conv_w_, conv_b_, cln_g_, cln_b_, b_pw_, ln_g_, ln_b_ = sm
        return (w_ada_[None], b_ada_, w_in_[None], rel_, sinks_, conv_w_[None], conv_b_, cln_g_, cln_b_,
                w_pw_[None], b_pw_, w_out_[None], ln_g_, ln_b_)

    grads = ordered(g_w_ada, g_w_in, g_w_pw, g_w_out, small_g)
    deltas = ordered(d_w_ada, d_w_in, d_w_pw, d_w_out, d_s)
    new_m = ordered(nm_w_ada, nm_w_in, nm_w_pw, nm_w_out, nm_s)
    new_v = ordered(nv_w_ada, nv_w_in, nv_w_pw, nv_w_out, nv_s)
    return (loss, grad_x.reshape(1, t, D), *grads, *deltas, *new_m, *new_v)
```

```python
import functools
import math

import jax
import jax.numpy as jnp
from jax import lax
from jax.experimental import pallas as pl
from jax.experimental.pallas import tpu as pltpu

F32, BF16 = jnp.float32, jnp.bfloat16
S = jax.ShapeDtypeStruct
MESH = pl.DeviceIdType.MESH

D = 2048
DA = 1024
HD = 64
NQ, NKV, GRP = 16, 4, 4
BLK = 128
CW = 31
HALO = 32
D_IN = 5632
NCHIP = 4
NB = D_IN // NCHIP
N_BUCKETS, MAX_DIST = 32, 128
LN_EPS = 1e-5
ALPHA = 2.0 ** 0.25
SCALE = HD ** -0.5
NEG = -1e30
LR, B1, B2, EPS, WD, STEP = 0.001, 0.9, 0.999, 1e-08, 0.01, 10

_VMEM = pl.BlockSpec(memory_space=pltpu.VMEM)
_SMEM = pl.BlockSpec(memory_space=pltpu.SMEM)
_ANY = pl.BlockSpec(memory_space=pl.ANY)


def _cp(sem=None, vmem_mb=None):
    kw = {}
    if sem is not None:
        kw["dimension_semantics"] = sem
    if vmem_mb is not None:
        kw["vmem_limit_bytes"] = vmem_mb * 1024 * 1024
    return pltpu.CompilerParams(**kw)


def _sig(v):
    return jax.nn.sigmoid(v)


def _dsilu(g, sg):
    return sg * (1.0 + g * (1.0 - sg))


def _me():
    return lax.axis_index("x"), lax.axis_index("y"), lax.axis_index("c")


def _allgather8(x_shard, name):
    m_per, n = x_shard.shape

    def body(x_ref, out_ref, send_sems, recv_sems, local_sem):
        x, y, c = _me()
        me, sibling = (x, y, c), (x, y, 1 - c)
        chips = [(1 - x, y), (x, 1 - y), (1 - x, 1 - y)]

        def rows(px, py, pc):
            return out_ref.at[pl.ds((4 * px + 2 * py + pc) * m_per, m_per), :]

        def copy(k, block, to, src=None):
            return pltpu.make_async_remote_copy(
                src_ref=rows(*block) if src is None else src, dst_ref=rows(*block),
                send_sem=send_sems.at[k], recv_sem=recv_sems.at[k], device_id=to, device_id_type=MESH)

        mine = pltpu.make_async_copy(x_ref, rows(*me), local_sem)
        mine.start()
        first = [copy(0, me, sibling, src=x_ref)]
        first += [copy(1 + j, me, (*chip, c), src=x_ref) for j, chip in enumerate(chips)]
        for cp in first:
            cp.start()
        passed = [copy(4 + j, (*chip, c), sibling) for j, chip in enumerate(chips)]
        for j, chip in enumerate(chips):
            copy(1 + j, (*chip, c), me).wait_recv()
            passed[j].start()
        copy(0, sibling, me).wait_recv()
        for j, chip in enumerate(chips):
            copy(4 + j, (*chip, 1 - c), me).wait_recv()
        for cp in first + passed:
            cp.wait_send()
        mine.wait()

    return pl.pallas_call(
        body, name=name, out_shape=S((8 * m_per, n), x_shard.dtype),
        in_specs=[_VMEM], out_specs=_VMEM,
        scratch_shapes=[pltpu.SemaphoreType.DMA((7,)), pltpu.SemaphoreType.DMA((7,)), pltpu.SemaphoreType.DMA],
    )(x_shard)


def _inproj_gather(x, scale1p, shift, wi, wo, wp, chip_id):
    t = x.shape[0]
    tm = min(512, t)
    ni = t // tm
    shards = (wi, wo, wp)
    halves = [s.shape[0] // 2 for s in shards]

    def body(cid_ref, x_ref, sc_ref, sh_ref, wi_ref, wo_ref, wp_ref, proj_ref, h_ref, gi_ref, go_ref, gp_ref,
             wbuf, obuf, pbuf, send_sems, recv_sems, ld_sems, st_sems):
        jj, i = pl.program_id(0), pl.program_id(1)
        mx, my, c = _me()
        p = 2 * mx + my
        sibling = (mx, my, 1 - c)
        srcs, dsts, bufs = (wi_ref, wo_ref, wp_ref), (gi_ref, go_ref, gp_ref), (wbuf, obuf, pbuf)
        chips = {1: (mx, 1 - my), 2: (1 - mx, my), 3: (1 - mx, 1 - my)}

        def half(ref, w, hc):
            return ref.at[pl.ds(hc * halves[w], halves[w])]

        def copy(k, src, dst, to):
            return pltpu.make_async_remote_copy(src_ref=src, dst_ref=dst, send_sem=send_sems.at[k],
                                                recv_sem=recv_sems.at[k], device_id=to, device_id_type=MESH)

        def sent(w, m):
            return copy(3 * w + m - 1, half(srcs[w], w, c), half(dsts[w].at[p], w, c), (*chips[m], c))

        def landed(w, m):
            blk = half(dsts[w].at[jnp.bitwise_xor(p, m)], w, c)
            return copy(3 * w + m - 1, blk, blk, (*chips[m], c))

        def passed(w, m, hc):
            blk = half(dsts[w].at[jnp.bitwise_xor(p, m)], w, hc)
            return copy(9 + 3 * w + m - 1, blk, blk, sibling)

        def load(w, src):
            return pltpu.make_async_copy(src, bufs[w], ld_sems.at[w])

        def store(w):
            return pltpu.make_async_copy(bufs[w], dsts[w].at[p], st_sems.at[w])

        @pl.when((jj == 0) & (i == 0))
        def _():
            for w in range(3):
                for m in (1, 2, 3):
                    sent(w, m).start()
            for w in range(3):
                load(w, srcs[w]).start()
            for w in range(3):
                load(w, srcs[w]).wait()
                store(w).start()

        for m in (1, 2, 3):
            @pl.when((jj == m) & (i == 0))
            def _(m=m):
                if m == 1:
                    store(0).wait()
                landed(0, m).wait_recv()
                passed(0, m, c).start()
                passed(0, m, 1 - c).wait_recv()
                cp = load(0, gi_ref.at[jnp.bitwise_xor(p, m)])
                cp.start()
                cp.wait()

        hb = (x_ref[...] * sc_ref[...] + sh_ref[...]).astype(BF16)

        @pl.when(jj == 0)
        def _():
            h_ref[...] = hb

        proj_ref[...] = jnp.dot(hb, wbuf[...], preferred_element_type=F32).astype(BF16)

        @pl.when((jj == NCHIP - 1) & (i == ni - 1))
        def _():
            for w in (1, 2):
                for m in (1, 2, 3):
                    landed(w, m).wait_recv()
                    passed(w, m, c).start()
            for w in (1, 2):
                for m in (1, 2, 3):
                    passed(w, m, 1 - c).wait_recv()
            for w in range(3):
                for m in (1, 2, 3):
                    sent(w, m).wait_send()
                    passed(w, m, c).wait_send()
            store(1).wait()
            store(2).wait()

    row = pl.BlockSpec((1, D), lambda jj, i, s: (0, 0))
    return pl.pallas_call(
        body, name="inproj_gather",
        grid_spec=pltpu.PrefetchScalarGridSpec(
            num_scalar_prefetch=1, grid=(NCHIP, ni),
            in_specs=[pl.BlockSpec((tm, D), lambda jj, i, s: (i, 0)), row, row, _ANY, _ANY, _ANY],
            out_specs=[pl.BlockSpec((tm, NB), lambda jj, i, s: (i, jnp.bitwise_xor(s[0], jj))),
                       pl.BlockSpec((tm, D), lambda jj, i, s: (jnp.where(jj == 0, i, ni - 1), 0)),
                       _ANY, _ANY, _ANY],
            scratch_shapes=[pltpu.VMEM(wi.shape, BF16), pltpu.VMEM(wo.shape, BF16), pltpu.VMEM(wp.shape, BF16),
                            pltpu.SemaphoreType.DMA((18,)), pltpu.SemaphoreType.DMA((18,)),
                            pltpu.SemaphoreType.DMA((3,)), pltpu.SemaphoreType.DMA((3,))]),
        out_shape=[S((t, D_IN), BF16), S((t, D), BF16)] + [S((NCHIP,) + s.shape, s.dtype) for s in shards],
        compiler_params=_cp(("arbitrary", "arbitrary"), 48),
    )(chip_id, x, scale1p, shift, wi, wo, wp)


def _pair_exchange(g_in, g_out, g_pw):
    def body(gi, go, gp, li, lo, lp, send_sems, recv_sems):
        x, y, c = _me()
        sibling = (x, y, 1 - c)
        copies = [(gi.at[1 - c], li)]
        for j in range(NCHIP):
            copies.append((go.at[2 * j + 1 - c], lo.at[j]))
        for j in range(NCHIP):
            copies.append((gp.at[2 * j + 1 - c], lp.at[j]))
        cps = [pltpu.make_async_remote_copy(src_ref=s_, dst_ref=d_, send_sem=send_sems.at[k], recv_sem=recv_sems.at[k],
                                            device_id=sibling, device_id_type=MESH) for k, (s_, d_) in enumerate(copies)]
        for cp in cps:
            cp.start()
        for cp in cps:
            cp.wait_recv()
        for cp in cps:
            cp.wait_send()

    n = 1 + 2 * NCHIP
    return pl.pallas_call(
        body, name="grad_pair_exchange",
        out_shape=[S(g_in.shape[1:], g_in.dtype), S((NCHIP,) + g_out.shape[1:], g_out.dtype),
                   S((NCHIP,) + g_pw.shape[1:], g_pw.dtype)],
        in_specs=[_ANY] * 3, out_specs=[_ANY] * 3,
        scratch_shapes=[pltpu.SemaphoreType.DMA((n,)), pltpu.SemaphoreType.DMA((n,))],
    )(g_in, g_out, g_pw)


def _chip_exchange_copies(pi, po, pp, li, lo, lp, send_sems, recv_sems):
    x, y, c = _me()
    chips = [(1 - x, y), (x, 1 - y), (1 - x, 1 - y)]
    cps = []
    for j, chip in enumerate(chips):
        q = 2 * chip[0] + chip[1]
        pairs = ((pi.at[:, pl.ds(q * NB, NB)], li.at[j]), (po.at[q], lo.at[j]), (pp.at[q], lp.at[j]))
        for w, (src, dst) in enumerate(pairs):
            k = 3 * j + w
            cps.append(pltpu.make_async_remote_copy(src_ref=src, dst_ref=dst, send_sem=send_sems.at[k],
                                                    recv_sem=recv_sems.at[k], device_id=(*chip, c), device_id_type=MESH))
    return cps


def _pair_share(f_in, f_out, f_pw):
    bufs = (f_in, f_out, f_pw)

    def body(ai, ao, ap, fi, fo, fp, send_sems, recv_sems):
        x, y, c = _me()
        sibling = (x, y, 1 - c)

        def copy(w, ref, hc):
            return pltpu.make_async_remote_copy(src_ref=ref.at[hc], dst_ref=ref.at[hc], send_sem=send_sems.at[w],
                                                recv_sem=recv_sems.at[w], device_id=sibling, device_id_type=MESH)

        cps = [copy(w, ref, c) for w, ref in enumerate((fi, fo, fp))]
        for cp in cps:
            cp.start()
        for w, ref in enumerate((fi, fo, fp)):
            copy(w, ref, 1 - c).wait_recv()
        for cp in cps:
            cp.wait_send()

    return pl.pallas_call(
        body, name="grad_pair_share",
        out_shape=[S(b.shape, b.dtype) for b in bufs],
        in_specs=[_ANY] * 3, out_specs=[_ANY] * 3, input_output_aliases={0: 0, 1: 1, 2: 2},
        scratch_shapes=[pltpu.SemaphoreType.DMA((3,)), pltpu.SemaphoreType.DMA((3,))],
    )(*bufs)


def _pair_sum(g, l, sel, name):
    n, r, ccols = l.shape
    tr = min(256 if ccols <= D else 128, r)

    def body(sel_ref, g_ref, l_ref, o32_ref, o16_ref):
        v = g_ref[...] + l_ref[...]
        o32_ref[...] = v
        o16_ref[...] = v.astype(BF16)

    spec_l = pl.BlockSpec((None, tr, ccols), lambda j, i, s: (j, i, 0))
    return pl.pallas_call(
        body, name=name,
        grid_spec=pltpu.PrefetchScalarGridSpec(
            num_scalar_prefetch=1, grid=(n, r // tr),
            in_specs=[pl.BlockSpec((None, tr, ccols), lambda j, i, s: (2 * j + s[0], i, 0)), spec_l],
            out_specs=[spec_l, spec_l]),
        out_shape=[S(l.shape, F32), S(l.shape, BF16)],
        compiler_params=_cp(("parallel", "parallel"), 48),
    )(sel, g, l)


def _chip_sum(own, recv, sel, name):
    r, ccols = own.shape
    tr = min(256, r)

    def body(sel_ref, o_ref, r_ref, out_ref):
        v = o_ref[...]
        for j in range(3):
            v = v + r_ref[j].astype(F32)
        out_ref[...] = v

    return pl.pallas_call(
        body, name=name,
        grid_spec=pltpu.PrefetchScalarGridSpec(
            num_scalar_prefetch=1, grid=(r // tr,),
            in_specs=[pl.BlockSpec((tr, ccols), lambda i, s: (i, 0)), pl.BlockSpec((3, tr, ccols), lambda i, s: (0, i, 0))],
            out_specs=pl.BlockSpec((None, tr, ccols), lambda i, s: (s[0], i, 0))),
        out_shape=S((2, r, ccols), F32),
        compiler_params=_cp(("parallel",), 48),
    )(sel, own, recv)


def _ada_fwd(c_all, w_ada, b_ada_p):
    n = w_ada.shape[1]
    tn = 512

    def body(c_ref, w_ref, b_ref, o_ref):
        cv = c_ref[...]
        ca = cv * _sig(cv)
        o_ref[...] = jnp.dot(ca, w_ref[...], preferred_element_type=F32) + b_ref[...]

    return pl.pallas_call(
        body, name="ada_fwd", grid=(n // tn,),
        in_specs=[pl.BlockSpec((8, D), lambda j: (0, 0)), pl.BlockSpec((D, tn), lambda j: (0, j)),
                  pl.BlockSpec((1, tn), lambda j: (0, j))],
        out_specs=pl.BlockSpec((8, tn), lambda j: (0, j)),
        out_shape=S((8, n), F32),
        compiler_params=_cp(("parallel",), 32),
    )(c_all, w_ada, b_ada_p)


def _build_bias(rel_ref, bk_ref, bias_ref):
    bk = bk_ref[...]
    for hd in range(NQ):
        acc = jnp.full((BLK, 2 * BLK), NEG, F32)
        for b in range(N_BUCKETS):
            acc = jnp.where(bk == b, rel_ref[b, hd], acc)
        bias_ref[hd] = acc


def _first_block_mask(n):
    kj = lax.broadcasted_iota(jnp.int32, (BLK, 2 * BLK), 1)
    return jnp.where((n == 0) & (kj < BLK), NEG, 0.0).astype(F32)


def _attn_fwd(proj, rel_bias, sinks, bucket):
    t = proj.shape[0]

    def body(rel_ref, sink_ref, bk_ref, q_ref, kvc_ref, kvp_ref, glo_ref, ghi_ref, ya_ref, o_ref, lse_ref, bias_ref):
        n = pl.program_id(0)

        @pl.when(n == 0)
        def _():
            _build_bias(rel_ref, bk_ref, bias_ref)

        first = _first_block_mask(n)
        kv = jnp.concatenate([kvp_ref[...], kvc_ref[...]], axis=0)
        for h in range(NKV):
            k_h = kv[:, h * HD:(h + 1) * HD]
            v_h = kv[:, NKV * HD + h * HD:NKV * HD + (h + 1) * HD]
            for g in range(GRP):
                hd = GRP * h + g
                cols = slice(hd * HD, (hd + 1) * HD)
                s = lax.dot_general(q_ref[:, cols], k_h, (((1,), (1,)), ((), ())), preferred_element_type=F32)
                s = s * SCALE + bias_ref[hd] + first
                sink = sink_ref[0, hd]
                m = jnp.maximum(jnp.max(s, axis=-1, keepdims=True), sink)
                p = jnp.exp(s - m)
                l = jnp.sum(p, axis=-1, keepdims=True) + jnp.exp(sink - m)
                o = jnp.dot(p.astype(BF16), v_h, preferred_element_type=F32) / l
                lse_ref[:, hd:hd + 1] = m + jnp.log(l)
                g_ref = glo_ref if hd < NQ // 2 else ghi_ref
                gc = slice((hd % (NQ // 2)) * HD, (hd % (NQ // 2) + 1) * HD)
                gt = g_ref[:, gc].astype(F32)
                o_ref[:, cols] = o.astype(BF16)
                ya_ref[:, cols] = (o * (gt * _sig(gt))).astype(BF16)

    blk = lambda w, cidx: pl.BlockSpec((BLK, w), lambda n: (n, cidx))
    return pl.pallas_call(
        body, name="attn_fwd", grid=(t // BLK,),
        in_specs=[_SMEM, _SMEM, pl.BlockSpec((BLK, 2 * BLK), lambda n: (0, 0)),
                  blk(DA, 0), blk(512, 2), pl.BlockSpec((BLK, 512), lambda n: (jnp.maximum(n - 1, 0), 2)),
                  blk(512, 3), blk(512, 4)],
        out_specs=[blk(DA, 0), blk(DA, 0), pl.BlockSpec((BLK, NQ), lambda n: (n, 0))],
        out_shape=[S((t, DA), BF16), S((t, DA), BF16), S((t, NQ), F32)],
        scratch_shapes=[pltpu.VMEM((NQ, BLK, 2 * BLK), F32)],
        compiler_params=_cp(("arbitrary",), 32),
    )(rel_bias, sinks, bucket, proj, proj, proj, proj, proj)


def _conv_rows(t):
    return min(256, t)


def _glu_into(u_scr, i, a_refs, b_refs, ah_refs, bh_refs, tc):
    for cc in range(DA // 128):
        half, lc = cc // 4, slice((cc % 4) * 128, (cc % 4 + 1) * 128)
        uh = ah_refs[half][:, lc].astype(F32) * _sig(bh_refs[half][:, lc].astype(F32))
        u_scr[cc, 0:HALO, :] = jnp.where(i == 0, 0.0, uh)
        u_scr[cc, HALO:HALO + tc, :] = a_refs[half][:, lc].astype(F32) * _sig(b_refs[half][:, lc].astype(F32))


def _conv_fwd(proj, cwb, conv_b, ln_g, ln_b, wpw, b_pw):
    t = proj.shape[0]
    tc = _conv_rows(t)
    rc = min(128, tc)

    def body(alo, ahi, blo, bhi, alo_h, ahi_h, blo_h, bhi_h, glo, ghi, cw_ref, cb_ref, lg_ref, lb_ref, wpw_ref, bpw_ref,
             uc_ref, pw_ref, yc_ref, u_scr):
        i = pl.program_id(0)
        _glu_into(u_scr, i, (alo, ahi), (blo, bhi), (alo_h, ahi_h), (blo_h, bhi_h), tc)

        def rows(r, carry):
            r0 = pl.multiple_of(r * rc, rc)
            for cc in range(DA // 128):
                lanes = pl.ds(cc * 128, 128)
                acc = jnp.zeros((rc // 8, 8, 128), F32)
                for k in range(CW):
                    u = u_scr[cc, pl.ds(r0 + (HALO - CW + 1) + k, rc), :].reshape(rc // 8, 8, 128)
                    acc = acc + u * cw_ref[k, :, lanes]
                uc_ref[pl.ds(r0, rc), lanes] = acc.reshape(rc, 128) + cb_ref[:, lanes]
            return carry

        lax.fori_loop(0, tc // rc, rows, 0)

        uc = uc_ref[...]
        mu = jnp.mean(uc, axis=-1, keepdims=True)
        xc = uc - mu
        rstd = lax.rsqrt(jnp.mean(xc * xc, axis=-1, keepdims=True) + LN_EPS)
        ln = xc * rstd * lg_ref[...] + lb_ref[...]
        sw = (ln * _sig(ln)).astype(BF16)
        pw = jnp.dot(sw, wpw_ref[...], preferred_element_type=F32) + bpw_ref[...]
        pw_ref[...] = pw.astype(BF16)
        gt = jnp.concatenate([glo[...], ghi[...]], axis=1).astype(F32)
        yc_ref[...] = (pw * (gt * _sig(gt))).astype(BF16)

    nh = tc // HALO
    cur = lambda cidx: pl.BlockSpec((tc, 512), lambda i: (i, cidx))
    halo = lambda cidx: pl.BlockSpec((HALO, 512), lambda i: (jnp.maximum(i * nh - 1, 0), cidx))
    row = pl.BlockSpec((1, DA), lambda i: (0, 0))
    full = pl.BlockSpec((tc, DA), lambda i: (i, 0))
    return pl.pallas_call(
        body, name="conv_fwd", grid=(t // tc,),
        in_specs=[cur(5), cur(6), cur(7), cur(8), halo(5), halo(6), halo(7), halo(8), cur(9), cur(10),
                  pl.BlockSpec((CW, 8, DA), lambda i: (0, 0, 0)), row, row, row,
                  pl.BlockSpec((DA, DA), lambda i: (0, 0)), row],
        out_specs=[full, full, full],
        out_shape=[S((t, DA), F32), S((t, DA), BF16), S((t, DA), BF16)],
        scratch_shapes=[pltpu.VMEM((DA // 128, HALO + tc, 128), F32)],
        compiler_params=_cp(("arbitrary",), 48),
    )(*([proj] * 10), cwb, conv_b, ln_g, ln_b, wpw, b_pw)


def _outproj_loss(ya, yc, w_out, x, target, gate, ln_g, ln_b):
    t = x.shape[0]
    tm = min(256, t)

    def body(ya_ref, yc_ref, w_ref, x_ref, t_ref, g_ref, lg_ref, lb_ref, gx_ref, dy_ref, st_ref):
        @pl.when(pl.program_id(0) == 0)
        def _():
            st_ref[...] = jnp.zeros_like(st_ref)

        y = jnp.dot(ya_ref[...], w_ref[0:DA, :], preferred_element_type=F32)
        y = y + jnp.dot(yc_ref[...], w_ref[DA:2 * DA, :], preferred_element_type=F32)
        gate_v = g_ref[...]
        z = ALPHA * x_ref[...] + gate_v * y
        mu = jnp.mean(z, axis=-1, keepdims=True)
        zc = z - mu
        rstd = lax.rsqrt(jnp.mean(zc * zc, axis=-1, keepdims=True) + LN_EPS)
        zh = zc * rstd
        diff = zh * lg_ref[...] + lb_ref[...] - t_ref[...]
        dout = diff * (1.0 / D)
        dzh = dout * lg_ref[...]
        m1 = jnp.mean(dzh, axis=-1, keepdims=True)
        m2 = jnp.mean(dzh * zh, axis=-1, keepdims=True)
        dz = rstd * (dzh - m1 - zh * m2)
        gx_ref[...] = ALPHA * dz
        dy_ref[...] = (dz * gate_v).astype(BF16)
        st_ref[0:1, :] += jnp.sum(dout * zh, axis=0, keepdims=True)
        st_ref[1:2, :] += jnp.sum(dout, axis=0, keepdims=True)
        st_ref[2:3, :] += jnp.sum(dz * y, axis=0, keepdims=True)
        st_ref[3:4, :] += jnp.sum(diff * diff, axis=0, keepdims=True) * (0.5 / D)

        @pl.when(pl.program_id(0) == t // tm - 1)
        def _():
            st_ref[3:4, :] = jnp.broadcast_to(jnp.sum(st_ref[3:4, :], axis=-1, keepdims=True), (1, D))

    row = pl.BlockSpec((1, D), lambda i: (0, 0))
    half = pl.BlockSpec((tm, DA), lambda i: (i, 0))
    full = pl.BlockSpec((tm, D), lambda i: (i, 0))
    return pl.pallas_call(
        body, name="outproj_loss", grid=(t // tm,),
        in_specs=[half, half, pl.BlockSpec((D, D), lambda i: (0, 0)), full, full, row, row, row],
        out_specs=[full, full, pl.BlockSpec((8, D), lambda i: (0, 0))],
        out_shape=[S((t, D), F32), S((t, D), BF16), S((8, D), F32)],
        compiler_params=_cp(("arbitrary",), 56),
    )(ya, yc, w_out, x, target, gate, ln_g, ln_b)


def _dycat_gates(dy, w_out, attn_o, pw, proj):
    t = dy.shape[0]
    tm = min(256, t)

    def body(dy_ref, w_ref, o_ref, pw_ref, galo, gahi, gclo, gchi, dao_ref, dga_ref, dpw_ref, dgc_ref):
        dyc = lax.dot_general(dy_ref[...], w_ref[...], (((1,), (1,)), ((), ())), preferred_element_type=F32)
        da, dc = dyc[:, 0:DA], dyc[:, DA:2 * DA]
        ga = jnp.concatenate([galo[...], gahi[...]], axis=1).astype(F32)
        sa = _sig(ga)
        dao_ref[...] = (da * (ga * sa)).astype(BF16)
        dga_ref[...] = (da * o_ref[...].astype(F32) * _dsilu(ga, sa)).astype(BF16)
        gc = jnp.concatenate([gclo[...], gchi[...]], axis=1).astype(F32)
        sc = _sig(gc)
        dpw_ref[...] = (dc * (gc * sc)).astype(BF16)
        dgc_ref[...] = (dc * pw_ref[...].astype(F32) * _dsilu(gc, sc)).astype(BF16)

    half = pl.BlockSpec((tm, DA), lambda i: (i, 0))
    cur = lambda cidx: pl.BlockSpec((tm, 512), lambda i: (i, cidx))
    return pl.pallas_call(
        body, name="dycat_gates", grid=(t // tm,),
        in_specs=[pl.BlockSpec((tm, D), lambda i: (i, 0)), pl.BlockSpec((D, D), lambda i: (0, 0)), half, half,
                  cur(3), cur(4), cur(9), cur(10)],
        out_specs=[half] * 4,
        out_shape=[S((t, DA), BF16)] * 4,
        compiler_params=_cp(("parallel",), 48),
    )(dy, w_out, attn_o, pw, proj, proj, proj, proj)


def _conv_bwd_ln(dpw, wpw, uc, ln_g, ln_b):
    t = dpw.shape[0]
    tc = min(256, t)

    def body(dpw_ref, w_ref, uc_ref, lg_ref, lb_ref, duc_ref, sw_ref, st_ref):
        @pl.when(pl.program_id(0) == 0)
        def _():
            st_ref[...] = jnp.zeros_like(st_ref)

        dpw_v = dpw_ref[...]
        ds = lax.dot_general(dpw_v, w_ref[...], (((1,), (1,)), ((), ())), preferred_element_type=F32)
        uc = uc_ref[...]
        mu = jnp.mean(uc, axis=-1, keepdims=True)
        xc = uc - mu
        rstd = lax.rsqrt(jnp.mean(xc * xc, axis=-1, keepdims=True) + LN_EPS)
        uh = xc * rstd
        ln = uh * lg_ref[...] + lb_ref[...]
        sg = _sig(ln)
        sw_ref[...] = (ln * sg).astype(BF16)
        dln = ds * _dsilu(ln, sg)
        dxh = dln * lg_ref[...]
        m1 = jnp.mean(dxh, axis=-1, keepdims=True)
        m2 = jnp.mean(dxh * uh, axis=-1, keepdims=True)
        duc = rstd * (dxh - m1 - uh * m2)
        duc_ref[...] = duc
        st_ref[0:1, :] += jnp.sum(dln * uh, axis=0, keepdims=True)
        st_ref[1:2, :] += jnp.sum(dln, axis=0, keepdims=True)
        st_ref[2:3, :] += jnp.sum(duc, axis=0, keepdims=True)
        st_ref[3:4, :] += jnp.sum(dpw_v.astype(F32), axis=0, keepdims=True)

    row = pl.BlockSpec((1, DA), lambda i: (0, 0))
    full = pl.BlockSpec((tc, DA), lambda i: (i, 0))
    return pl.pallas_call(
        body, name="conv_bwd_ln", grid=(t // tc,),
        in_specs=[full, pl.BlockSpec((DA, DA), lambda i: (0, 0)), full, row, row],
        out_specs=[full, full, pl.BlockSpec((8, DA), lambda i: (0, 0))],
        out_shape=[S((t, DA), F32), S((t, DA), BF16), S((8, DA), F32)],
        compiler_params=_cp(("arbitrary",), 48),
    )(dpw, wpw, uc, ln_g, ln_b)


def _conv_bwd_dw(duc, proj, cwb):
    t = duc.shape[0]
    tc = _conv_rows(t)
    rc = min(128, tc)
    nt = t // tc
    off = HALO - CW + 1

    def body(dcur, dnext, alo, ahi, blo, bhi, alo_h, ahi_h, blo_h, bhi_h, cw_ref, da_ref, db_ref, gw_ref, u_scr, d_scr, g_scr):
        i = pl.program_id(0)

        @pl.when(i == 0)
        def _():
            g_scr[...] = jnp.zeros_like(g_scr)

        _glu_into(u_scr, i, (alo, ahi), (blo, bhi), (alo_h, ahi_h), (blo_h, bhi_h), tc)
        for cc in range(DA // 128):
            d_scr[cc, 0:tc, :] = dcur[:, cc * 128:(cc + 1) * 128]
            d_scr[cc, tc:tc + HALO, :] = jnp.where(i == nt - 1, 0.0, dnext[:, cc * 128:(cc + 1) * 128])

        def rows(r, carry):
            r0 = pl.multiple_of(r * rc, rc)
            for cc in range(DA // 128):
                lanes = pl.ds(cc * 128, 128)
                acc = jnp.zeros((rc // 8, 8, 128), F32)
                for j in range(CW):
                    dv = d_scr[cc, pl.ds(r0 + j, rc), :].reshape(rc // 8, 8, 128)
                    acc = acc + dv * cw_ref[CW - 1 - j, :, lanes]
                du = acc.reshape(rc, 128)
                d0 = d_scr[cc, pl.ds(r0, rc), :].reshape(rc // 8, 8, 128)
                for k in range(CW):
                    u = u_scr[cc, pl.ds(r0 + off + k, rc), :].reshape(rc // 8, 8, 128)
                    g_scr[k, :, lanes] += jnp.sum(d0 * u, axis=0)
                a_ref, b_ref = (alo, blo) if cc < 4 else (ahi, bhi)
                lc = pl.ds((cc % 4) * 128, 128)
                av = a_ref[pl.ds(r0, rc), lc].astype(F32)
                sb = _sig(b_ref[pl.ds(r0, rc), lc].astype(F32))
                da_ref[pl.ds(r0, rc), lanes] = (du * sb).astype(BF16)
                db_ref[pl.ds(r0, rc), lanes] = (du * av * sb * (1.0 - sb)).astype(BF16)
            return carry

        lax.fori_loop(0, tc // rc, rows, 0)

        @pl.when(i == nt - 1)
        def _():
            gw_ref[0:CW, :] = jnp.sum(g_scr[...], axis=1)
            gw_ref[CW:32, :] = jnp.zeros((32 - CW, DA), F32)

    nh = tc // HALO
    nhb = t // HALO
    cur = lambda cidx: pl.BlockSpec((tc, 512), lambda i: (i, cidx))
    halo = lambda cidx: pl.BlockSpec((HALO, 512), lambda i: (jnp.maximum(i * nh - 1, 0), cidx))
    full = pl.BlockSpec((tc, DA), lambda i: (i, 0))
    return pl.pallas_call(
        body, name="conv_bwd_dw", grid=(nt,),
        in_specs=[full, pl.BlockSpec((HALO, DA), lambda i: (jnp.minimum((i + 1) * nh, nhb - 1), 0)),
                  cur(5), cur(6), cur(7), cur(8), halo(5), halo(6), halo(7), halo(8),
                  pl.BlockSpec((CW, 8, DA), lambda i: (0, 0, 0))],
        out_specs=[full, full, pl.BlockSpec((32, DA), lambda i: (0, 0))],
        out_shape=[S((t, DA), BF16), S((t, DA), BF16), S((32, DA), F32)],
        scratch_shapes=[pltpu.VMEM((DA // 128, HALO + tc, 128), F32), pltpu.VMEM((DA // 128, tc + HALO, 128), F32),
                        pltpu.VMEM((CW, 8, DA), F32)],
        compiler_params=_cp(("arbitrary",), 48),
    )(duc, duc, *([proj] * 8), cwb)


def _attn_bwd(proj, dao, attn_o, lse, rel_bias, sinks, bucket):
    t = proj.shape[0]

    def body(rel_ref, sink_ref, bk_ref, q_ref, kvc_ref, kvp_ref, do_ref, o_ref, lse_ref,
             dq_ref, dcur_ref, dprev_ref, dsacc_ref, dsk_ref, bias_ref):
        n = pl.program_id(0)

        @pl.when(n == 0)
        def _():
            _build_bias(rel_ref, bk_ref, bias_ref)
            dsacc_ref[...] = jnp.zeros_like(dsacc_ref)
            dsk_ref[...] = jnp.zeros_like(dsk_ref)

        first = _first_block_mask(n)
        kv = jnp.concatenate([kvp_ref[...], kvc_ref[...]], axis=0)
        for h in range(NKV):
            k_h = kv[:, h * HD:(h + 1) * HD]
            v_h = kv[:, NKV * HD + h * HD:NKV * HD + (h + 1) * HD]
            dk = jnp.zeros((2 * BLK, HD), F32)
            dv = jnp.zeros((2 * BLK, HD), F32)
            for g in range(GRP):
                hd = GRP * h + g
                cols = slice(hd * HD, (hd + 1) * HD)
                q = q_ref[:, cols]
                do = do_ref[:, cols]
                lse_h = lse_ref[:, hd:hd + 1]
                s = lax.dot_general(q, k_h, (((1,), (1,)), ((), ())), preferred_element_type=F32)
                p = jnp.exp(s * SCALE + bias_ref[hd] + first - lse_h)
                dp = lax.dot_general(do, v_h, (((1,), (1,)), ((), ())), preferred_element_type=F32)
                delta = jnp.sum(do.astype(F32) * o_ref[:, cols].astype(F32), axis=-1, keepdims=True)
                ds = p * (dp - delta)
                dsacc_ref[hd] += ds
                dsk_ref[:, hd:hd + 1] += -jnp.exp(sink_ref[0, hd] - lse_h) * delta
                dsb = ds.astype(BF16)
                dq_ref[:, cols] = (jnp.dot(dsb, k_h, preferred_element_type=F32) * SCALE).astype(BF16)
                dk = dk + lax.dot_general(dsb, q, (((0,), (0,)), ((), ())), preferred_element_type=F32)
                dv = dv + lax.dot_general(p.astype(BF16), do, (((0,), (0,)), ((), ())), preferred_element_type=F32)
            dk = dk * SCALE
            kc = slice(h * HD, (h + 1) * HD)
            vc = slice(NKV * HD + h * HD, NKV * HD + (h + 1) * HD)
            dprev_ref[:, kc] = dk[0:BLK]
            dcur_ref[:, kc] = dk[BLK:2 * BLK]
            dprev_ref[:, vc] = dv[0:BLK]
            dcur_ref[:, vc] = dv[BLK:2 * BLK]

        @pl.when(n == t // BLK - 1)
        def _():
            dsk_ref[0:1, :] = jnp.sum(dsk_ref[...], axis=0, keepdims=True)

    blk = lambda w, cidx: pl.BlockSpec((BLK, w), lambda n: (n, cidx))
    return pl.pallas_call(
        body, name="attn_bwd", grid=(t // BLK,),
        in_specs=[_SMEM, _SMEM, pl.BlockSpec((BLK, 2 * BLK), lambda n: (0, 0)),
                  blk(DA, 0), blk(512, 2), pl.BlockSpec((BLK, 512), lambda n: (jnp.maximum(n - 1, 0), 2)),
                  blk(DA, 0), blk(DA, 0), pl.BlockSpec((BLK, NQ), lambda n: (n, 0))],
        out_specs=[blk(DA, 0), blk(512, 0), blk(512, 0),
                   pl.BlockSpec((NQ, BLK, 2 * BLK), lambda n: (0, 0, 0)), pl.BlockSpec((BLK, NQ), lambda n: (0, 0))],
        out_shape=[S((t, DA), BF16), S((t, 512), F32), S((t, 512), F32), S((NQ, BLK, 2 * BLK), F32), S((BLK, NQ), F32)],
        scratch_shapes=[pltpu.VMEM((NQ, BLK, 2 * BLK), F32)],
        compiler_params=_cp(("arbitrary",), 40),
    )(rel_bias, sinks, bucket, proj, proj, proj, dao, attn_o, lse)


def _dkv_fix(dcur, dprev):
    t = dcur.shape[0]
    nb = t // BLK

    def body(c_ref, p_ref, o_ref):
        nxt = jnp.where(pl.program_id(0) == nb - 1, 0.0, p_ref[...])
        o_ref[...] = (c_ref[...] + nxt).astype(BF16)

    return pl.pallas_call(
        body, name="dkv_fix", grid=(nb,),
        in_specs=[pl.BlockSpec((BLK, 512), lambda n: (n, 0)), pl.BlockSpec((BLK, 512), lambda n: (jnp.minimum(n + 1, nb - 1), 0))],
        out_specs=pl.BlockSpec((BLK, 512), lambda n: (n, 0)),
        out_shape=S((t, 512), BF16),
        compiler_params=_cp(("parallel",), 32),
    )(dcur, dprev)


def _bias_grad(dsacc, bucket):
    def body(ds_ref, bk_ref, o_ref, row_scr):
        bk = bk_ref[...]

        def head(hd, carry):
            dsv = ds_ref[hd]
            for b in range(N_BUCKETS):
                row_scr[hd, b:b + 1, :] = jnp.sum(jnp.where(bk == b, dsv, 0.0), axis=0, keepdims=True)
            return carry

        lax.fori_loop(0, NQ, head, 0)
        for hd in range(NQ):
            o_ref[hd] = jnp.sum(row_scr[hd], axis=-1, keepdims=True)

    return pl.pallas_call(
        body, name="bias_grad", out_shape=S((NQ, N_BUCKETS, 1), F32),
        in_specs=[_VMEM, _VMEM], out_specs=_VMEM,
        scratch_shapes=[pltpu.VMEM((NQ, N_BUCKETS, 2 * BLK), F32)],
        compiler_params=_cp(None, 32),
    )(dsacc, bucket)


def _dh_gradx(dproj, wg, gx0, x, scale1p, p_in, p_out, p_pw):
    t = x.shape[0]
    tm = min(512, t)
    ni = t // tm

    def body(dp_ref, w_ref, gx_ref, x_ref, sc_ref, pi, po, pp, out_ref, st_ref, li, lo, lp, acc_ref, send_sems, recv_sems):
        i, j = pl.program_id(0), pl.program_id(1)

        @pl.when((i == 0) & (j == 0))
        def _():
            st_ref[...] = jnp.zeros_like(st_ref)
            for cp in _chip_exchange_copies(pi, po, pp, li, lo, lp, send_sems, recv_sems):
                cp.start()

        part = lax.dot_general(dp_ref[...], w_ref[...], (((1,), (1,)), ((), ())), preferred_element_type=F32)

        @pl.when(j == 0)
        def _():
            acc_ref[...] = part

        @pl.when(j > 0)
        def _():
            acc_ref[...] += part

        @pl.when(j == NCHIP - 1)
        def _():
            dh = acc_ref[...]
            out_ref[...] = gx_ref[...] + dh * sc_ref[...]
            st_ref[0:1, :] += jnp.sum(dh, axis=0, keepdims=True)
            st_ref[1:2, :] += jnp.sum(dh * x_ref[...], axis=0, keepdims=True)

        @pl.when((i == ni - 1) & (j == NCHIP - 1))
        def _():
            cps = _chip_exchange_copies(pi, po, pp, li, lo, lp, send_sems, recv_sems)
            for cp in cps:
                cp.wait_recv()
            for cp in cps:
                cp.wait_send()

    full = pl.BlockSpec((tm, D), lambda i, j: (i, 0))
    return pl.pallas_call(
        body, name="dh_gradx", grid=(ni, NCHIP),
        in_specs=[pl.BlockSpec((tm, NB), lambda i, j: (i, j)), pl.BlockSpec((None, D, NB), lambda i, j: (j, 0, 0)),
                  full, full, pl.BlockSpec((1, D), lambda i, j: (0, 0)), _ANY, _ANY, _ANY],
        out_specs=[full, pl.BlockSpec((8, D), lambda i, j: (0, 0)), _ANY, _ANY, _ANY],
        out_shape=[S((t, D), F32), S((8, D), F32), S((3, p_in.shape[0], NB), p_in.dtype),
                   S((3,) + p_out.shape[1:], p_out.dtype), S((3,) + p_pw.shape[1:], p_pw.dtype)],
        scratch_shapes=[pltpu.VMEM((tm, D), F32), pltpu.SemaphoreType.DMA((9,)), pltpu.SemaphoreType.DMA((9,))],
        compiler_params=_cp(("arbitrary", "arbitrary"), 56),
    )(dproj, wg, gx0, x, scale1p, p_in, p_out, p_pw)


def _atb(a, b, bn, name):
    t, m = a.shape
    n = b.shape[1]
    tk = min(512, t)
    nk = t // tk

    def body(a_ref, b_ref, o_ref):
        part = lax.dot_general(a_ref[...], b_ref[...], (((0,), (0,)), ((), ())), preferred_element_type=F32)

        @pl.when(pl.program_id(1) == 0)
        def _():
            o_ref[...] = part

        @pl.when(pl.program_id(1) > 0)
        def _():
            o_ref[...] += part

    return pl.pallas_call(
        body, name=name, grid=(n // bn, nk),
        in_specs=[pl.BlockSpec((tk, m), lambda j, k: (k, 0)), pl.BlockSpec((tk, bn), lambda j, k: (k, j))],
        out_specs=pl.BlockSpec((m, bn), lambda j, k: (0, j)),
        out_shape=S((m, n), F32),
        compiler_params=_cp(("parallel", "arbitrary"), 56),
    )(a, b)


def _sum8(parts):
    _, r, n = parts.shape

    def body(p_ref, o_ref):
        v = p_ref[0]
        for d in range(1, 8):
            v = v + p_ref[d]
        o_ref[...] = v

    return pl.pallas_call(body, name="sum8", out_shape=S((r, n), F32), in_specs=[_VMEM], out_specs=_VMEM,
                          compiler_params=_cp(None, 32))(parts)


def _adam_math(w, g, m, v):
    m = B1 * m + (1.0 - B1) * g
    v = B2 * v + (1.0 - B2) * (g * g)
    m_hat = m / (1.0 - B1 ** STEP)
    v_hat = v / (1.0 - B2 ** STEP)
    delta = -LR * (m_hat / (jnp.sqrt(v_hat) + EPS) + WD * w)
    return delta, m, v


def _adamw(w, g, m, v, name):
    r, n = w.shape
    tr = min(256, r)

    def body(w_ref, g_ref, m_ref, v_ref, d_ref, nm_ref, nv_ref):
        d_ref[...], nm_ref[...], nv_ref[...] = _adam_math(w_ref[...], g_ref[...], m_ref[...], v_ref[...])

    spec = pl.BlockSpec((tr, n), lambda i: (i, 0))
    return pl.pallas_call(
        body, name=name, grid=(r // tr,), in_specs=[spec] * 4, out_specs=[spec] * 3,
        out_shape=[S((r, n), F32)] * 3, compiler_params=_cp(("parallel",), 48),
    )(w, g, m, v)


def _adamw_ada(ct, dmod_p, w, m, v):
    r, n = w.shape
    tr = min(256, r)

    def body(ct_ref, dm_ref, w_ref, m_ref, v_ref, g_ref, d_ref, nm_ref, nv_ref):
        cv = ct_ref[...]
        g = jnp.dot(cv * _sig(cv), dm_ref[...], preferred_element_type=F32)
        g_ref[...] = g
        d_ref[...], nm_ref[...], nv_ref[...] = _adam_math(w_ref[...], g, m_ref[...], v_ref[...])

    spec = pl.BlockSpec((tr, n), lambda i: (i, 0))
    return pl.pallas_call(
        body, name="adamw_ada", grid=(r // tr,),
        in_specs=[pl.BlockSpec((tr, 8), lambda i: (i, 0)), pl.BlockSpec((8, n), lambda i: (0, 0)), spec, spec, spec],
        out_specs=[spec] * 4, out_shape=[S((r, n), F32)] * 4, compiler_params=_cp(("parallel",), 48),
    )(ct, dmod_p, w, m, v)


def _adamw_small(ws, gs, ms, vs):
    k = len(ws)

    def body(*refs):
        ins, outs = refs[:4 * k], refs[4 * k:]
        for i in range(k):
            d, nm, nv = _adam_math(ins[i][...], ins[k + i][...], ins[2 * k + i][...], ins[3 * k + i][...])
            outs[i][...] = d
            outs[k + i][...] = nm
            outs[2 * k + i][...] = nv

    shapes = [S(w.shape, F32) for w in ws]
    return pl.pallas_call(body, name="adamw_small", out_shape=shapes * 3, in_specs=[_VMEM] * (4 * k),
                          out_specs=[_VMEM] * (3 * k), compiler_params=_cp(None, 32))(*ws, *gs, *ms, *vs)


def _bucket_map():
    qi = jnp.arange(BLK, dtype=jnp.int32)[:, None]
    kj = jnp.arange(2 * BLK, dtype=jnp.int32)[None, :]
    dist = qi + BLK - kj
    dd = jnp.maximum(dist, 0)
    max_exact = N_BUCKETS // 2
    dfl = jnp.maximum(dd, 1).astype(F32)
    large = max_exact + (jnp.log(dfl / max_exact) / math.log(MAX_DIST / max_exact) * (N_BUCKETS - max_exact)).astype(jnp.int32)
    large = jnp.minimum(large, N_BUCKETS - 1)
    bucket = jnp.where(dd < max_exact, dd, large)
    return jnp.where((dist >= 0) & (dist < BLK), bucket, -1).astype(jnp.int32)


def _pad_rows(a, rows):
    return jnp.pad(a, ((0, rows - a.shape[0]), (0, 0)))


def kernel(x, c, w_ada, b_ada, w_in, rel_bias, sinks, conv_w, conv_b, conv_ln_g, conv_ln_b, w_pw, b_pw, w_out, ln_g, ln_b, loss_target, m_w_ada, m_b_ada, m_w_in, m_rel_bias, m_sinks, m_conv_w, m_conv_b, m_conv_ln_g, m_conv_ln_b, m_w_pw, m_b_pw, m_w_out, m_ln_g, m_ln_b, v_w_ada, v_b_ada, v_w_in, v_rel_bias, v_sinks, v_conv_w, v_conv_b, v_conv_ln_g, v_conv_ln_b, v_w_pw, v_b_pw, v_w_out, v_ln_g, v_ln_b):
    mx, my, mc = _me()
    chip = 2 * mx + my
    dev = 2 * chip + mc
    t = x.shape[1]
    x2 = x.reshape(t, D)
    tgt = loss_target.reshape(t, D)
    n_ada = w_ada.shape[2]
    cw_cols = conv_w.shape[2]

    pack0 = jnp.concatenate([_pad_rows(c, 8), _pad_rows(_pad_rows(conv_w[0], 32).reshape(-1, D), 8)], axis=0)
    g0 = _allgather8(pack0, "gather_c_convw").reshape(8, 16, D)
    c_all = g0[:, 0, :]
    cw_rows = 32 * cw_cols // D
    cw_full = jnp.concatenate([g0[2 * q, 8:8 + cw_rows, :].reshape(32, cw_cols) for q in range(NCHIP)], axis=1)[:CW]
    cwb = jnp.broadcast_to(cw_full[:, None, :], (CW, 8, DA))

    b_ada_p = lax.dynamic_slice(b_ada, (0, chip * n_ada), (1, n_ada))
    mod_part = _ada_fwd(c_all, w_ada[0], b_ada_p)
    mod_all = _allgather8(mod_part, "gather_mod").reshape(8, 8, n_ada)
    mod = jnp.concatenate([lax.dynamic_slice(mod_all[2 * q], (dev, 0), (1, n_ada)) for q in range(NCHIP)], axis=1)
    shift, scale, gate = mod[:, 0:D], mod[:, D:2 * D], mod[:, 2 * D:3 * D]
    scale1p = 1.0 + scale

    bucket = _bucket_map()
    sel_chip = jnp.reshape(chip, (1,)).astype(jnp.int32)
    sel = jnp.reshape(mc, (1,)).astype(jnp.int32)
    proj, h, wg_in, wg_out, wg_pw = _inproj_gather(
        x2, scale1p, shift, w_in[0].astype(BF16), w_out[0].astype(BF16), w_pw[0].astype(BF16), sel_chip)
    w_out_f = wg_out.reshape(D, D)
    w_pw_f = wg_pw.reshape(DA, DA)
    ya, attn_o, lse = _attn_fwd(proj, rel_bias, sinks, bucket)
    uc, pw, yc = _conv_fwd(proj, cwb, conv_b, conv_ln_g, conv_ln_b, w_pw_f, b_pw)
    gx0, dy, st_out = _outproj_loss(ya, yc, w_out_f, x2, tgt, gate, ln_g, ln_b)

    dao, dga, dpw, dgc = _dycat_gates(dy, w_out_f, attn_o, pw, proj)
    duc, sw, st_conv = _conv_bwd_ln(dpw, w_pw_f, uc, conv_ln_g, conv_ln_b)
    dglu_a, dglu_b, gw_conv = _conv_bwd_dw(duc, proj, cwb)
    dq, dkv_cur, dkv_prev, dsacc, dsk = _attn_bwd(proj, dao, attn_o, lse, rel_bias, sinks, bucket)
    dkv = _dkv_fix(dkv_cur, dkv_prev)
    g_rel = _bias_grad(dsacc, bucket).reshape(NQ, N_BUCKETS).T
    dproj = jnp.concatenate([dq, dkv, dga, dglu_a, dglu_b, dgc], axis=1)
    gp_in = _atb(h, dproj, NB, "grad_w_in")
    gp_out = _atb(jnp.concatenate([ya, yc], axis=1), dy, 1024, "grad_w_out")
    gp_pw = _atb(sw, dpw, 1024, "grad_w_pw")

    r_out, r_pw = D // NCHIP // 2, DA // NCHIP // 2
    gi3, go3, gp3 = gp_in.reshape(2, D // 2, D_IN), gp_out.reshape(8, r_out, D), gp_pw.reshape(8, r_pw, DA)
    l_in, l_out, l_pw = _pair_exchange(gi3, go3, gp3)
    pi32, pi16 = _pair_sum(gi3, l_in.reshape(1, D // 2, D_IN), sel, "pair_sum_in")
    po32, po16 = _pair_sum(go3, l_out, sel, "pair_sum_out")
    pp32, pp16 = _pair_sum(gp3, l_pw, sel, "pair_sum_pw")
    grad_x, st_in, rc_in, rc_out, rc_pw = _dh_gradx(dproj, wg_in, gx0, x2, scale1p, pi16[0], po16, pp16)
    own_in = lax.dynamic_slice(pi32[0], (0, chip * NB), (D // 2, NB))
    own_out = lax.dynamic_index_in_dim(po32, chip, 0, keepdims=False)
    own_pw = lax.dynamic_index_in_dim(pp32, chip, 0, keepdims=False)
    h_in = _chip_sum(own_in, rc_in, sel, "chip_sum_in")
    h_out = _chip_sum(own_out, rc_out, sel, "chip_sum_out")
    h_pw = _chip_sum(own_pw, rc_pw, sel, "chip_sum_pw")
    f_in, f_out, f_pw = _pair_share(h_in, h_out, h_pw)
    g_w_in = f_in.reshape(D, NB)
    g_w_out = f_out.reshape(D // NCHIP, D)
    g_w_pw = f_pw.reshape(DA // NCHIP, DA)

    dmod = jnp.concatenate([st_in[0:1], st_in[1:2], st_out[2:3]], axis=1)
    loss_row = st_out[3:4, 0:1]
    small = jnp.concatenate([
        dmod, st_out[0:1], st_out[1:2],
        st_conv[0:1], st_conv[1:2], st_conv[2:3], st_conv[3:4],
        g_rel.reshape(1, N_BUCKETS * NQ), dsk[0:1], loss_row,
        gw_conv[:CW].reshape(1, CW * DA)], axis=1)
    n_small = small.shape[1]
    rows_small = -(-n_small // (8 * D)) * 8
    small = jnp.pad(small, ((0, 0), (0, rows_small * D - n_small))).reshape(rows_small, D)
    parts = _allgather8(small, "gather_small").reshape(8, rows_small, D)
    tot = _sum8(parts).reshape(1, rows_small * D)
    dmod_all = parts.reshape(8, rows_small * D)[:, 0:3 * D]

    o = 3 * D
    def take(nn):
        nonlocal o
        v = tot[:, o:o + nn]
        o += nn
        return v
    g_b_ada = tot[:, 0:3 * D]
    g_ln_g, g_ln_b = take(D), take(D)
    g_cln_g, g_cln_b, g_conv_b, g_b_pw = take(DA), take(DA), take(DA), take(DA)
    g_rel_bias = take(N_BUCKETS * NQ).reshape(N_BUCKETS, NQ)
    g_sinks = take(NQ)
    loss = take(1).reshape(())
    g_conv_w_full = take(CW * DA).reshape(CW, DA)
    g_conv_w = lax.dynamic_slice(g_conv_w_full, (0, chip * cw_cols), (CW, cw_cols))

    dmod_p = lax.dynamic_slice(dmod_all, (0, chip * n_ada), (8, n_ada))
    g_w_ada, d_w_ada, nm_w_ada, nv_w_ada = _adamw_ada(c_all.T, dmod_p, w_ada[0], m_w_ada[0], v_w_ada[0])
    d_w_in, nm_w_in, nv_w_in = _adamw(w_in[0], g_w_in, m_w_in[0], v_w_in[0], "adamw_in")
    d_w_out, nm_w_out, nv_w_out = _adamw(w_out[0], g_w_out, m_w_out[0], v_w_out[0], "adamw_out")
    d_w_pw, nm_w_pw, nv_w_pw = _adamw(w_pw[0], g_w_pw, m_w_pw[0], v_w_pw[0], "adamw_pw")
    small_w = [b_ada, rel_bias, sinks, conv_w[0], conv_b, conv_ln_g, conv_ln_b, b_pw, ln_g, ln_b]
    small_g = [g_b_ada, g_rel_bias, g_sinks, g_conv_w, g_conv_b, g_cln_g, g_cln_b, g_b_pw, g_ln_g, g_ln_b]
    small_m = [m_b_ada, m_rel_bias, m_sinks, m_conv_w[0], m_conv_b, m_conv_ln_g, m_conv_ln_b, m_b_pw, m_ln_g, m_ln_b]
    small_v = [v_b_ada, v_rel_bias, v_sinks, v_conv_w[0], v_conv_b, v_conv_ln_g, v_conv_ln_b, v_b_pw, v_ln_g, v_ln_b]
    res = _adamw_small(small_w, small_g, small_m, small_v)
    ns = len(small_w)
    d_s, nm_s, nv_s = res[:ns], res[ns:2 * ns], res[2 * ns:]

    def ordered(w_ada_, w_in_, w_pw_, w_out_, sm):
        b_ada_, rel_, sinks_, conv_w_, conv_b_, cln_g_, cln_b_, b_pw_, ln_g_, ln_b_ = sm
        return (w_ada_[None], b_ada_, w_in_[None], rel_, sinks_, conv_w_[None], conv_b_, cln_g_, cln_b_,
                w_pw_[None], b_pw_, w_out_[None], ln_g_, ln_b_)

    grads = ordered(g_w_ada, g_w_in, g_w_pw, g_w_out, small_g)
    deltas = ordered(d_w_ada, d_w_in, d_w_pw, d_w_out, d_s)
    new_m = ordered(nm_w_ada, nm_w_in, nm_w_pw, nm_w_out, nm_s)
    new_v = ordered(nv_w_ada, nv_w_in, nv_w_pw, nv_w_out, nv_s)
    return (loss, grad_x.reshape(1, t, D), *grads, *deltas, *new_m, *new_v)
```

```python
import functools
import math

import jax
import jax.numpy as jnp
from jax import lax
from jax.experimental import pallas as pl
from jax.experimental.pallas import tpu as pltpu

F32, BF16 = jnp.float32, jnp.bfloat16
S = jax.ShapeDtypeStruct
MESH = pl.DeviceIdType.MESH

D = 2048
DA = 1024
HD = 64
NQ, NKV, GRP = 16, 4, 4
BLK = 128
CW = 31
HALO = 32
D_IN = 5632
NCHIP = 4
NB = D_IN // NCHIP
N_BUCKETS, MAX_DIST = 32, 128
LN_EPS = 1e-5
ALPHA = 2.0 ** 0.25
SCALE = HD ** -0.5
NEG = -1e30
LR, B1, B2, EPS, WD, STEP = 0.001, 0.9, 0.999, 1e-08, 0.01, 10

_VMEM = pl.BlockSpec(memory_space=pltpu.VMEM)
_SMEM = pl.BlockSpec(memory_space=pltpu.SMEM)
_ANY = pl.BlockSpec(memory_space=pl.ANY)


def _cp(sem=None, vmem_mb=None):
    kw = {}
    if sem is not None:
        kw["dimension_semantics"] = sem
    if vmem_mb is not None:
        kw["vmem_limit_bytes"] = vmem_mb * 1024 * 1024
    return pltpu.CompilerParams(**kw)


def _sig(v):
    return jax.nn.sigmoid(v)


def _dsilu(g, sg):
    return sg * (1.0 + g * (1.0 - sg))


def _me():
    return lax.axis_index("x"), lax.axis_index("y"), lax.axis_index("c")


def _allgather8(x_shard, name):
    m_per, n = x_shard.shape

    def body(x_ref, out_ref, send_sems, recv_sems, local_sem):
        x, y, c = _me()
        me, sibling = (x, y, c), (x, y, 1 - c)
        chips = [(1 - x, y), (x, 1 - y), (1 - x, 1 - y)]

        def rows(px, py, pc):
            return out_ref.at[pl.ds((4 * px + 2 * py + pc) * m_per, m_per), :]

        def copy(k, block, to, src=None):
            return pltpu.make_async_remote_copy(
                src_ref=rows(*block) if src is None else src, dst_ref=rows(*block),
                send_sem=send_sems.at[k], recv_sem=recv_sems.at[k], device_id=to, device_id_type=MESH)

        mine = pltpu.make_async_copy(x_ref, rows(*me), local_sem)
        mine.start()
        first = [copy(0, me, sibling, src=x_ref)]
        first += [copy(1 + j, me, (*chip, c), src=x_ref) for j, chip in enumerate(chips)]
        for cp in first:
            cp.start()
        passed = [copy(4 + j, (*chip, c), sibling) for j, chip in enumerate(chips)]
        for j, chip in enumerate(chips):
            copy(1 + j, (*chip, c), me).wait_recv()
            passed[j].start()
        copy(0, sibling, me).wait_recv()
        for j, chip in enumerate(chips):
            copy(4 + j, (*chip, 1 - c), me).wait_recv()
        for cp in first + passed:
            cp.wait_send()
        mine.wait()

    return pl.pallas_call(
        body, name=name, out_shape=S((8 * m_per, n), x_shard.dtype),
        in_specs=[_VMEM], out_specs=_VMEM,
        scratch_shapes=[pltpu.SemaphoreType.DMA((7,)), pltpu.SemaphoreType.DMA((7,)), pltpu.SemaphoreType.DMA],
    )(x_shard)


def _inproj_gather(x, scale1p, shift, wi, wo, wp, chip_id):
    t = x.shape[0]
    tm = min(512, t)
    ni = t // tm
    shards = (wi, wo, wp)
    halves = [s.shape[0] // 2 for s in shards]

    def body(cid_ref, x_ref, sc_ref, sh_ref, wi_ref, wo_ref, wp_ref, proj_ref, h_ref, gi_ref, go_ref, gp_ref,
             wbuf, obuf, pbuf, send_sems, recv_sems, ld_sems, st_sems):
        jj, i = pl.program_id(0), pl.program_id(1)
        mx, my, c = _me()
        p = 2 * mx + my
        sibling = (mx, my, 1 - c)
        srcs, dsts, bufs = (wi_ref, wo_ref, wp_ref), (gi_ref, go_ref, gp_ref), (wbuf, obuf, pbuf)
        chips = {1: (mx, 1 - my), 2: (1 - mx, my), 3: (1 - mx, 1 - my)}

        def half(ref, w, hc):
            return ref.at[pl.ds(hc * halves[w], halves[w])]

        def copy(k, src, dst, to):
            return pltpu.make_async_remote_copy(src_ref=src, dst_ref=dst, send_sem=send_sems.at[k],
                                                recv_sem=recv_sems.at[k], device_id=to, device_id_type=MESH)

        def sent(w, m):
            return copy(3 * w + m - 1, half(srcs[w], w, c), half(dsts[w].at[p], w, c), (*chips[m], c))

        def landed(w, m):
            blk = half(dsts[w].at[jnp.bitwise_xor(p, m)], w, c)
            return copy(3 * w + m - 1, blk, blk, (*chips[m], c))

        def passed(w, m, hc):
            blk = half(dsts[w].at[jnp.bitwise_xor(p, m)], w, hc)
            return copy(9 + 3 * w + m - 1, blk, blk, sibling)

        def load(w, src):
            return pltpu.make_async_copy(src, bufs[w], ld_sems.at[w])

        def store(w):
            return pltpu.make_async_copy(bufs[w], dsts[w].at[p], st_sems.at[w])

        @pl.when((jj == 0) & (i == 0))
        def _():
            sent(0, 1).start()
            sent(0, 2).start()
            for w in range(3):
                load(w, srcs[w]).start()
            for w in range(3):
                load(w, srcs[w]).wait()
                store(w).start()

        for m in (1, 2, 3):
            @pl.when((jj == m) & (i == 0))
            def _(m=m):
                if m == 1:
                    sent(0, 3).start()
                    store(0).wait()
                if m == 2:
                    for w in (1, 2):
                        for mm in (1, 2, 3):
                            sent(w, mm).start()
                landed(0, m).wait_recv()
                passed(0, m, c).start()
                passed(0, m, 1 - c).wait_recv()
                cp = load(0, gi_ref.at[jnp.bitwise_xor(p, m)])
                cp.start()
                cp.wait()

        hb = (x_ref[...] * sc_ref[...] + sh_ref[...]).astype(BF16)

        @pl.when(jj == 0)
        def _():
            h_ref[...] = hb

        proj_ref[...] = jnp.dot(hb, wbuf[...], preferred_element_type=F32).astype(BF16)

        @pl.when((jj == NCHIP - 1) & (i == ni - 1))
        def _():
            for w in (1, 2):
                for m in (1, 2, 3):
                    landed(w, m).wait_recv()
                    passed(w, m, c).start()
            for w in (1, 2):
                for m in (1, 2, 3):
                    passed(w, m, 1 - c).wait_recv()
            for w in range(3):
                for m in (1, 2, 3):
                    sent(w, m).wait_send()
                    passed(w, m, c).wait_send()
            store(1).wait()
            store(2).wait()

    row = pl.BlockSpec((1, D), lambda jj, i, s: (0, 0))
    return pl.pallas_call(
        body, name="inproj_gather",
        grid_spec=pltpu.PrefetchScalarGridSpec(
            num_scalar_prefetch=1, grid=(NCHIP, ni),
            in_specs=[pl.BlockSpec((tm, D), lambda jj, i, s: (i, 0)), row, row, _ANY, _ANY, _ANY],
            out_specs=[pl.BlockSpec((tm, NB), lambda jj, i, s: (i, jnp.bitwise_xor(s[0], jj))),
                       pl.BlockSpec((tm, D), lambda jj, i, s: (jnp.where(jj == 0, i, ni - 1), 0)),
                       _ANY, _ANY, _ANY],
            scratch_shapes=[pltpu.VMEM(wi.shape, BF16), pltpu.VMEM(wo.shape, BF16), pltpu.VMEM(wp.shape, BF16),
                            pltpu.SemaphoreType.DMA((18,)), pltpu.SemaphoreType.DMA((18,)),
                            pltpu.SemaphoreType.DMA((3,)), pltpu.SemaphoreType.DMA((3,))]),
        out_shape=[S((t, D_IN), BF16), S((t, D), BF16)] + [S((NCHIP,) + s.shape, s.dtype) for s in shards],
        compiler_params=_cp(("arbitrary", "arbitrary"), 48),
    )(chip_id, x, scale1p, shift, wi, wo, wp)


def _pair_exchange(g_in, g_out, g_pw):
    def body(gi, go, gp, li, lo, lp, send_sems, recv_sems):
        x, y, c = _me()
        sibling = (x, y, 1 - c)
        copies = [(gi.at[1 - c], li)]
        for j in range(NCHIP):
            copies.append((go.at[2 * j + 1 - c], lo.at[j]))
        for j in range(NCHIP):
            copies.append((gp.at[2 * j + 1 - c], lp.at[j]))
        cps = [pltpu.make_async_remote_copy(src_ref=s_, dst_ref=d_, send_sem=send_sems.at[k], recv_sem=recv_sems.at[k],
                                            device_id=sibling, device_id_type=MESH) for k, (s_, d_) in enumerate(copies)]
        for cp in cps:
            cp.start()
        for cp in cps:
            cp.wait_recv()
        for cp in cps:
            cp.wait_send()

    n = 1 + 2 * NCHIP
    return pl.pallas_call(
        body, name="grad_pair_exchange",
        out_shape=[S(g_in.shape[1:], g_in.dtype), S((NCHIP,) + g_out.shape[1:], g_out.dtype),
                   S((NCHIP,) + g_pw.shape[1:], g_pw.dtype)],
        in_specs=[_ANY] * 3, out_specs=[_ANY] * 3,
        scratch_shapes=[pltpu.SemaphoreType.DMA((n,)), pltpu.SemaphoreType.DMA((n,))],
    )(g_in, g_out, g_pw)


def _chip_exchange_copies(pi, po, pp, li, lo, lp, send_sems, recv_sems):
    x, y, c = _me()
    chips = [(1 - x, y), (x, 1 - y), (1 - x, 1 - y)]
    cps = []
    for j, chip in enumerate(chips):
        q = 2 * chip[0] + chip[1]
        pairs = ((pi.at[:, pl.ds(q * NB, NB)], li.at[j]), (po.at[q], lo.at[j]), (pp.at[q], lp.at[j]))
        for w, (src, dst) in enumerate(pairs):
            k = 3 * j + w
            cps.append(pltpu.make_async_remote_copy(src_ref=src, dst_ref=dst, send_sem=send_sems.at[k],
                                                    recv_sem=recv_sems.at[k], device_id=(*chip, c), device_id_type=MESH))
    return cps


def _pair_share(f_in, f_out, f_pw):
    bufs = (f_in, f_out, f_pw)

    def body(ai, ao, ap, fi, fo, fp, send_sems, recv_sems):
        x, y, c = _me()
        sibling = (x, y, 1 - c)

        def copy(w, ref, hc):
            return pltpu.make_async_remote_copy(src_ref=ref.at[hc], dst_ref=ref.at[hc], send_sem=send_sems.at[w],
                                                recv_sem=recv_sems.at[w], device_id=sibling, device_id_type=MESH)

        cps = [copy(w, ref, c) for w, ref in enumerate((fi, fo, fp))]
        for cp in cps:
            cp.start()
        for w, ref in enumerate((fi, fo, fp)):
            copy(w, ref, 1 - c).wait_recv()
        for cp in cps:
            cp.wait_send()

    return pl.pallas_call(
        body, name="grad_pair_share",
        out_shape=[S(b.shape, b.dtype) for b in bufs],
        in_specs=[_ANY] * 3, out_specs=[_ANY] * 3, input_output_aliases={0: 0, 1: 1, 2: 2},
        scratch_shapes=[pltpu.SemaphoreType.DMA((3,)), pltpu.SemaphoreType.DMA((3,))],
    )(*bufs)


def _pair_sum(g, l, sel, name):
    n, r, ccols = l.shape
    tr = min(256 if ccols <= D else 128, r)

    def body(sel_ref, g_ref, l_ref, o32_ref, o16_ref):
        v = g_ref[...] + l_ref[...]
        o32_ref[...] = v
        o16_ref[...] = v.astype(BF16)

    spec_l = pl.BlockSpec((None, tr, ccols), lambda j, i, s: (j, i, 0))
    return pl.pallas_call(
        body, name=name,
        grid_spec=pltpu.PrefetchScalarGridSpec(
            num_scalar_prefetch=1, grid=(n, r // tr),
            in_specs=[pl.BlockSpec((None, tr, ccols), lambda j, i, s: (2 * j + s[0], i, 0)), spec_l],
            out_specs=[spec_l, spec_l]),
        out_shape=[S(l.shape, F32), S(l.shape, BF16)],
        compiler_params=_cp(("parallel", "parallel"), 48),
    )(sel, g, l)


def _chip_sum(own, recv, sel, name):
    r, ccols = own.shape
    tr = min(256, r)

    def body(sel_ref, o_ref, r_ref, out_ref):
        v = o_ref[...]
        for j in range(3):
            v = v + r_ref[j].astype(F32)
        out_ref[...] = v

    return pl.pallas_call(
        body, name=name,
        grid_spec=pltpu.PrefetchScalarGridSpec(
            num_scalar_prefetch=1, grid=(r // tr,),
            in_specs=[pl.BlockSpec((tr, ccols), lambda i, s: (i, 0)), pl.BlockSpec((3, tr, ccols), lambda i, s: (0, i, 0))],
            out_specs=pl.BlockSpec((None, tr, ccols), lambda i, s: (s[0], i, 0))),
        out_shape=S((2, r, ccols), F32),
        compiler_params=_cp(("parallel",), 48),
    )(sel, own, recv)


def _ada_fwd(c_all, w_ada, b_ada_p):
    n = w_ada.shape[1]
    tn = 512

    def body(c_ref, w_ref, b_ref, o_ref):
        cv = c_ref[...]
        ca = cv * _sig(cv)
        o_ref[...] = jnp.dot(ca, w_ref[...], preferred_element_type=F32) + b_ref[...]

    return pl.pallas_call(
        body, name="ada_fwd", grid=(n // tn,),
        in_specs=[pl.BlockSpec((8, D), lambda j: (0, 0)), pl.BlockSpec((D, tn), lambda j: (0, j)),
                  pl.BlockSpec((1, tn), lambda j: (0, j))],
        out_specs=pl.BlockSpec((8, tn), lambda j: (0, j)),
        out_shape=S((8, n), F32),
        compiler_params=_cp(("parallel",), 32),
    )(c_all, w_ada, b_ada_p)


def _build_bias(rel_ref, bk_ref, bias_ref):
    bk = bk_ref[...]
    kj = lax.broadcasted_iota(jnp.int32, (BLK, 2 * BLK), 1)
    for hd in range(NQ):
        acc = jnp.full((BLK, 2 * BLK), NEG, F32)
        for b in range(N_BUCKETS):
            acc = jnp.where(bk == b, rel_ref[b, hd], acc)
        bias_ref[1, hd] = acc
        bias_ref[0, hd] = jnp.where(kj < BLK, NEG, acc)


def _attn_fwd(proj, rel_bias, sinks, bucket):
    t = proj.shape[0]

    def body(rel_ref, sink_ref, bk_ref, q_ref, kvc_ref, kvp_ref, glo_ref, ghi_ref, ya_ref, o_ref, lse_ref, bias_ref):
        n = pl.program_id(0)

        @pl.when(n == 0)
        def _():
            _build_bias(rel_ref, bk_ref, bias_ref)

        tbl = jnp.where(n == 0, 0, 1)
        kv = jnp.concatenate([kvp_ref[...], kvc_ref[...]], axis=0)
        for h in range(NKV):
            k_h = kv[:, h * HD:(h + 1) * HD]
            v_h = kv[:, NKV * HD + h * HD:NKV * HD + (h + 1) * HD]
            for g in range(GRP):
                hd = GRP * h + g
                cols = slice(hd * HD, (hd + 1) * HD)
                qs = q_ref[:, cols] * SCALE
                s = lax.dot_general(qs, k_h, (((1,), (1,)), ((), ())), preferred_element_type=F32) + bias_ref[tbl, hd]
                sink = sink_ref[0, hd]
                m = jnp.maximum(jnp.max(s, axis=-1, keepdims=True), sink)
                p = jnp.exp(s - m)
                l = jnp.sum(p, axis=-1, keepdims=True) + jnp.exp(sink - m)
                o = jnp.dot(p.astype(BF16), v_h, preferred_element_type=F32) / l
                lse_ref[:, hd:hd + 1] = m + jnp.log(l)
                g_ref = glo_ref if hd < NQ // 2 else ghi_ref
                gc = slice((hd % (NQ // 2)) * HD, (hd % (NQ // 2) + 1) * HD)
                gt = g_ref[:, gc].astype(F32)
                o_ref[:, cols] = o.astype(BF16)
                ya_ref[:, cols] = (o * (gt * _sig(gt))).astype(BF16)

    blk = lambda w, cidx: pl.BlockSpec((BLK, w), lambda n: (n, cidx))
    return pl.pallas_call(
        body, name="attn_fwd", grid=(t // BLK,),
        in_specs=[_SMEM, _SMEM, pl.BlockSpec((BLK, 2 * BLK), lambda n: (0, 0)),
                  blk(DA, 0), blk(512, 2), pl.BlockSpec((BLK, 512), lambda n: (jnp.maximum(n - 1, 0), 2)),
                  blk(512, 3), blk(512, 4)],
        out_specs=[blk(DA, 0), blk(DA, 0), pl.BlockSpec((BLK, NQ), lambda n: (n, 0))],
        out_shape=[S((t, DA), BF16), S((t, DA), BF16), S((t, NQ), F32)],
        scratch_shapes=[pltpu.VMEM((2, NQ, BLK, 2 * BLK), F32)],
        compiler_params=_cp(("arbitrary",), 32),
    )(rel_bias, sinks, bucket, proj, proj, proj, proj, proj)


def _conv_rows(t):
    return min(256, t)


def _glu_into(u_scr, i, a_refs, b_refs, ah_refs, bh_refs, tc):
    for cc in range(DA // 128):
        half, lc = cc // 4, slice((cc % 4) * 128, (cc % 4 + 1) * 128)
        uh = ah_refs[half][:, lc].astype(F32) * _sig(bh_refs[half][:, lc].astype(F32))
        u_scr[cc, 0:HALO, :] = jnp.where(i == 0, 0.0, uh)
        u_scr[cc, HALO:HALO + tc, :] = a_refs[half][:, lc].astype(F32) * _sig(b_refs[half][:, lc].astype(F32))


def _conv_fwd(proj, cwb, conv_b, ln_g, ln_b, wpw, b_pw):
    t = proj.shape[0]
    tc = _conv_rows(t)
    rc = min(128, tc)

    def body(alo, ahi, blo, bhi, alo_h, ahi_h, blo_h, bhi_h, glo, ghi, cw_ref, cb_ref, lg_ref, lb_ref, wpw_ref, bpw_ref,
             uc_ref, pw_ref, yc_ref, u_scr):
        i = pl.program_id(0)
        _glu_into(u_scr, i, (alo, ahi), (blo, bhi), (alo_h, ahi_h), (blo_h, bhi_h), tc)

        def rows(r, carry):
            r0 = pl.multiple_of(r * rc, rc)
            for cc in range(DA // 128):
                lanes = pl.ds(cc * 128, 128)
                acc = jnp.zeros((rc // 8, 8, 128), F32)
                for k in range(CW):
                    u = u_scr[cc, pl.ds(r0 + (HALO - CW + 1) + k, rc), :].reshape(rc // 8, 8, 128)
                    acc = acc + u * cw_ref[k, :, lanes]
                uc_ref[pl.ds(r0, rc), lanes] = acc.reshape(rc, 128) + cb_ref[:, lanes]
            return carry

        lax.fori_loop(0, tc // rc, rows, 0)

        uc = uc_ref[...]
        mu = jnp.mean(uc, axis=-1, keepdims=True)
        xc = uc - mu
        rstd = lax.rsqrt(jnp.mean(xc * xc, axis=-1, keepdims=True) + LN_EPS)
        ln = xc * rstd * lg_ref[...] + lb_ref[...]
        sw = (ln * _sig(ln)).astype(BF16)
        pw = jnp.dot(sw, wpw_ref[...], preferred_element_type=F32) + bpw_ref[...]
        pw_ref[...] = pw.astype(BF16)
        gt = jnp.concatenate([glo[...], ghi[...]], axis=1).astype(F32)
        yc_ref[...] = (pw * (gt * _sig(gt))).astype(BF16)

    nh = tc // HALO
    cur = lambda cidx: pl.BlockSpec((tc, 512), lambda i: (i, cidx))
    halo = lambda cidx: pl.BlockSpec((HALO, 512), lambda i: (jnp.maximum(i * nh - 1, 0), cidx))
    row = pl.BlockSpec((1, DA), lambda i: (0, 0))
    full = pl.BlockSpec((tc, DA), lambda i: (i, 0))
    return pl.pallas_call(
        body, name="conv_fwd", grid=(t // tc,),
        in_specs=[cur(5), cur(6), cur(7), cur(8), halo(5), halo(6), halo(7), halo(8), cur(9), cur(10),
                  pl.BlockSpec((CW, 8, DA), lambda i: (0, 0, 0)), row, row, row,
                  pl.BlockSpec((DA, DA), lambda i: (0, 0)), row],
        out_specs=[full, full, full],
        out_shape=[S((t, DA), F32), S((t, DA), BF16), S((t, DA), BF16)],
        scratch_shapes=[pltpu.VMEM((DA // 128, HALO + tc, 128), F32)],
        compiler_params=_cp(("arbitrary",), 48),
    )(*([proj] * 10), cwb, conv_b, ln_g, ln_b, wpw, b_pw)


def _outproj_loss(ya, yc, w_out, x, target, gate, ln_g, ln_b):
    t = x.shape[0]
    tm = min(256, t)

    def body(ya_ref, yc_ref, w_ref, x_ref, t_ref, g_ref, lg_ref, lb_ref, gx_ref, dy_ref, st_ref):
        @pl.when(pl.program_id(0) == 0)
        def _():
            st_ref[...] = jnp.zeros_like(st_ref)

        y = jnp.dot(ya_ref[...], w_ref[0:DA, :], preferred_element_type=F32)
        y = y + jnp.dot(yc_ref[...], w_ref[DA:2 * DA, :], preferred_element_type=F32)
        gate_v = g_ref[...]
        z = ALPHA * x_ref[...] + gate_v * y
        mu = jnp.mean(z, axis=-1, keepdims=True)
        zc = z - mu
        rstd = lax.rsqrt(jnp.mean(zc * zc, axis=-1, keepdims=True) + LN_EPS)
        zh = zc * rstd
        diff = zh * lg_ref[...] + lb_ref[...] - t_ref[...]
        dout = diff * (1.0 / D)
        dzh = dout * lg_ref[...]
        m1 = jnp.mean(dzh, axis=-1, keepdims=True)
        m2 = jnp.mean(dzh * zh, axis=-1, keepdims=True)
        dz = rstd * (dzh - m1 - zh * m2)
        gx_ref[...] = ALPHA * dz
        dy_ref[...] = (dz * gate_v).astype(BF16)
        st_ref[0:1, :] += jnp.sum(dout * zh, axis=0, keepdims=True)
        st_ref[1:2, :] += jnp.sum(dout, axis=0, keepdims=True)
        st_ref[2:3, :] += jnp.sum(dz * y, axis=0, keepdims=True)
        st_ref[3:4, :] += jnp.sum(diff * diff, axis=0, keepdims=True) * (0.5 / D)

        @pl.when(pl.program_id(0) == t // tm - 1)
        def _():
            st_ref[3:4, :] = jnp.broadcast_to(jnp.sum(st_ref[3:4, :], axis=-1, keepdims=True), (1, D))

    row = pl.BlockSpec((1, D), lambda i: (0, 0))
    half = pl.BlockSpec((tm, DA), lambda i: (i, 0))
    full = pl.BlockSpec((tm, D), lambda i: (i, 0))
    return pl.pallas_call(
        body, name="outproj_loss", grid=(t // tm,),
        in_specs=[half, half, pl.BlockSpec((D, D), lambda i: (0, 0)), full, full, row, row, row],
        out_specs=[full, full, pl.BlockSpec((8, D), lambda i: (0, 0))],
        out_shape=[S((t, D), F32), S((t, D), BF16), S((8, D), F32)],
        compiler_params=_cp(("arbitrary",), 56),
    )(ya, yc, w_out, x, target, gate, ln_g, ln_b)


def _dycat_gates(dy, w_out, attn_o, pw, proj):
    t = dy.shape[0]
    tm = min(256, t)

    def body(dy_ref, w_ref, o_ref, pw_ref, galo, gahi, gclo, gchi, dao_ref, dga_ref, dpw_ref, dgc_ref):
        dyc = lax.dot_general(dy_ref[...], w_ref[...], (((1,), (1,)), ((), ())), preferred_element_type=F32)
        da, dc = dyc[:, 0:DA], dyc[:, DA:2 * DA]
        ga = jnp.concatenate([galo[...], gahi[...]], axis=1).astype(F32)
        sa = _sig(ga)
        dao_ref[...] = (da * (ga * sa)).astype(BF16)
        dga_ref[...] = (da * o_ref[...].astype(F32) * _dsilu(ga, sa)).astype(BF16)
        gc = jnp.concatenate([gclo[...], gchi[...]], axis=1).astype(F32)
        sc = _sig(gc)
        dpw_ref[...] = (dc * (gc * sc)).astype(BF16)
        dgc_ref[...] = (dc * pw_ref[...].astype(F32) * _dsilu(gc, sc)).astype(BF16)

    half = pl.BlockSpec((tm, DA), lambda i: (i, 0))
    cur = lambda cidx: pl.BlockSpec((tm, 512), lambda i: (i, cidx))
    return pl.pallas_call(
        body, name="dycat_gates", grid=(t // tm,),
        in_specs=[pl.BlockSpec((tm, D), lambda i: (i, 0)), pl.BlockSpec((D, D), lambda i: (0, 0)), half, half,
                  cur(3), cur(4), cur(9), cur(10)],
        out_specs=[half] * 4,
        out_shape=[S((t, DA), BF16)] * 4,
        compiler_params=_cp(("parallel",), 48),
    )(dy, w_out, attn_o, pw, proj, proj, proj, proj)


def _conv_bwd_ln(dpw, wpw, uc, ln_g, ln_b):
    t = dpw.shape[0]
    tc = min(256, t)

    def body(dpw_ref, w_ref, uc_ref, lg_ref, lb_ref, duc_ref, sw_ref, st_ref):
        @pl.when(pl.program_id(0) == 0)
        def _():
            st_ref[...] = jnp.zeros_like(st_ref)

        dpw_v = dpw_ref[...]
        ds = lax.dot_general(dpw_v, w_ref[...], (((1,), (1,)), ((), ())), preferred_element_type=F32)
        uc = uc_ref[...]
        mu = jnp.mean(uc, axis=-1, keepdims=True)
        xc = uc - mu
        rstd = lax.rsqrt(jnp.mean(xc * xc, axis=-1, keepdims=True) + LN_EPS)
        uh = xc * rstd
        ln = uh * lg_ref[...] + lb_ref[...]
        sg = _sig(ln)
        sw_ref[...] = (ln * sg).astype(BF16)
        dln = ds * _dsilu(ln, sg)
        dxh = dln * lg_ref[...]
        m1 = jnp.mean(dxh, axis=-1, keepdims=True)
        m2 = jnp.mean(dxh * uh, axis=-1, keepdims=True)
        duc = rstd * (dxh - m1 - uh * m2)
        duc_ref[...] = duc
        st_ref[0:1, :] += jnp.sum(dln * uh, axis=0, keepdims=True)
        st_ref[1:2, :] += jnp.sum(dln, axis=0, keepdims=True)
        st_ref[2:3, :] += jnp.sum(duc, axis=0, keepdims=True)
        st_ref[3:4, :] += jnp.sum(dpw_v.astype(F32), axis=0, keepdims=True)

    row = pl.BlockSpec((1, DA), lambda i: (0, 0))
    full = pl.BlockSpec((tc, DA), lambda i: (i, 0))
    return pl.pallas_call(
        body, name="conv_bwd_ln", grid=(t // tc,),
        in_specs=[full, pl.BlockSpec((DA, DA), lambda i: (0, 0)), full, row, row],
        out_specs=[full, full, pl.BlockSpec((8, DA), lambda i: (0, 0))],
        out_shape=[S((t, DA), F32), S((t, DA), BF16), S((8, DA), F32)],
        compiler_params=_cp(("arbitrary",), 48),
    )(dpw, wpw, uc, ln_g, ln_b)


def _conv_bwd_dw(duc, proj, cwb):
    t = duc.shape[0]
    tc = _conv_rows(t)
    rc = min(128, tc)
    nt = t // tc
    off = HALO - CW + 1

    def body(dcur, dnext, alo, ahi, blo, bhi, alo_h, ahi_h, blo_h, bhi_h, cw_ref, da_ref, db_ref, gw_ref, u_scr, d_scr, g_scr):
        i = pl.program_id(0)

        @pl.when(i == 0)
        def _():
            g_scr[...] = jnp.zeros_like(g_scr)

        _glu_into(u_scr, i, (alo, ahi), (blo, bhi), (alo_h, ahi_h), (blo_h, bhi_h), tc)
        for cc in range(DA // 128):
            d_scr[cc, 0:tc, :] = dcur[:, cc * 128:(cc + 1) * 128]
            d_scr[cc, tc:tc + HALO, :] = jnp.where(i == nt - 1, 0.0, dnext[:, cc * 128:(cc + 1) * 128])

        def rows(r, carry):
            r0 = pl.multiple_of(r * rc, rc)
            for cc in range(DA // 128):
                lanes = pl.ds(cc * 128, 128)
                acc = jnp.zeros((rc // 8, 8, 128), F32)
                for j in range(CW):
                    dv = d_scr[cc, pl.ds(r0 + j, rc), :].reshape(rc // 8, 8, 128)
                    acc = acc + dv * cw_ref[CW - 1 - j, :, lanes]
                du = acc.reshape(rc, 128)
                d0 = d_scr[cc, pl.ds(r0, rc), :].reshape(rc // 8, 8, 128)
                for k in range(CW):
                    u = u_scr[cc, pl.ds(r0 + off + k, rc), :].reshape(rc // 8, 8, 128)
                    g_scr[k, :, lanes] += jnp.sum(d0 * u, axis=0)
                a_ref, b_ref = (alo, blo) if cc < 4 else (ahi, bhi)
                lc = pl.ds((cc % 4) * 128, 128)
                av = a_ref[pl.ds(r0, rc), lc].astype(F32)
                sb = _sig(b_ref[pl.ds(r0, rc), lc].astype(F32))
                da_ref[pl.ds(r0, rc), lanes] = (du * sb).astype(BF16)
                db_ref[pl.ds(r0, rc), lanes] = (du * av * sb * (1.0 - sb)).astype(BF16)
            return carry

        lax.fori_loop(0, tc // rc, rows, 0)

        @pl.when(i == nt - 1)
        def _():
            gw_ref[0:CW, :] = jnp.sum(g_scr[...], axis=1)
            gw_ref[CW:32, :] = jnp.zeros((32 - CW, DA), F32)

    nh = tc // HALO
    nhb = t // HALO
    cur = lambda cidx: pl.BlockSpec((tc, 512), lambda i: (i, cidx))
    halo = lambda cidx: pl.BlockSpec((HALO, 512), lambda i: (jnp.maximum(i * nh - 1, 0), cidx))
    full = pl.BlockSpec((tc, DA), lambda i: (i, 0))
    return pl.pallas_call(
        body, name="conv_bwd_dw", grid=(nt,),
        in_specs=[full, pl.BlockSpec((HALO, DA), lambda i: (jnp.minimum((i + 1) * nh, nhb - 1), 0)),
                  cur(5), cur(6), cur(7), cur(8), halo(5), halo(6), halo(7), halo(8),
                  pl.BlockSpec((CW, 8, DA), lambda i: (0, 0, 0))],
        out_specs=[full, full, pl.BlockSpec((32, DA), lambda i: (0, 0))],
        out_shape=[S((t, DA), BF16), S((t, DA), BF16), S((32, DA), F32)],
        scratch_shapes=[pltpu.VMEM((DA // 128, HALO + tc, 128), F32), pltpu.VMEM((DA // 128, tc + HALO, 128), F32),
                        pltpu.VMEM((CW, 8, DA), F32)],
        compiler_params=_cp(("arbitrary",), 48),
    )(duc, duc, *([proj] * 8), cwb)


def _attn_bwd(proj, dao, attn_o, lse, rel_bias, sinks, bucket):
    t = proj.shape[0]

    def body(rel_ref, sink_ref, bk_ref, q_ref, kvc_ref, kvp_ref, do_ref, o_ref, lse_ref,
             dq_ref, dcur_ref, dprev_ref, dsacc_ref, dsk_ref, bias_ref):
        n = pl.program_id(0)

        @pl.when(n == 0)
        def _():
            _build_bias(rel_ref, bk_ref, bias_ref)
            dsacc_ref[...] = jnp.zeros_like(dsacc_ref)
            dsk_ref[...] = jnp.zeros_like(dsk_ref)

        tbl = jnp.where(n == 0, 0, 1)
        kv = jnp.concatenate([kvp_ref[...], kvc_ref[...]], axis=0)
        for h in range(NKV):
            k_h = kv[:, h * HD:(h + 1) * HD]
            v_h = kv[:, NKV * HD + h * HD:NKV * HD + (h + 1) * HD]
            dk = jnp.zeros((2 * BLK, HD), F32)
            dv = jnp.zeros((2 * BLK, HD), F32)
            for g in range(GRP):
                hd = GRP * h + g
                cols = slice(hd * HD, (hd + 1) * HD)
                q = q_ref[:, cols] * SCALE
                do = do_ref[:, cols]
                lse_h = lse_ref[:, hd:hd + 1]
                s = lax.dot_general(q, k_h, (((1,), (1,)), ((), ())), preferred_element_type=F32)
                p = jnp.exp(s + bias_ref[tbl, hd] - lse_h)
                dp = lax.dot_general(do, v_h, (((1,), (1,)), ((), ())), preferred_element_type=F32)
                delta = jnp.sum(do.astype(F32) * o_ref[:, cols].astype(F32), axis=-1, keepdims=True)
                ds = p * (dp - delta)
                dsacc_ref[hd] += ds
                dsk_ref[:, hd:hd + 1] += -jnp.exp(sink_ref[0, hd] - lse_h) * delta
                dsb = ds.astype(BF16)
                dq_ref[:, cols] = (jnp.dot(dsb, k_h, preferred_element_type=F32) * SCALE).astype(BF16)
                dk = dk + lax.dot_general(dsb, q, (((0,), (0,)), ((), ())), preferred_element_type=F32)
                dv = dv + lax.dot_general(p.astype(BF16), do, (((0,), (0,)), ((), ())), preferred_element_type=F32)
            kc = slice(h * HD, (h + 1) * HD)
            vc = slice(NKV * HD + h * HD, NKV * HD + (h + 1) * HD)
            dprev_ref[:, kc] = dk[0:BLK]
            dcur_ref[:, kc] = dk[BLK:2 * BLK]
            dprev_ref[:, vc] = dv[0:BLK]
            dcur_ref[:, vc] = dv[BLK:2 * BLK]

        @pl.when(n == t // BLK - 1)
        def _():
            dsk_ref[0:1, :] = jnp.sum(dsk_ref[...], axis=0, keepdims=True)

    blk = lambda w, cidx: pl.BlockSpec((BLK, w), lambda n: (n, cidx))
    return pl.pallas_call(
        body, name="attn_bwd", grid=(t // BLK,),
        in_specs=[_SMEM, _SMEM, pl.BlockSpec((BLK, 2 * BLK), lambda n: (0, 0)),
                  blk(DA, 0), blk(512, 2), pl.BlockSpec((BLK, 512), lambda n: (jnp.maximum(n - 1, 0), 2)),
                  blk(DA, 0), blk(DA, 0), pl.BlockSpec((BLK, NQ), lambda n: (n, 0))],
        out_specs=[blk(DA, 0), blk(512, 0), blk(512, 0),
                   pl.BlockSpec((NQ, BLK, 2 * BLK), lambda n: (0, 0, 0)), pl.BlockSpec((BLK, NQ), lambda n: (0, 0))],
        out_shape=[S((t, DA), BF16), S((t, 512), F32), S((t, 512), F32), S((NQ, BLK, 2 * BLK), F32), S((BLK, NQ), F32)],
        scratch_shapes=[pltpu.VMEM((2, NQ, BLK, 2 * BLK), F32)],
        compiler_params=_cp(("arbitrary",), 40),
    )(rel_bias, sinks, bucket, proj, proj, proj, dao, attn_o, lse)


def _dkv_fix(dcur, dprev):
    t = dcur.shape[0]
    tb = min(1024, t)
    nt = t // tb
    per = tb // BLK

    def body(c_ref, p_ref, pn_ref, o_ref):
        if tb > BLK:
            o_ref[0:tb - BLK, :] = (c_ref[0:tb - BLK, :] + p_ref[BLK:tb, :]).astype(BF16)
        nxt = jnp.where(pl.program_id(0) == nt - 1, 0.0, pn_ref[...])
        o_ref[tb - BLK:tb, :] = (c_ref[tb - BLK:tb, :] + nxt).astype(BF16)

    tile = pl.BlockSpec((tb, 512), lambda i: (i, 0))
    return pl.pallas_call(
        body, name="dkv_fix", grid=(nt,),
        in_specs=[tile, tile, pl.BlockSpec((BLK, 512), lambda i: (jnp.minimum((i + 1) * per, t // BLK - 1), 0))],
        out_specs=tile,
        out_shape=S((t, 512), BF16),
        compiler_params=_cp(("parallel",), 32),
    )(dcur, dprev, dprev)


def _bias_grad(dsacc, bucket):
    def body(ds_ref, bk_ref, o_ref, row_scr):
        bk = bk_ref[...]

        def head(hd, carry):
            dsv = ds_ref[hd]
            for b in range(N_BUCKETS):
                row_scr[hd, b:b + 1, :] = jnp.sum(jnp.where(bk == b, dsv, 0.0), axis=0, keepdims=True)
            return carry

        lax.fori_loop(0, NQ, head, 0)
        for hd in range(NQ):
            o_ref[hd] = jnp.sum(row_scr[hd], axis=-1, keepdims=True)

    return pl.pallas_call(
        body, name="bias_grad", out_shape=S((NQ, N_BUCKETS, 1), F32),
        in_specs=[_VMEM, _VMEM], out_specs=_VMEM,
        scratch_shapes=[pltpu.VMEM((NQ, N_BUCKETS, 2 * BLK), F32)],
        compiler_params=_cp(None, 32),
    )(dsacc, bucket)


def _dh_gradx(dproj, wg, gx0, x, scale1p, p_in, p_out, p_pw):
    t = x.shape[0]
    tm = min(256, t)
    ni = t // tm
    bn = D // 2
    nn = D // bn

    def body(dp_ref, w_ref, gx_ref, x_ref, sc_ref, pi, po, pp, out_ref, st_ref, li, lo, lp, send_sems, recv_sems):
        n, i = pl.program_id(0), pl.program_id(1)

        @pl.when((n == 0) & (i == 0))
        def _():
            for cp in _chip_exchange_copies(pi, po, pp, li, lo, lp, send_sems, recv_sems):
                cp.start()

        @pl.when(i == 0)
        def _():
            st_ref[...] = jnp.zeros_like(st_ref)

        dh = None
        for j in range(NCHIP):
            part = lax.dot_general(dp_ref[:, j * NB:(j + 1) * NB], w_ref[j], (((1,), (1,)), ((), ())),
                                   preferred_element_type=F32)
            dh = part if dh is None else dh + part
        out_ref[...] = gx_ref[...] + dh * sc_ref[...]
        st_ref[0:1, :] += jnp.sum(dh, axis=0, keepdims=True)
        st_ref[1:2, :] += jnp.sum(dh * x_ref[...], axis=0, keepdims=True)

        @pl.when((n == nn - 1) & (i == ni - 1))
        def _():
            cps = _chip_exchange_copies(pi, po, pp, li, lo, lp, send_sems, recv_sems)
            for cp in cps:
                cp.wait_recv()
            for cp in cps:
                cp.wait_send()

    tile = pl.BlockSpec((tm, bn), lambda n, i: (i, n))
    return pl.pallas_call(
        body, name="dh_gradx", grid=(nn, ni),
        in_specs=[pl.BlockSpec((tm, D_IN), lambda n, i: (i, 0)), pl.BlockSpec((NCHIP, bn, NB), lambda n, i: (0, n, 0)),
                  tile, tile, pl.BlockSpec((1, bn), lambda n, i: (0, n)), _ANY, _ANY, _ANY],
        out_specs=[tile, pl.BlockSpec((8, bn), lambda n, i: (0, n)), _ANY, _ANY, _ANY],
        out_shape=[S((t, D), F32), S((8, D), F32), S((3, p_in.shape[0], NB), p_in.dtype),
                   S((3,) + p_out.shape[1:], p_out.dtype), S((3,) + p_pw.shape[1:], p_pw.dtype)],
        scratch_shapes=[pltpu.SemaphoreType.DMA((9,)), pltpu.SemaphoreType.DMA((9,))],
        compiler_params=_cp(("arbitrary", "arbitrary"), 56),
    )(dproj, wg, gx0, x, scale1p, p_in, p_out, p_pw)


def _atb(a_parts, b, bn, name):
    bm = DA
    t = b.shape[0]
    n = b.shape[1]
    tk = min(2048, t)
    nk = t // tk
    na = len(a_parts)

    def body(*refs):
        a_refs, b_ref, o_ref = refs[:na], refs[na], refs[na + 1]
        mi, k = pl.program_id(0), pl.program_id(2)
        for q in range(na):
            @pl.when(mi == q)
            def _(q=q):
                part = lax.dot_general(a_refs[q][...], b_ref[...], (((0,), (0,)), ((), ())), preferred_element_type=F32)

                @pl.when(k == 0)
                def _():
                    o_ref[...] = part

                @pl.when(k > 0)
                def _():
                    o_ref[...] += part

    a_spec = lambda q, cidx: pl.BlockSpec((tk, bm), lambda mi, j, k: (jnp.where(mi == q, k, 0), cidx))
    return pl.pallas_call(
        body, name=name, grid=(na, n // bn, nk),
        in_specs=[a_spec(q, cidx) for q, (_, cidx) in enumerate(a_parts)] + [pl.BlockSpec((tk, bn), lambda mi, j, k: (k, j))],
        out_specs=pl.BlockSpec((bm, bn), lambda mi, j, k: (mi, j)),
        out_shape=S((na * bm, n), F32),
        compiler_params=_cp(("parallel", "parallel", "arbitrary"), 56),
    )(*[arr for arr, _ in a_parts], b)


def _sum8(parts):
    _, r, n = parts.shape

    def body(p_ref, o_ref):
        v = p_ref[0]
        for d in range(1, 8):
            v = v + p_ref[d]
        o_ref[...] = v

    return pl.pallas_call(body, name="sum8", out_shape=S((r, n), F32), in_specs=[_VMEM], out_specs=_VMEM,
                          compiler_params=_cp(None, 32))(parts)


def _adam_math(w, g, m, v):
    m = B1 * m + (1.0 - B1) * g
    v = B2 * v + (1.0 - B2) * (g * g)
    m_hat = m / (1.0 - B1 ** STEP)
    v_hat = v / (1.0 - B2 ** STEP)
    delta = -LR * (m_hat / (jnp.sqrt(v_hat) + EPS) + WD * w)
    return delta, m, v


def _adamw(w, g, m, v, name):
    r, n = w.shape
    tr = min(256, r)

    def body(w_ref, g_ref, m_ref, v_ref, d_ref, nm_ref, nv_ref):
        d_ref[...], nm_ref[...], nv_ref[...] = _adam_math(w_ref[...], g_ref[...], m_ref[...], v_ref[...])

    spec = pl.BlockSpec((tr, n), lambda i: (i, 0))
    return pl.pallas_call(
        body, name=name, grid=(r // tr,), in_specs=[spec] * 4, out_specs=[spec] * 3,
        out_shape=[S((r, n), F32)] * 3, compiler_params=_cp(("parallel",), 48),
    )(w, g, m, v)


def _adamw_ada(ct, dmod_p, w, m, v):
    r, n = w.shape
    tr = min(256, r)

    def body(ct_ref, dm_ref, w_ref, m_ref, v_ref, g_ref, d_ref, nm_ref, nv_ref):
        cv = ct_ref[...]
        g = jnp.dot(cv * _sig(cv), dm_ref[...], preferred_element_type=F32)
        g_ref[...] = g
        d_ref[...], nm_ref[...], nv_ref[...] = _adam_math(w_ref[...], g, m_ref[...], v_ref[...])

    spec = pl.BlockSpec((tr, n), lambda i: (i, 0))
    return pl.pallas_call(
        body, name="adamw_ada", grid=(r // tr,),
        in_specs=[pl.BlockSpec((tr, 8), lambda i: (i, 0)), pl.BlockSpec((8, n), lambda i: (0, 0)), spec, spec, spec],
        out_specs=[spec] * 4, out_shape=[S((r, n), F32)] * 4, compiler_params=_cp(("parallel",), 48),
    )(ct, dmod_p, w, m, v)


def _adamw_small(ws, gs, ms, vs):
    k = len(ws)

    def body(*refs):
        ins, outs = refs[:4 * k], refs[4 * k:]
        for i in range(k):
            d, nm, nv = _adam_math(ins[i][...], ins[k + i][...], ins[2 * k + i][...], ins[3 * k + i][...])
            outs[i][...] = d
            outs[k + i][...] = nm
            outs[2 * k + i][...] = nv

    shapes = [S(w.shape, F32) for w in ws]
    return pl.pallas_call(body, name="adamw_small", out_shape=shapes * 3, in_specs=[_VMEM] * (4 * k),
                          out_specs=[_VMEM] * (3 * k), compiler_params=_cp(None, 32))(*ws, *gs, *ms, *vs)


def _bucket_map():
    qi = jnp.arange(BLK, dtype=jnp.int32)[:, None]
    kj = jnp.arange(2 * BLK, dtype=jnp.int32)[None, :]
    dist = qi + BLK - kj
    dd = jnp.maximum(dist, 0)
    max_exact = N_BUCKETS // 2
    dfl = jnp.maximum(dd, 1).astype(F32)
    large = max_exact + (jnp.log(dfl / max_exact) / math.log(MAX_DIST / max_exact) * (N_BUCKETS - max_exact)).astype(jnp.int32)
    large = jnp.minimum(large, N_BUCKETS - 1)
    bucket = jnp.where(dd < max_exact, dd, large)
    return jnp.where((dist >= 0) & (dist < BLK), bucket, -1).astype(jnp.int32)


def _pad_rows(a, rows):
    return jnp.pad(a, ((0, rows - a.shape[0]), (0, 0)))


def kernel(x, c, w_ada, b_ada, w_in, rel_bias, sinks, conv_w, conv_b, conv_ln_g, conv_ln_b, w_pw, b_pw, w_out, ln_g, ln_b, loss_target, m_w_ada, m_b_ada, m_w_in, m_rel_bias, m_sinks, m_conv_w, m_conv_b, m_conv_ln_g, m_conv_ln_b, m_w_pw, m_b_pw, m_w_out, m_ln_g, m_ln_b, v_w_ada, v_b_ada, v_w_in, v_rel_bias, v_sinks, v_conv_w, v_conv_b, v_conv_ln_g, v_conv_ln_b, v_w_pw, v_b_pw, v_w_out, v_ln_g, v_ln_b):
    mx, my, mc = _me()
    chip = 2 * mx + my
    dev = 2 * chip + mc
    t = x.shape[1]
    x2 = x.reshape(t, D)
    tgt = loss_target.reshape(t, D)
    n_ada = w_ada.shape[2]
    cw_cols = conv_w.shape[2]

    pack0 = jnp.concatenate([_pad_rows(c, 8), _pad_rows(_pad_rows(conv_w[0], 32).reshape(-1, D), 8)], axis=0)
    g0 = _allgather8(pack0, "gather_c_convw").reshape(8, 16, D)
    c_all = g0[:, 0, :]
    cw_rows = 32 * cw_cols // D
    cw_full = jnp.concatenate([g0[2 * q, 8:8 + cw_rows, :].reshape(32, cw_cols) for q in range(NCHIP)], axis=1)[:CW]
    cwb = jnp.broadcast_to(cw_full[:, None, :], (CW, 8, DA))

    b_ada_p = lax.dynamic_slice(b_ada, (0, chip * n_ada), (1, n_ada))
    mod_part = _ada_fwd(c_all, w_ada[0], b_ada_p)
    mod_all = _allgather8(mod_part, "gather_mod").reshape(8, 8, n_ada)
    mod = jnp.concatenate([lax.dynamic_slice(mod_all[2 * q], (dev, 0), (1, n_ada)) for q in range(NCHIP)], axis=1)
    shift, scale, gate = mod[:, 0:D], mod[:, D:2 * D], mod[:, 2 * D:3 * D]
    scale1p = 1.0 + scale

    bucket = _bucket_map()
    sel_chip = jnp.reshape(chip, (1,)).astype(jnp.int32)
    sel = jnp.reshape(mc, (1,)).astype(jnp.int32)
    proj, h, wg_in, wg_out, wg_pw = _inproj_gather(
        x2, scale1p, shift, w_in[0].astype(BF16), w_out[0].astype(BF16), w_pw[0].astype(BF16), sel_chip)
    w_out_f = wg_out.reshape(D, D)
    w_pw_f = wg_pw.reshape(DA, DA)
    ya, attn_o, lse = _attn_fwd(proj, rel_bias, sinks, bucket)
    uc, pw, yc = _conv_fwd(proj, cwb, conv_b, conv_ln_g, conv_ln_b, w_pw_f, b_pw)
    gx0, dy, st_out = _outproj_loss(ya, yc, w_out_f, x2, tgt, gate, ln_g, ln_b)

    dao, dga, dpw, dgc = _dycat_gates(dy, w_out_f, attn_o, pw, proj)
    duc, sw, st_conv = _conv_bwd_ln(dpw, w_pw_f, uc, conv_ln_g, conv_ln_b)
    dglu_a, dglu_b, gw_conv = _conv_bwd_dw(duc, proj, cwb)
    dq, dkv_cur, dkv_prev, dsacc, dsk = _attn_bwd(proj, dao, attn_o, lse, rel_bias, sinks, bucket)
    dkv = _dkv_fix(dkv_cur, dkv_prev)
    g_rel = _bias_grad(dsacc, bucket).reshape(NQ, N_BUCKETS).T
    dproj = jnp.concatenate([dq, dkv, dga, dglu_a, dglu_b, dgc], axis=1)
    gp_in = _atb([(h, 0), (h, 1)], dproj, NB, "grad_w_in")
    gp_out = _atb([(ya, 0), (yc, 0)], dy, 1024, "grad_w_out")
    gp_pw = _atb([(sw, 0)], dpw, 1024, "grad_w_pw")

    r_out, r_pw = D // NCHIP // 2, DA // NCHIP // 2
    gi3, go3, gp3 = gp_in.reshape(2, D // 2, D_IN), gp_out.reshape(8, r_out, D), gp_pw.reshape(8, r_pw, DA)
    l_in, l_out, l_pw = _pair_exchange(gi3, go3, gp3)
    pi32, pi16 = _pair_sum(gi3, l_in.reshape(1, D // 2, D_IN), sel, "pair_sum_in")
    po32, po16 = _pair_sum(go3, l_out, sel, "pair_sum_out")
    pp32, pp16 = _pair_sum(gp3, l_pw, sel, "pair_sum_pw")
    grad_x, st_in, rc_in, rc_out, rc_pw = _dh_gradx(dproj, wg_in, gx0, x2, scale1p, pi16[0], po16, pp16)
    own_in = lax.dynamic_slice(pi32[0], (0, chip * NB), (D // 2, NB))
    own_out = lax.dynamic_index_in_dim(po32, chip, 0, keepdims=False)
    own_pw = lax.dynamic_index_in_dim(pp32, chip, 0, keepdims=False)
    h_in = _chip_sum(own_in, rc_in, sel, "chip_sum_in")
    h_out = _chip_sum(own_out, rc_out, sel, "chip_sum_out")
    h_pw = _chip_sum(own_pw, rc_pw, sel, "chip_sum_pw")
    f_in, f_out, f_pw = _pair_share(h_in, h_out, h_pw)
    g_w_in = f_in.reshape(D, NB)
    g_w_out = f_out.reshape(D // NCHIP, D)
    g_w_pw = f_pw.reshape(DA // NCHIP, DA)

    dmod = jnp.concatenate([st_in[0:1], st_in[1:2], st_out[2:3]], axis=1)
    loss_row = st_out[3:4, 0:1]
    small = jnp.concatenate([
        dmod, st_out[0:1], st_out[1:2],
        st_conv[0:1], st_conv[1:2], st_conv[2:3], st_conv[3:4],
        g_rel.reshape(1, N_BUCKETS * NQ), dsk[0:1], loss_row,
        gw_conv[:CW].reshape(1, CW * DA)], axis=1)
    n_small = small.shape[1]
    rows_small = -(-n_small // (8 * D)) * 8
    small = jnp.pad(small, ((0, 0), (0, rows_small * D - n_small))).reshape(rows_small, D)
    parts = _allgather8(small, "gather_small").reshape(8, rows_small, D)
    tot = _sum8(parts).reshape(1, rows_small * D)
    dmod_all = parts.reshape(8, rows_small * D)[:, 0:3 * D]

    o = 3 * D
    def take(nn):
        nonlocal o
        v = tot[:, o:o + nn]
        o += nn
        return v
    g_b_ada = tot[:, 0:3 * D]
    g_ln_g, g_ln_b = take(D), take(D)
    g_cln_g, g_cln_b, g_conv_b, g_b_pw = take(DA), take(DA), take(DA), take(DA)
    g_rel_bias = take(N_BUCKETS * NQ).reshape(N_BUCKETS, NQ)
    g_sinks = take(NQ)
    loss = take(1).reshape(())
    g_conv_w_full = take(CW * DA).reshape(CW, DA)
    g_conv_w = lax.dynamic_slice(g_conv_w_full, (0, chip * cw_cols), (CW, cw_cols))

    dmod_p = lax.dynamic_slice(dmod_all, (0, chip * n_ada), (8, n_ada))
    g_w_ada, d_w_ada, nm_w_ada, nv_w_ada = _adamw_ada(c_all.T, dmod_p, w_ada[0], m_w_ada[0], v_w_ada[0])
    d_w_in, nm_w_in, nv_w_in = _adamw(w_in[0], g_w_in, m_w_in[0], v_w_in[0], "adamw_in")
    d_w_out, nm_w_out, nv_w_out = _adamw(w_out[0], g_w_out, m_w_out[0], v_w_out[0], "adamw_out")
    d_w_pw, nm_w_pw, nv_w_pw = _adamw(w_pw[0], g_w_pw, m_w_pw[0], v_w_pw[0], "adamw_pw")
    small_w = [b_ada, rel_bias, sinks, conv_w[0], conv_b, conv_ln_g, conv_ln_b, b_pw, ln_g, ln_b]
    small_g = [g_b_ada, g_rel_bias, g_sinks, g_conv_w, g_conv_b, g_cln_g, g_cln_b, g_b_pw, g_ln_g, g_ln_b]
    small_m = [m_b_ada, m_rel_bias, m_sinks, m_conv_w[0], m_conv_b, m_conv_ln_g, m_conv_ln_b, m_b_pw, m_ln_g, m_ln_b]
    small_v = [v_b_ada, v_rel_bias, v_sinks, v_conv_w[0], v_conv_b, v_conv_ln_g, v_conv_ln_b, v_b_pw, v_ln_g, v_ln_b]
    res = _adamw_small(small_w, small_g, small_m, small_v)
    ns = len(small_w)
    d_s, nm_s, nv_s = res[:ns], res[ns:2 * ns], res[2 * ns:]

    def ordered(w_ada_, w_in_, w_pw_, w_out_, sm):
        b_ada_, rel_, sinks_, conv_w_, conv_b_, cln_g_, cln_b_, b_pw_, ln_g_, ln_b_ = sm
        return (w_ada_[None], b_ada_, w_in_[None], rel_, sinks_, conv_w_[None], conv_b_, cln_g_, cln_b_,
                w_pw_[None], b_pw_, w_out_[None], ln_g_, ln_b_)

    grads = ordered(g_w_ada, g_w_in, g_w_pw, g_w_out, small_g)
    deltas = ordered(d_w_ada, d_w_in, d_w_pw, d_w_out, d_s)
    new_m = ordered(nm_w_ada, nm_w_in, nm_w_pw, nm_w_out, nm_s)
    new_v = ordered(nv_w_ada, nv_w_in, nv_w_pw, nv_w_out, nv_s)
    return (loss, grad_x.reshape(1, t, D), *grads, *deltas, *new_m, *new_v)
```

```python
import functools
import math

import jax
import jax.numpy as jnp
from jax import lax
from jax.experimental import pallas as pl
from jax.experimental.pallas import tpu as pltpu

F32, BF16 = jnp.float32, jnp.bfloat16
S = jax.ShapeDtypeStruct
MESH = pl.DeviceIdType.MESH

D = 2048
DA = 1024
HD = 64
NQ, NKV, GRP = 16, 4, 4
BLK = 128
CW = 31
HALO = 32
D_IN = 5632
NCHIP = 4
NB = D_IN // NCHIP
N_BUCKETS, MAX_DIST = 32, 128
LN_EPS = 1e-5
ALPHA = 2.0 ** 0.25
SCALE = HD ** -0.5
NEG = -1e30
LR, B1, B2, EPS, WD, STEP = 0.001, 0.9, 0.999, 1e-08, 0.01, 10

_VMEM = pl.BlockSpec(memory_space=pltpu.VMEM)
_SMEM = pl.BlockSpec(memory_space=pltpu.SMEM)
_ANY = pl.BlockSpec(memory_space=pl.ANY)


def _cp(sem=None, vmem_mb=None):
    kw = {}
    if sem is not None:
        kw["dimension_semantics"] = sem
    if vmem_mb is not None:
        kw["vmem_limit_bytes"] = vmem_mb * 1024 * 1024
    return pltpu.CompilerParams(**kw)


def _sig(v):
    return jax.nn.sigmoid(v)


def _dsilu(g, sg):
    return sg * (1.0 + g * (1.0 - sg))


def _me():
    return lax.axis_index("x"), lax.axis_index("y"), lax.axis_index("c")


def _allgather8(x_shard, name):
    m_per, n = x_shard.shape

    def body(x_ref, out_ref, send_sems, recv_sems, local_sem):
        x, y, c = _me()
        me, sibling = (x, y, c), (x, y, 1 - c)
        chips = [(1 - x, y), (x, 1 - y), (1 - x, 1 - y)]

        def rows(px, py, pc):
            return out_ref.at[pl.ds((4 * px + 2 * py + pc) * m_per, m_per), :]

        def copy(k, block, to, src=None):
            return pltpu.make_async_remote_copy(
                src_ref=rows(*block) if src is None else src, dst_ref=rows(*block),
                send_sem=send_sems.at[k], recv_sem=recv_sems.at[k], device_id=to, device_id_type=MESH)

        mine = pltpu.make_async_copy(x_ref, rows(*me), local_sem)
        mine.start()
        first = [copy(0, me, sibling, src=x_ref)]
        first += [copy(1 + j, me, (*chip, c), src=x_ref) for j, chip in enumerate(chips)]
        for cp in first:
            cp.start()
        passed = [copy(4 + j, (*chip, c), sibling) for j, chip in enumerate(chips)]
        for j, chip in enumerate(chips):
            copy(1 + j, (*chip, c), me).wait_recv()
            passed[j].start()
        copy(0, sibling, me).wait_recv()
        for j, chip in enumerate(chips):
            copy(4 + j, (*chip, 1 - c), me).wait_recv()
        for cp in first + passed:
            cp.wait_send()
        mine.wait()

    return pl.pallas_call(
        body, name=name, out_shape=S((8 * m_per, n), x_shard.dtype),
        in_specs=[_VMEM], out_specs=_VMEM,
        scratch_shapes=[pltpu.SemaphoreType.DMA((7,)), pltpu.SemaphoreType.DMA((7,)), pltpu.SemaphoreType.DMA],
    )(x_shard)


def _inproj_gather(x, scale1p, shift, wi, wo, wp, chip_id):
    t = x.shape[0]
    tm = min(512, t)
    ni = t // tm
    shards = (wi, wo, wp)
    halves = [s.shape[0] // 2 for s in shards]

    def body(cid_ref, x_ref, sc_ref, sh_ref, wi_ref, wo_ref, wp_ref, proj_ref, h_ref, gi_ref, go_ref, gp_ref,
             wbuf, obuf, pbuf, send_sems, recv_sems, ld_sems, st_sems):
        jj, i = pl.program_id(0), pl.program_id(1)
        mx, my, c = _me()
        p = 2 * mx + my
        sibling = (mx, my, 1 - c)
        srcs, dsts, bufs = (wi_ref, wo_ref, wp_ref), (gi_ref, go_ref, gp_ref), (wbuf, obuf, pbuf)
        chips = {1: (mx, 1 - my), 2: (1 - mx, my), 3: (1 - mx, 1 - my)}

        def half(ref, w, hc):
            return ref.at[pl.ds(hc * halves[w], halves[w])]

        def copy(k, src, dst, to):
            return pltpu.make_async_remote_copy(src_ref=src, dst_ref=dst, send_sem=send_sems.at[k],
                                                recv_sem=recv_sems.at[k], device_id=to, device_id_type=MESH)

        def sent(w, m):
            return copy(3 * w + m - 1, half(srcs[w], w, c), half(dsts[w].at[p], w, c), (*chips[m], c))

        def landed(w, m):
            blk = half(dsts[w].at[jnp.bitwise_xor(p, m)], w, c)
            return copy(3 * w + m - 1, blk, blk, (*chips[m], c))

        def passed(w, m, hc):
            blk = half(dsts[w].at[jnp.bitwise_xor(p, m)], w, hc)
            return copy(9 + 3 * w + m - 1, blk, blk, sibling)

        def vm(w, slot):
            return bufs[w].at[slot] if w == 0 else bufs[w]

        def load(w, src, slot=0):
            return pltpu.make_async_copy(src, vm(w, slot), ld_sems.at[w])

        def store(w):
            return pltpu.make_async_copy(vm(w, 0), dsts[w].at[p], st_sems.at[w])

        def block_load(m):
            return load(0, gi_ref.at[jnp.bitwise_xor(p, m)], m % 2)

        @pl.when((jj == 0) & (i == 0))
        def _():
            sent(0, 1).start()
            sent(0, 2).start()
            for w in range(3):
                load(w, srcs[w]).start()
            for w in range(3):
                load(w, srcs[w]).wait()
                store(w).start()

        for m in (1, 2, 3):
            @pl.when((jj == m) & (i == 0))
            def _(m=m):
                if m == 1:
                    sent(0, 3).start()
                    store(0).wait()
                if m == 2:
                    for w in (1, 2):
                        for mm in (1, 2, 3):
                            sent(w, mm).start()
                block_load(m).wait()

        hb = (x_ref[...] * sc_ref[...] + sh_ref[...]).astype(BF16)

        @pl.when(jj == 0)
        def _():
            h_ref[...] = hb

        proj_ref[...] = jnp.dot(hb, wbuf[jj % 2], preferred_element_type=F32).astype(BF16)

        for m in (1, 2, 3):
            @pl.when((jj == m - 1) & (i == ni - 1))
            def _(m=m):
                landed(0, m).wait_recv()
                passed(0, m, c).start()
                passed(0, m, 1 - c).wait_recv()
                block_load(m).start()

        @pl.when((jj == NCHIP - 1) & (i == ni - 1))
        def _():
            for w in (1, 2):
                for m in (1, 2, 3):
                    landed(w, m).wait_recv()
                    passed(w, m, c).start()
            for w in (1, 2):
                for m in (1, 2, 3):
                    passed(w, m, 1 - c).wait_recv()
            for w in range(3):
                for m in (1, 2, 3):
                    sent(w, m).wait_send()
                    passed(w, m, c).wait_send()
            store(1).wait()
            store(2).wait()

    row = pl.BlockSpec((1, D), lambda jj, i, s: (0, 0))
    return pl.pallas_call(
        body, name="inproj_gather",
        grid_spec=pltpu.PrefetchScalarGridSpec(
            num_scalar_prefetch=1, grid=(NCHIP, ni),
            in_specs=[pl.BlockSpec((tm, D), lambda jj, i, s: (i, 0)), row, row, _ANY, _ANY, _ANY],
            out_specs=[pl.BlockSpec((tm, NB), lambda jj, i, s: (i, jnp.bitwise_xor(s[0], jj))),
                       pl.BlockSpec((tm, D), lambda jj, i, s: (jnp.where(jj == 0, i, ni - 1), 0)),
                       _ANY, _ANY, _ANY],
            scratch_shapes=[pltpu.VMEM((2,) + wi.shape, BF16), pltpu.VMEM(wo.shape, BF16), pltpu.VMEM(wp.shape, BF16),
                            pltpu.SemaphoreType.DMA((18,)), pltpu.SemaphoreType.DMA((18,)),
                            pltpu.SemaphoreType.DMA((3,)), pltpu.SemaphoreType.DMA((3,))]),
        out_shape=[S((t, D_IN), BF16), S((t, D), BF16)] + [S((NCHIP,) + s.shape, s.dtype) for s in shards],
        compiler_params=_cp(("arbitrary", "arbitrary"), 48),
    )(chip_id, x, scale1p, shift, wi, wo, wp)


def _pair_exchange(g_in, g_out, g_pw):
    def body(gi, go, gp, li, lo, lp, send_sems, recv_sems):
        x, y, c = _me()
        sibling = (x, y, 1 - c)
        copies = [(gi.at[1 - c], li)]
        for j in range(NCHIP):
            copies.append((go.at[2 * j + 1 - c], lo.at[j]))
        for j in range(NCHIP):
            copies.append((gp.at[2 * j + 1 - c], lp.at[j]))
        cps = [pltpu.make_async_remote_copy(src_ref=s_, dst_ref=d_, send_sem=send_sems.at[k], recv_sem=recv_sems.at[k],
                                            device_id=sibling, device_id_type=MESH) for k, (s_, d_) in enumerate(copies)]
        for cp in cps:
            cp.start()
        for cp in cps:
            cp.wait_recv()
        for cp in cps:
            cp.wait_send()

    n = 1 + 2 * NCHIP
    return pl.pallas_call(
        body, name="grad_pair_exchange",
        out_shape=[S(g_in.shape[1:], g_in.dtype), S((NCHIP,) + g_out.shape[1:], g_out.dtype),
                   S((NCHIP,) + g_pw.shape[1:], g_pw.dtype)],
        in_specs=[_ANY] * 3, out_specs=[_ANY] * 3,
        scratch_shapes=[pltpu.SemaphoreType.DMA((n,)), pltpu.SemaphoreType.DMA((n,))],
    )(g_in, g_out, g_pw)


def _chip_exchange_copies(pi, po, pp, li, lo, lp, send_sems, recv_sems):
    x, y, c = _me()
    chips = [(1 - x, y), (x, 1 - y), (1 - x, 1 - y)]
    cps = []
    for j, chip in enumerate(chips):
        q = 2 * chip[0] + chip[1]
        pairs = ((pi.at[:, pl.ds(q * NB, NB)], li.at[j]), (po.at[q], lo.at[j]), (pp.at[q], lp.at[j]))
        for w, (src, dst) in enumerate(pairs):
            k = 3 * j + w
            cps.append(pltpu.make_async_remote_copy(src_ref=src, dst_ref=dst, send_sem=send_sems.at[k],
                                                    recv_sem=recv_sems.at[k], device_id=(*chip, c), device_id_type=MESH))
    return cps


def _pair_share(f_in, f_out, f_pw):
    bufs = (f_in, f_out, f_pw)

    def body(ai, ao, ap, fi, fo, fp, send_sems, recv_sems):
        x, y, c = _me()
        sibling = (x, y, 1 - c)

        def copy(w, ref, hc):
            return pltpu.make_async_remote_copy(src_ref=ref.at[hc], dst_ref=ref.at[hc], send_sem=send_sems.at[w],
                                                recv_sem=recv_sems.at[w], device_id=sibling, device_id_type=MESH)

        cps = [copy(w, ref, c) for w, ref in enumerate((fi, fo, fp))]
        for cp in cps:
            cp.start()
        for w, ref in enumerate((fi, fo, fp)):
            copy(w, ref, 1 - c).wait_recv()
        for cp in cps:
            cp.wait_send()

    return pl.pallas_call(
        body, name="grad_pair_share",
        out_shape=[S(b.shape, b.dtype) for b in bufs],
        in_specs=[_ANY] * 3, out_specs=[_ANY] * 3, input_output_aliases={0: 0, 1: 1, 2: 2},
        scratch_shapes=[pltpu.SemaphoreType.DMA((3,)), pltpu.SemaphoreType.DMA((3,))],
    )(*bufs)


def _pair_sum(g, l, sel, name):
    n, r, ccols = l.shape
    tr = min(256 if ccols <= D else 128, r)

    def body(sel_ref, g_ref, l_ref, o32_ref, o16_ref):
        v = g_ref[...] + l_ref[...]
        o32_ref[...] = v
        o16_ref[...] = v.astype(BF16)

    spec_l = pl.BlockSpec((None, tr, ccols), lambda j, i, s: (j, i, 0))
    return pl.pallas_call(
        body, name=name,
        grid_spec=pltpu.PrefetchScalarGridSpec(
            num_scalar_prefetch=1, grid=(n, r // tr),
            in_specs=[pl.BlockSpec((None, tr, ccols), lambda j, i, s: (2 * j + s[0], i, 0)), spec_l],
            out_specs=[spec_l, spec_l]),
        out_shape=[S(l.shape, F32), S(l.shape, BF16)],
        compiler_params=_cp(("parallel", "parallel"), 48),
    )(sel, g, l)


def _chip_sum(own, recv, sel, name):
    r, ccols = own.shape
    tr = min(256, r)

    def body(sel_ref, o_ref, r_ref, out_ref):
        v = o_ref[...]
        for j in range(3):
            v = v + r_ref[j].astype(F32)
        out_ref[...] = v

    return pl.pallas_call(
        body, name=name,
        grid_spec=pltpu.PrefetchScalarGridSpec(
            num_scalar_prefetch=1, grid=(r // tr,),
            in_specs=[pl.BlockSpec((tr, ccols), lambda i, s: (i, 0)), pl.BlockSpec((3, tr, ccols), lambda i, s: (0, i, 0))],
            out_specs=pl.BlockSpec((None, tr, ccols), lambda i, s: (s[0], i, 0))),
        out_shape=S((2, r, ccols), F32),
        compiler_params=_cp(("parallel",), 48),
    )(sel, own, recv)


def _ada_fwd(c_all, w_ada, b_ada_p):
    n = w_ada.shape[1]
    tn = 512

    def body(c_ref, w_ref, b_ref, o_ref):
        cv = c_ref[...]
        ca = cv * _sig(cv)
        o_ref[...] = jnp.dot(ca, w_ref[...], preferred_element_type=F32) + b_ref[...]

    return pl.pallas_call(
        body, name="ada_fwd", grid=(n // tn,),
        in_specs=[pl.BlockSpec((8, D), lambda j: (0, 0)), pl.BlockSpec((D, tn), lambda j: (0, j)),
                  pl.BlockSpec((1, tn), lambda j: (0, j))],
        out_specs=pl.BlockSpec((8, tn), lambda j: (0, j)),
        out_shape=S((8, n), F32),
        compiler_params=_cp(("parallel",), 32),
    )(c_all, w_ada, b_ada_p)


def _build_bias(rel_ref, bk_ref, bias_ref):
    bk = bk_ref[...]
    kj = lax.broadcasted_iota(jnp.int32, (2 * BLK, BLK), 0)
    for hd in range(NQ):
        acc = jnp.full((2 * BLK, BLK), NEG, F32)
        for b in range(N_BUCKETS):
            acc = jnp.where(bk == b, rel_ref[b, hd], acc)
        lanes = slice((hd % GRP) * BLK, (hd % GRP + 1) * BLK)
        bias_ref[1, hd // GRP, :, lanes] = acc
        bias_ref[0, hd // GRP, :, lanes] = jnp.where(kj < BLK, NEG, acc)


def _group_rows(ref, h):
    return jnp.concatenate([ref[:, (GRP * h + g) * HD:(GRP * h + g + 1) * HD] for g in range(GRP)], axis=0)


def _sink_row(sink_ref, h):
    return jnp.concatenate([jnp.full((1, BLK), sink_ref[0, GRP * h + g], F32) for g in range(GRP)], axis=1)


def _attn_fwd(proj, rel_bias, sinks, bucket):
    t = proj.shape[0]

    def body(rel_ref, sink_ref, bk_ref, q_ref, kvc_ref, kvp_ref, glo_ref, ghi_ref, ya_ref, o_ref, lse_ref, bias_ref):
        n = pl.program_id(0)

        @pl.when(n == 0)
        def _():
            _build_bias(rel_ref, bk_ref, bias_ref)

        tbl = jnp.where(n == 0, 0, 1)
        kv = jnp.concatenate([kvp_ref[...], kvc_ref[...]], axis=0)
        for h in range(NKV):
            k_h = kv[:, h * HD:(h + 1) * HD]
            v_h = kv[:, NKV * HD + h * HD:NKV * HD + (h + 1) * HD]
            q4 = _group_rows(q_ref, h) * SCALE
            s = lax.dot_general(k_h, q4, (((1,), (1,)), ((), ())), preferred_element_type=F32) + bias_ref[tbl, h]
            sink = _sink_row(sink_ref, h)
            m = jnp.maximum(jnp.max(s, axis=0, keepdims=True), sink)
            p = jnp.exp(s - m)
            l = jnp.sum(p, axis=0, keepdims=True) + jnp.exp(sink - m)
            pn = (p * (1.0 / l)).astype(BF16)
            o4 = lax.dot_general(pn, v_h, (((0,), (0,)), ((), ())), preferred_element_type=F32)
            lse_ref[h:h + 1, :] = m + jnp.log(l)
            for g in range(GRP):
                hd = GRP * h + g
                cols = slice(hd * HD, (hd + 1) * HD)
                o = o4[g * BLK:(g + 1) * BLK]
                g_ref = glo_ref if hd < NQ // 2 else ghi_ref
                gc = slice((hd % (NQ // 2)) * HD, (hd % (NQ // 2) + 1) * HD)
                gt = g_ref[:, gc].astype(F32)
                o_ref[:, cols] = o.astype(BF16)
                ya_ref[:, cols] = (o * (gt * _sig(gt))).astype(BF16)

    blk = lambda w, cidx: pl.BlockSpec((BLK, w), lambda n: (n, cidx))
    return pl.pallas_call(
        body, name="attn_fwd", grid=(t // BLK,),
        in_specs=[_SMEM, _SMEM, pl.BlockSpec((2 * BLK, BLK), lambda n: (0, 0)),
                  blk(DA, 0), blk(512, 2), pl.BlockSpec((BLK, 512), lambda n: (jnp.maximum(n - 1, 0), 2)),
                  blk(512, 3), blk(512, 4)],
        out_specs=[blk(DA, 0), blk(DA, 0), pl.BlockSpec((None, NKV, GRP * BLK), lambda n: (n, 0, 0))],
        out_shape=[S((t, DA), BF16), S((t, DA), BF16), S((t // BLK, NKV, GRP * BLK), F32)],
        scratch_shapes=[pltpu.VMEM((2, NKV, 2 * BLK, GRP * BLK), F32)],
        compiler_params=_cp(("arbitrary",), 32),
    )(rel_bias, sinks, bucket, proj, proj, proj, proj, proj)


def _conv_rows(t):
    return min(256, t)


def _glu_into(u_scr, i, a_refs, b_refs, ah_refs, bh_refs, tc):
    for cc in range(DA // 128):
        half, lc = cc // 4, slice((cc % 4) * 128, (cc % 4 + 1) * 128)
        uh = ah_refs[half][:, lc].astype(F32) * _sig(bh_refs[half][:, lc].astype(F32))
        u_scr[cc, 0:HALO, :] = jnp.where(i == 0, 0.0, uh)
        u_scr[cc, HALO:HALO + tc, :] = a_refs[half][:, lc].astype(F32) * _sig(b_refs[half][:, lc].astype(F32))


def _conv_fwd(proj, cwb, conv_b, ln_g, ln_b, wpw, b_pw):
    t = proj.shape[0]
    tc = _conv_rows(t)
    rc = min(128, tc)

    def body(alo, ahi, blo, bhi, alo_h, ahi_h, blo_h, bhi_h, glo, ghi, cw_ref, cb_ref, lg_ref, lb_ref, wpw_ref, bpw_ref,
             uc_ref, pw_ref, yc_ref, u_scr):
        i = pl.program_id(0)
        _glu_into(u_scr, i, (alo, ahi), (blo, bhi), (alo_h, ahi_h), (blo_h, bhi_h), tc)

        def rows(r, carry):
            r0 = pl.multiple_of(r * rc, rc)
            for cc in range(DA // 128):
                lanes = pl.ds(cc * 128, 128)
                acc = jnp.zeros((rc // 8, 8, 128), F32)
                for k in range(CW):
                    u = u_scr[cc, pl.ds(r0 + (HALO - CW + 1) + k, rc), :].reshape(rc // 8, 8, 128)
                    acc = acc + u * cw_ref[k, :, lanes]
                uc_ref[pl.ds(r0, rc), lanes] = acc.reshape(rc, 128) + cb_ref[:, lanes]
            return carry

        lax.fori_loop(0, tc // rc, rows, 0)

        uc = uc_ref[...]
        mu = jnp.mean(uc, axis=-1, keepdims=True)
        xc = uc - mu
        rstd = lax.rsqrt(jnp.mean(xc * xc, axis=-1, keepdims=True) + LN_EPS)
        ln = xc * rstd * lg_ref[...] + lb_ref[...]
        sw = (ln * _sig(ln)).astype(BF16)
        pw = jnp.dot(sw, wpw_ref[...], preferred_element_type=F32) + bpw_ref[...]
        pw_ref[...] = pw.astype(BF16)
        gt = jnp.concatenate([glo[...], ghi[...]], axis=1).astype(F32)
        yc_ref[...] = (pw * (gt * _sig(gt))).astype(BF16)

    nh = tc // HALO
    cur = lambda cidx: pl.BlockSpec((tc, 512), lambda i: (i, cidx))
    halo = lambda cidx: pl.BlockSpec((HALO, 512), lambda i: (jnp.maximum(i * nh - 1, 0), cidx))
    row = pl.BlockSpec((1, DA), lambda i: (0, 0))
    full = pl.BlockSpec((tc, DA), lambda i: (i, 0))
    return pl.pallas_call(
        body, name="conv_fwd", grid=(t // tc,),
        in_specs=[cur(5), cur(6), cur(7), cur(8), halo(5), halo(6), halo(7), halo(8), cur(9), cur(10),
                  pl.BlockSpec((CW, 8, DA), lambda i: (0, 0, 0)), row, row, row,
                  pl.BlockSpec((DA, DA), lambda i: (0, 0)), row],
        out_specs=[full, full, full],
        out_shape=[S((t, DA), F32), S((t, DA), BF16), S((t, DA), BF16)],
        scratch_shapes=[pltpu.VMEM((DA // 128, HALO + tc, 128), F32)],
        compiler_params=_cp(("arbitrary",), 48),
    )(*([proj] * 10), cwb, conv_b, ln_g, ln_b, wpw, b_pw)


def _outproj_loss(ya, yc, w_out, x, target, gate, ln_g, ln_b):
    t = x.shape[0]
    tm = min(256, t)

    def body(ya_ref, yc_ref, w_ref, x_ref, t_ref, g_ref, lg_ref, lb_ref, gx_ref, dy_ref, st_ref):
        @pl.when(pl.program_id(0) == 0)
        def _():
            st_ref[...] = jnp.zeros_like(st_ref)

        y = jnp.dot(ya_ref[...], w_ref[0:DA, :], preferred_element_type=F32)
        y = y + jnp.dot(yc_ref[...], w_ref[DA:2 * DA, :], preferred_element_type=F32)
        gate_v = g_ref[...]
        z = ALPHA * x_ref[...] + gate_v * y
        mu = jnp.mean(z, axis=-1, keepdims=True)
        zc = z - mu
        rstd = lax.rsqrt(jnp.mean(zc * zc, axis=-1, keepdims=True) + LN_EPS)
        zh = zc * rstd
        diff = zh * lg_ref[...] + lb_ref[...] - t_ref[...]
        dout = diff * (1.0 / D)
        dzh = dout * lg_ref[...]
        m1 = jnp.mean(dzh, axis=-1, keepdims=True)
        m2 = jnp.mean(dzh * zh, axis=-1, keepdims=True)
        dz = rstd * (dzh - m1 - zh * m2)
        gx_ref[...] = ALPHA * dz
        dy_ref[...] = (dz * gate_v).astype(BF16)
        st_ref[0:1, :] += jnp.sum(dout * zh, axis=0, keepdims=True)
        st_ref[1:2, :] += jnp.sum(dout, axis=0, keepdims=True)
        st_ref[2:3, :] += jnp.sum(dz * y, axis=0, keepdims=True)
        st_ref[3:4, :] += jnp.sum(diff * diff, axis=0, keepdims=True) * (0.5 / D)

        @pl.when(pl.program_id(0) == t // tm - 1)
        def _():
            st_ref[3:4, :] = jnp.broadcast_to(jnp.sum(st_ref[3:4, :], axis=-1, keepdims=True), (1, D))

    row = pl.BlockSpec((1, D), lambda i: (0, 0))
    half = pl.BlockSpec((tm, DA), lambda i: (i, 0))
    full = pl.BlockSpec((tm, D), lambda i: (i, 0))
    return pl.pallas_call(
        body, name="outproj_loss", grid=(t // tm,),
        in_specs=[half, half, pl.BlockSpec((D, D), lambda i: (0, 0)), full, full, row, row, row],
        out_specs=[full, full, pl.BlockSpec((8, D), lambda i: (0, 0))],
        out_shape=[S((t, D), F32), S((t, D), BF16), S((8, D), F32)],
        compiler_params=_cp(("arbitrary",), 56),
    )(ya, yc, w_out, x, target, gate, ln_g, ln_b)


def _dycat_gates(dy, w_out, attn_o, pw, proj):
    t = dy.shape[0]
    tm = min(256, t)

    def body(dy_ref, w_ref, o_ref, pw_ref, galo, gahi, gclo, gchi, dao_ref, dga_ref, dpw_ref, dgc_ref):
        dyc = lax.dot_general(dy_ref[...], w_ref[...], (((1,), (1,)), ((), ())), preferred_element_type=F32)
        da, dc = dyc[:, 0:DA], dyc[:, DA:2 * DA]
        ga = jnp.concatenate([galo[...], gahi[...]], axis=1).astype(F32)
        sa = _sig(ga)
        dao_ref[...] = (da * (ga * sa)).astype(BF16)
        dga_ref[...] = (da * o_ref[...].astype(F32) * _dsilu(ga, sa)).astype(BF16)
        gc = jnp.concatenate([gclo[...], gchi[...]], axis=1).astype(F32)
        sc = _sig(gc)
        dpw_ref[...] = (dc * (gc * sc)).astype(BF16)
        dgc_ref[...] = (dc * pw_ref[...].astype(F32) * _dsilu(gc, sc)).astype(BF16)

    half = pl.BlockSpec((tm, DA), lambda i: (i, 0))
    cur = lambda cidx: pl.BlockSpec((tm, 512), lambda i: (i, cidx))
    return pl.pallas_call(
        body, name="dycat_gates", grid=(t // tm,),
        in_specs=[pl.BlockSpec((tm, D), lambda i: (i, 0)), pl.BlockSpec((D, D), lambda i: (0, 0)), half, half,
                  cur(3), cur(4), cur(9), cur(10)],
        out_specs=[half] * 4,
        out_shape=[S((t, DA), BF16)] * 4,
        compiler_params=_cp(("parallel",), 48),
    )(dy, w_out, attn_o, pw, proj, proj, proj, proj)


def _conv_bwd_ln(dpw, wpw, uc, ln_g, ln_b):
    t = dpw.shape[0]
    tc = min(256, t)

    def body(dpw_ref, w_ref, uc_ref, lg_ref, lb_ref, duc_ref, sw_ref, st_ref):
        @pl.when(pl.program_id(0) == 0)
        def _():
            st_ref[...] = jnp.zeros_like(st_ref)

        dpw_v = dpw_ref[...]
        ds = lax.dot_general(dpw_v, w_ref[...], (((1,), (1,)), ((), ())), preferred_element_type=F32)
        uc = uc_ref[...]
        mu = jnp.mean(uc, axis=-1, keepdims=True)
        xc = uc - mu
        rstd = lax.rsqrt(jnp.mean(xc * xc, axis=-1, keepdims=True) + LN_EPS)
        uh = xc * rstd
        ln = uh * lg_ref[...] + lb_ref[...]
        sg = _sig(ln)
        sw_ref[...] = (ln * sg).astype(BF16)
        dln = ds * _dsilu(ln, sg)
        dxh = dln * lg_ref[...]
        m1 = jnp.mean(dxh, axis=-1, keepdims=True)
        m2 = jnp.mean(dxh * uh, axis=-1, keepdims=True)
        duc = rstd * (dxh - m1 - uh * m2)
        duc_ref[...] = duc
        st_ref[0:1, :] += jnp.sum(dln * uh, axis=0, keepdims=True)
        st_ref[1:2, :] += jnp.sum(dln, axis=0, keepdims=True)
        st_ref[2:3, :] += jnp.sum(duc, axis=0, keepdims=True)
        st_ref[3:4, :] += jnp.sum(dpw_v.astype(F32), axis=0, keepdims=True)

    row = pl.BlockSpec((1, DA), lambda i: (0, 0))
    full = pl.BlockSpec((tc, DA), lambda i: (i, 0))
    return pl.pallas_call(
        body, name="conv_bwd_ln", grid=(t // tc,),
        in_specs=[full, pl.BlockSpec((DA, DA), lambda i: (0, 0)), full, row, row],
        out_specs=[full, full, pl.BlockSpec((8, DA), lambda i: (0, 0))],
        out_shape=[S((t, DA), F32), S((t, DA), BF16), S((8, DA), F32)],
        compiler_params=_cp(("arbitrary",), 48),
    )(dpw, wpw, uc, ln_g, ln_b)


def _conv_bwd_dw(duc, proj, cwb):
    t = duc.shape[0]
    tc = _conv_rows(t)
    rc = min(128, tc)
    nt = t // tc
    off = HALO - CW + 1

    def body(dcur, dnext, alo, ahi, blo, bhi, alo_h, ahi_h, blo_h, bhi_h, cw_ref, da_ref, db_ref, gw_ref, u_scr, d_scr, g_scr):
        i = pl.program_id(0)

        @pl.when(i == 0)
        def _():
            g_scr[...] = jnp.zeros_like(g_scr)

        _glu_into(u_scr, i, (alo, ahi), (blo, bhi), (alo_h, ahi_h), (blo_h, bhi_h), tc)
        for cc in range(DA // 128):
            d_scr[cc, 0:tc, :] = dcur[:, cc * 128:(cc + 1) * 128]
            d_scr[cc, tc:tc + HALO, :] = jnp.where(i == nt - 1, 0.0, dnext[:, cc * 128:(cc + 1) * 128])

        def rows(r, carry):
            r0 = pl.multiple_of(r * rc, rc)
            for cc in range(DA // 128):
                lanes = pl.ds(cc * 128, 128)
                acc = jnp.zeros((rc // 8, 8, 128), F32)
                for j in range(CW):
                    dv = d_scr[cc, pl.ds(r0 + j, rc), :].reshape(rc // 8, 8, 128)
                    acc = acc + dv * cw_ref[CW - 1 - j, :, lanes]
                du = acc.reshape(rc, 128)
                d0 = d_scr[cc, pl.ds(r0, rc), :].reshape(rc // 8, 8, 128)
                for k in range(CW):
                    u = u_scr[cc, pl.ds(r0 + off + k, rc), :].reshape(rc // 8, 8, 128)
                    g_scr[k, :, lanes] += jnp.sum(d0 * u, axis=0)
                a_ref, b_ref = (alo, blo) if cc < 4 else (ahi, bhi)
                lc = pl.ds((cc % 4) * 128, 128)
                av = a_ref[pl.ds(r0, rc), lc].astype(F32)
                sb = _sig(b_ref[pl.ds(r0, rc), lc].astype(F32))
                da_ref[pl.ds(r0, rc), lanes] = (du * sb).astype(BF16)
                db_ref[pl.ds(r0, rc), lanes] = (du * av * sb * (1.0 - sb)).astype(BF16)
            return carry

        lax.fori_loop(0, tc // rc, rows, 0)

        @pl.when(i == nt - 1)
        def _():
            gw_ref[0:CW, :] = jnp.sum(g_scr[...], axis=1)
            gw_ref[CW:32, :] = jnp.zeros((32 - CW, DA), F32)

    nh = tc // HALO
    nhb = t // HALO
    cur = lambda cidx: pl.BlockSpec((tc, 512), lambda i: (i, cidx))
    halo = lambda cidx: pl.BlockSpec((HALO, 512), lambda i: (jnp.maximum(i * nh - 1, 0), cidx))
    full = pl.BlockSpec((tc, DA), lambda i: (i, 0))
    return pl.pallas_call(
        body, name="conv_bwd_dw", grid=(nt,),
        in_specs=[full, pl.BlockSpec((HALO, DA), lambda i: (jnp.minimum((i + 1) * nh, nhb - 1), 0)),
                  cur(5), cur(6), cur(7), cur(8), halo(5), halo(6), halo(7), halo(8),
                  pl.BlockSpec((CW, 8, DA), lambda i: (0, 0, 0))],
        out_specs=[full, full, pl.BlockSpec((32, DA), lambda i: (0, 0))],
        out_shape=[S((t, DA), BF16), S((t, DA), BF16), S((32, DA), F32)],
        scratch_shapes=[pltpu.VMEM((DA // 128, HALO + tc, 128), F32), pltpu.VMEM((DA // 128, tc + HALO, 128), F32),
                        pltpu.VMEM((CW, 8, DA), F32)],
        compiler_params=_cp(("arbitrary",), 48),
    )(duc, duc, *([proj] * 8), cwb)


def _attn_bwd(proj, dao, lse, rel_bias, sinks, bucket):
    t = proj.shape[0]

    def body(rel_ref, sink_ref, bk_ref, q_ref, kvc_ref, kvp_ref, do_ref, lse_ref,
             dq_ref, dcur_ref, dprev_ref, dsacc_ref, dsk_ref, bias_ref):
        n = pl.program_id(0)

        @pl.when(n == 0)
        def _():
            _build_bias(rel_ref, bk_ref, bias_ref)
            dsacc_ref[...] = jnp.zeros_like(dsacc_ref)
            dsk_ref[...] = jnp.zeros_like(dsk_ref)

        tbl = jnp.where(n == 0, 0, 1)
        kv = jnp.concatenate([kvp_ref[...], kvc_ref[...]], axis=0)
        for h in range(NKV):
            k_h = kv[:, h * HD:(h + 1) * HD]
            v_h = kv[:, NKV * HD + h * HD:NKV * HD + (h + 1) * HD]
            q4 = _group_rows(q_ref, h) * SCALE
            do4 = _group_rows(do_ref, h)
            lse_h = lse_ref[h:h + 1, :]
            s = lax.dot_general(k_h, q4, (((1,), (1,)), ((), ())), preferred_element_type=F32)
            p = jnp.exp(s + bias_ref[tbl, h] - lse_h)
            dp = lax.dot_general(v_h, do4, (((1,), (1,)), ((), ())), preferred_element_type=F32)
            delta = jnp.sum(p * dp, axis=0, keepdims=True)
            ds = p * (dp - delta)
            dsacc_ref[h] += ds
            dsk_ref[h:h + 1, :] += -jnp.exp(_sink_row(sink_ref, h) - lse_h) * delta
            dsb = ds.astype(BF16)
            dq4 = lax.dot_general(dsb, k_h, (((0,), (0,)), ((), ())), preferred_element_type=F32) * SCALE
            for g in range(GRP):
                hd = GRP * h + g
                dq_ref[:, hd * HD:(hd + 1) * HD] = dq4[g * BLK:(g + 1) * BLK].astype(BF16)
            dk = jnp.dot(dsb, q4, preferred_element_type=F32)
            dv = jnp.dot(p.astype(BF16), do4, preferred_element_type=F32)
            kc = slice(h * HD, (h + 1) * HD)
            vc = slice(NKV * HD + h * HD, NKV * HD + (h + 1) * HD)
            dprev_ref[:, kc] = dk[0:BLK]
            dcur_ref[:, kc] = dk[BLK:2 * BLK]
            dprev_ref[:, vc] = dv[0:BLK]
            dcur_ref[:, vc] = dv[BLK:2 * BLK]

        @pl.when(n == t // BLK - 1)
        def _():
            for g in range(GRP):
                lanes = slice(g * BLK, (g + 1) * BLK)
                dsk_ref[:, lanes] = jnp.broadcast_to(jnp.sum(dsk_ref[:, lanes], axis=-1, keepdims=True), (NKV, BLK))

    blk = lambda w, cidx: pl.BlockSpec((BLK, w), lambda n: (n, cidx))
    return pl.pallas_call(
        body, name="attn_bwd", grid=(t // BLK,),
        in_specs=[_SMEM, _SMEM, pl.BlockSpec((2 * BLK, BLK), lambda n: (0, 0)),
                  blk(DA, 0), blk(512, 2), pl.BlockSpec((BLK, 512), lambda n: (jnp.maximum(n - 1, 0), 2)),
                  blk(DA, 0), pl.BlockSpec((None, NKV, GRP * BLK), lambda n: (n, 0, 0))],
        out_specs=[blk(DA, 0), blk(512, 0), blk(512, 0),
                   pl.BlockSpec((NKV, 2 * BLK, GRP * BLK), lambda n: (0, 0, 0)), pl.BlockSpec((NKV, GRP * BLK), lambda n: (0, 0))],
        out_shape=[S((t, DA), BF16), S((t, 512), F32), S((t, 512), F32), S((NKV, 2 * BLK, GRP * BLK), F32),
                   S((NKV, GRP * BLK), F32)],
        scratch_shapes=[pltpu.VMEM((2, NKV, 2 * BLK, GRP * BLK), F32)],
        compiler_params=_cp(("arbitrary",), 40),
    )(rel_bias, sinks, bucket, proj, proj, proj, dao, lse)


def _dkv_fix(dcur, dprev):
    t = dcur.shape[0]
    tb = min(1024, t)
    nt = t // tb
    per = tb // BLK

    def body(c_ref, p_ref, pn_ref, o_ref):
        if tb > BLK:
            o_ref[0:tb - BLK, :] = (c_ref[0:tb - BLK, :] + p_ref[BLK:tb, :]).astype(BF16)
        nxt = jnp.where(pl.program_id(0) == nt - 1, 0.0, pn_ref[...])
        o_ref[tb - BLK:tb, :] = (c_ref[tb - BLK:tb, :] + nxt).astype(BF16)

    tile = pl.BlockSpec((tb, 512), lambda i: (i, 0))
    return pl.pallas_call(
        body, name="dkv_fix", grid=(nt,),
        in_specs=[tile, tile, pl.BlockSpec((BLK, 512), lambda i: (jnp.minimum((i + 1) * per, t // BLK - 1), 0))],
        out_specs=tile,
        out_shape=S((t, 512), BF16),
        compiler_params=_cp(("parallel",), 32),
    )(dcur, dprev, dprev)


def _bias_grad(dsacc, bucket):
    def body(ds_ref, bk_ref, o_ref, row_scr):
        bk = bk_ref[...]

        def group(h, carry):
            for g in range(GRP):
                dsv = ds_ref[h, :, g * BLK:(g + 1) * BLK]
                for b in range(N_BUCKETS):
                    row_scr[GRP * h + g, b:b + 1, :] = jnp.sum(jnp.where(bk == b, dsv, 0.0), axis=0, keepdims=True)
            return carry

        lax.fori_loop(0, NKV, group, 0)
        for hd in range(NQ):
            o_ref[hd] = jnp.sum(row_scr[hd], axis=-1, keepdims=True)

    return pl.pallas_call(
        body, name="bias_grad", out_shape=S((NQ, N_BUCKETS, 1), F32),
        in_specs=[_VMEM, _VMEM], out_specs=_VMEM,
        scratch_shapes=[pltpu.VMEM((NQ, N_BUCKETS, BLK), F32)],
        compiler_params=_cp(None, 32),
    )(dsacc, bucket)


def _dh_gradx(dproj, wg, gx0, x, scale1p, p_in, p_out, p_pw):
    t = x.shape[0]
    tm = min(256, t)
    ni = t // tm
    bn = D // 2
    nn = D // bn

    def body(dp_ref, w_ref, gx_ref, x_ref, sc_ref, pi, po, pp, out_ref, st_ref, li, lo, lp, send_sems, recv_sems):
        n, i = pl.program_id(0), pl.program_id(1)

        @pl.when((n == 0) & (i == 0))
        def _():
            for cp in _chip_exchange_copies(pi, po, pp, li, lo, lp, send_sems, recv_sems):
                cp.start()

        @pl.when(i == 0)
        def _():
            st_ref[...] = jnp.zeros_like(st_ref)

        dh = None
        for j in range(NCHIP):
            part = lax.dot_general(dp_ref[:, j * NB:(j + 1) * NB], w_ref[j], (((1,), (1,)), ((), ())),
                                   preferred_element_type=F32)
            dh = part if dh is None else dh + part
        out_ref[...] = gx_ref[...] + dh * sc_ref[...]
        st_ref[0:1, :] += jnp.sum(dh, axis=0, keepdims=True)
        st_ref[1:2, :] += jnp.sum(dh * x_ref[...], axis=0, keepdims=True)

        @pl.when((n == nn - 1) & (i == ni - 1))
        def _():
            cps = _chip_exchange_copies(pi, po, pp, li, lo, lp, send_sems, recv_sems)
            for cp in cps:
                cp.wait_recv()
            for cp in cps:
                cp.wait_send()

    tile = pl.BlockSpec((tm, bn), lambda n, i: (i, n))
    return pl.pallas_call(
        body, name="dh_gradx", grid=(nn, ni),
        in_specs=[pl.BlockSpec((tm, D_IN), lambda n, i: (i, 0)), pl.BlockSpec((NCHIP, bn, NB), lambda n, i: (0, n, 0)),
                  tile, tile, pl.BlockSpec((1, bn), lambda n, i: (0, n)), _ANY, _ANY, _ANY],
        out_specs=[tile, pl.BlockSpec((8, bn), lambda n, i: (0, n)), _ANY, _ANY, _ANY],
        out_shape=[S((t, D), F32), S((8, D), F32), S((3, p_in.shape[0], NB), p_in.dtype),
                   S((3,) + p_out.shape[1:], p_out.dtype), S((3,) + p_pw.shape[1:], p_pw.dtype)],
        scratch_shapes=[pltpu.SemaphoreType.DMA((9,)), pltpu.SemaphoreType.DMA((9,))],
        compiler_params=_cp(("arbitrary", "arbitrary"), 56),
    )(dproj, wg, gx0, x, scale1p, p_in, p_out, p_pw)


def _atb(a_parts, b, bn, name):
    bm = DA
    t = b.shape[0]
    n = b.shape[1]
    tk = min(2048, t)
    nk = t // tk
    na = len(a_parts)

    def body(*refs):
        a_refs, b_ref, o_ref = refs[:na], refs[na], refs[na + 1]
        mi, k = pl.program_id(0), pl.program_id(2)
        for q in range(na):
            @pl.when(mi == q)
            def _(q=q):
                part = lax.dot_general(a_refs[q][...], b_ref[...], (((0,), (0,)), ((), ())), preferred_element_type=F32)

                @pl.when(k == 0)
                def _():
                    o_ref[...] = part

                @pl.when(k > 0)
                def _():
                    o_ref[...] += part

    a_spec = lambda q, cidx: pl.BlockSpec((tk, bm), lambda mi, j, k: (jnp.where(mi == q, k, 0), cidx))
    return pl.pallas_call(
        body, name=name, grid=(na, n // bn, nk),
        in_specs=[a_spec(q, cidx) for q, (_, cidx) in enumerate(a_parts)] + [pl.BlockSpec((tk, bn), lambda mi, j, k: (k, j))],
        out_specs=pl.BlockSpec((bm, bn), lambda mi, j, k: (mi, j)),
        out_shape=S((na * bm, n), F32),
        compiler_params=_cp(("parallel", "parallel", "arbitrary"), 56),
    )(*[arr for arr, _ in a_parts], b)


def _sum8(parts):
    _, r, n = parts.shape

    def body(p_ref, o_ref):
        v = p_ref[0]
        for d in range(1, 8):
            v = v + p_ref[d]
        o_ref[...] = v

    return pl.pallas_call(body, name="sum8", out_shape=S((r, n), F32), in_specs=[_VMEM], out_specs=_VMEM,
                          compiler_params=_cp(None, 32))(parts)


def _adam_math(w, g, m, v):
    m = B1 * m + (1.0 - B1) * g
    v = B2 * v + (1.0 - B2) * (g * g)
    m_hat = m / (1.0 - B1 ** STEP)
    v_hat = v / (1.0 - B2 ** STEP)
    delta = -LR * (m_hat / (jnp.sqrt(v_hat) + EPS) + WD * w)
    return delta, m, v


def _adamw(w, g, m, v, name):
    r, n = w.shape
    tr = min(256, r)

    def body(w_ref, g_ref, m_ref, v_ref, d_ref, nm_ref, nv_ref):
        d_ref[...], nm_ref[...], nv_ref[...] = _adam_math(w_ref[...], g_ref[...], m_ref[...], v_ref[...])

    spec = pl.BlockSpec((tr, n), lambda i: (i, 0))
    return pl.pallas_call(
        body, name=name, grid=(r // tr,), in_specs=[spec] * 4, out_specs=[spec] * 3,
        out_shape=[S((r, n), F32)] * 3, compiler_params=_cp(("parallel",), 48),
    )(w, g, m, v)


def _adamw_ada(ct, dmod_p, w, m, v):
    r, n = w.shape
    tr = min(256, r)

    def body(ct_ref, dm_ref, w_ref, m_ref, v_ref, g_ref, d_ref, nm_ref, nv_ref):
        cv = ct_ref[...]
        g = jnp.dot(cv * _sig(cv), dm_ref[...], preferred_element_type=F32)
        g_ref[...] = g
        d_ref[...], nm_ref[...], nv_ref[...] = _adam_math(w_ref[...], g, m_ref[...], v_ref[...])

    spec = pl.BlockSpec((tr, n), lambda i: (i, 0))
    return pl.pallas_call(
        body, name="adamw_ada", grid=(r // tr,),
        in_specs=[pl.BlockSpec((tr, 8), lambda i: (i, 0)), pl.BlockSpec((8, n), lambda i: (0, 0)), spec, spec, spec],
        out_specs=[spec] * 4, out_shape=[S((r, n), F32)] * 4, compiler_params=_cp(("parallel",), 48),
    )(ct, dmod_p, w, m, v)


def _adamw_small(ws, gs, ms, vs):
    k = len(ws)

    def body(*refs):
        ins, outs = refs[:4 * k], refs[4 * k:]
        for i in range(k):
            d, nm, nv = _adam_math(ins[i][...], ins[k + i][...], ins[2 * k + i][...], ins[3 * k + i][...])
            outs[i][...] = d
            outs[k + i][...] = nm
            outs[2 * k + i][...] = nv

    shapes = [S(w.shape, F32) for w in ws]
    return pl.pallas_call(body, name="adamw_small", out_shape=shapes * 3, in_specs=[_VMEM] * (4 * k),
                          out_specs=[_VMEM] * (3 * k), compiler_params=_cp(None, 32))(*ws, *gs, *ms, *vs)


def _bucket_map():
    qi = jnp.arange(BLK, dtype=jnp.int32)[None, :]
    kj = jnp.arange(2 * BLK, dtype=jnp.int32)[:, None]
    dist = qi + BLK - kj
    dd = jnp.maximum(dist, 0)
    max_exact = N_BUCKETS // 2
    dfl = jnp.maximum(dd, 1).astype(F32)
    large = max_exact + (jnp.log(dfl / max_exact) / math.log(MAX_DIST / max_exact) * (N_BUCKETS - max_exact)).astype(jnp.int32)
    large = jnp.minimum(large, N_BUCKETS - 1)
    bucket = jnp.where(dd < max_exact, dd, large)
    return jnp.where((dist >= 0) & (dist < BLK), bucket, -1).astype(jnp.int32)


def _pad_rows(a, rows):
    return jnp.pad(a, ((0, rows - a.shape[0]), (0, 0)))


def kernel(x, c, w_ada, b_ada, w_in, rel_bias, sinks, conv_w, conv_b, conv_ln_g, conv_ln_b, w_pw, b_pw, w_out, ln_g, ln_b, loss_target, m_w_ada, m_b_ada, m_w_in, m_rel_bias, m_sinks, m_conv_w, m_conv_b, m_conv_ln_g, m_conv_ln_b, m_w_pw, m_b_pw, m_w_out, m_ln_g, m_ln_b, v_w_ada, v_b_ada, v_w_in, v_rel_bias, v_sinks, v_conv_w, v_conv_b, v_conv_ln_g, v_conv_ln_b, v_w_pw, v_b_pw, v_w_out, v_ln_g, v_ln_b):
    mx, my, mc = _me()
    chip = 2 * mx + my
    dev = 2 * chip + mc
    t = x.shape[1]
    x2 = x.reshape(t, D)
    tgt = loss_target.reshape(t, D)
    n_ada = w_ada.shape[2]
    cw_cols = conv_w.shape[2]

    pack0 = jnp.concatenate([_pad_rows(c, 8), _pad_rows(_pad_rows(conv_w[0], 32).reshape(-1, D), 8)], axis=0)
    g0 = _allgather8(pack0, "gather_c_convw").reshape(8, 16, D)
    c_all = g0[:, 0, :]
    cw_rows = 32 * cw_cols // D
    cw_full = jnp.concatenate([g0[2 * q, 8:8 + cw_rows, :].reshape(32, cw_cols) for q in range(NCHIP)], axis=1)[:CW]
    cwb = jnp.broadcast_to(cw_full[:, None, :], (CW, 8, DA))

    b_ada_p = lax.dynamic_slice(b_ada, (0, chip * n_ada), (1, n_ada))
    mod_part = _ada_fwd(c_all, w_ada[0], b_ada_p)
    mod_all = _allgather8(mod_part, "gather_mod").reshape(8, 8, n_ada)
    mod = jnp.concatenate([lax.dynamic_slice(mod_all[2 * q], (dev, 0), (1, n_ada)) for q in range(NCHIP)], axis=1)
    shift, scale, gate = mod[:, 0:D], mod[:, D:2 * D], mod[:, 2 * D:3 * D]
    scale1p = 1.0 + scale

    bucket = _bucket_map()
    sel_chip = jnp.reshape(chip, (1,)).astype(jnp.int32)
    sel = jnp.reshape(mc, (1,)).astype(jnp.int32)
    proj, h, wg_in, wg_out, wg_pw = _inproj_gather(
        x2, scale1p, shift, w_in[0].astype(BF16), w_out[0].astype(BF16), w_pw[0].astype(BF16), sel_chip)
    w_out_f = wg_out.reshape(D, D)
    w_pw_f = wg_pw.reshape(DA, DA)
    ya, attn_o, lse = _attn_fwd(proj, rel_bias, sinks, bucket)
    uc, pw, yc = _conv_fwd(proj, cwb, conv_b, conv_ln_g, conv_ln_b, w_pw_f, b_pw)
    gx0, dy, st_out = _outproj_loss(ya, yc, w_out_f, x2, tgt, gate, ln_g, ln_b)

    dao, dga, dpw, dgc = _dycat_gates(dy, w_out_f, attn_o, pw, proj)
    duc, sw, st_conv = _conv_bwd_ln(dpw, w_pw_f, uc, conv_ln_g, conv_ln_b)
    dglu_a, dglu_b, gw_conv = _conv_bwd_dw(duc, proj, cwb)
    dq, dkv_cur, dkv_prev, dsacc, dsk = _attn_bwd(proj, dao, lse, rel_bias, sinks, bucket)
    dkv = _dkv_fix(dkv_cur, dkv_prev)
    g_rel = _bias_grad(dsacc, bucket).reshape(NQ, N_BUCKETS).T
    dproj = jnp.concatenate([dq, dkv, dga, dglu_a, dglu_b, dgc], axis=1)
    gp_in = _atb([(h, 0), (h, 1)], dproj, NB, "grad_w_in")
    gp_out = _atb([(ya, 0), (yc, 0)], dy, 1024, "grad_w_out")
    gp_pw = _atb([(sw, 0)], dpw, 1024, "grad_w_pw")

    r_out, r_pw = D // NCHIP // 2, DA // NCHIP // 2
    gi3, go3, gp3 = gp_in.reshape(2, D // 2, D_IN), gp_out.reshape(8, r_out, D), gp_pw.reshape(8, r_pw, DA)
    l_in, l_out, l_pw = _pair_exchange(gi3, go3, gp3)
    pi32, pi16 = _pair_sum(gi3, l_in.reshape(1, D // 2, D_IN), sel, "pair_sum_in")
    po32, po16 = _pair_sum(go3, l_out, sel, "pair_sum_out")
    pp32, pp16 = _pair_sum(gp3, l_pw, sel, "pair_sum_pw")
    grad_x, st_in, rc_in, rc_out, rc_pw = _dh_gradx(dproj, wg_in, gx0, x2, scale1p, pi16[0], po16, pp16)
    own_in = lax.dynamic_slice(pi32[0], (0, chip * NB), (D // 2, NB))
    own_out = lax.dynamic_index_in_dim(po32, chip, 0, keepdims=False)
    own_pw = lax.dynamic_index_in_dim(pp32, chip, 0, keepdims=False)
    h_in = _chip_sum(own_in, rc_in, sel, "chip_sum_in")
    h_out = _chip_sum(own_out, rc_out, sel, "chip_sum_out")
    h_pw = _chip_sum(own_pw, rc_pw, sel, "chip_sum_pw")
    f_in, f_out, f_pw = _pair_share(h_in, h_out, h_pw)
    g_w_in = f_in.reshape(D, NB)
    g_w_out = f_out.reshape(D // NCHIP, D)
    g_w_pw = f_pw.reshape(DA // NCHIP, DA)

    dmod = jnp.concatenate([st_in[0:1], st_in[1:2], st_out[2:3]], axis=1)
    loss_row = st_out[3:4, 0:1]
    small = jnp.concatenate([
        dmod, st_out[0:1], st_out[1:2],
        st_conv[0:1], st_conv[1:2], st_conv[2:3], st_conv[3:4],
        g_rel.reshape(1, N_BUCKETS * NQ), dsk.reshape(NKV, GRP, BLK)[:, :, 0].reshape(1, NQ), loss_row,
        gw_conv[:CW].reshape(1, CW * DA)], axis=1)
    n_small = small.shape[1]
    rows_small = -(-n_small // (8 * D)) * 8
    small = jnp.pad(small, ((0, 0), (0, rows_small * D - n_small))).reshape(rows_small, D)
    parts = _allgather8(small, "gather_small").reshape(8, rows_small, D)
    tot = _sum8(parts).reshape(1, rows_small * D)
    dmod_all = parts.reshape(8, rows_small * D)[:, 0:3 * D]

    o = 3 * D
    def take(nn):
        nonlocal o
        v = tot[:, o:o + nn]
        o += nn
        return v
    g_b_ada = tot[:, 0:3 * D]
    g_ln_g, g_ln_b = take(D), take(D)
    g_cln_g, g_cln_b, g_conv_b, g_b_pw = take(DA), take(DA), take(DA), take(DA)
    g_rel_bias = take(N_BUCKETS * NQ).reshape(N_BUCKETS, NQ)
    g_sinks = take(NQ)
    loss = take(1).reshape(())
    g_conv_w_full = take(CW * DA).reshape(CW, DA)
    g_conv_w = lax.dynamic_slice(g_conv_w_full, (0, chip * cw_cols), (CW, cw_cols))

    dmod_p = lax.dynamic_slice(dmod_all, (0, chip * n_ada), (8, n_ada))
    g_w_ada, d_w_ada, nm_w_ada, nv_w_ada = _adamw_ada(c_all.T, dmod_p, w_ada[0], m_w_ada[0], v_w_ada[0])
    d_w_in, nm_w_in, nv_w_in = _adamw(w_in[0], g_w_in, m_w_in[0], v_w_in[0], "adamw_in")
    d_w_out, nm_w_out, nv_w_out = _adamw(w_out[0], g_w_out, m_w_out[0], v_w_out[0], "adamw_out")
    d_w_pw, nm_w_pw, nv_w_pw = _adamw(w_pw[0], g_w_pw, m_w_pw[0], v_w_pw[0], "adamw_pw")
    small_w = [b_ada, rel_bias, sinks, conv_w[0], conv_b, conv_ln_g, conv_ln_b, b_pw, ln_g, ln_b]
    small_g = [g_b_ada, g_rel_bias, g_sinks, g_conv_w, g_conv_b, g_cln_g, g_cln_b, g_b_pw, g_ln_g, g_ln_b]
    small_m = [m_b_ada, m_rel_bias, m_sinks, m_conv_w[0], m_conv_b, m_conv_ln_g, m_conv_ln_b, m_b_pw, m_ln_g, m_ln_b]
    small_v = [v_b_ada, v_rel_bias, v_sinks, v_conv_w[0], v_conv_b, v_conv_ln_g, v_conv_ln_b, v_b_pw, v_ln_g, v_ln_b]
    res = _adamw_small(small_w, small_g, small_m, small_v)
    ns = len(small_w)
    d_s, nm_s, nv_s = res[:ns], res[ns:2 * ns], res[2 * ns:]

    def ordered(w_ada_, w_in_, w_pw_, w_out_, sm):
        b_ada_, rel_, sinks_, conv_w_, conv_b_, cln_g_, cln_b_, b_pw_, ln_g_, ln_b_ = sm
        return (w_ada_[None], b_ada_, w_in_[None], rel_, sinks_, conv_w_[None], conv_b_, cln_g_, cln_b_,
                w_pw_[None], b_pw_, w_out_[None], ln_g_, ln_b_)

    grads = ordered(g_w_ada, g_w_in, g_w_pw, g_w_out, small_g)
    deltas = ordered(d_w_ada, d_w_in, d_w_pw, d_w_out, d_s)
    new_m = ordered(nm_w_ada, nm_w_in, nm_w_pw, nm_w_out, nm_s)
    new_v = ordered(nv_w_ada, nv_w_in, nv_w_pw, nv_w_out, nv_s)
    return (loss, grad_x.reshape(1, t, D), *grads, *deltas, *new_m, *new_v)
```

```python
import functools
import math

import jax
import jax.numpy as jnp
from jax import lax
from jax.experimental import pallas as pl
from jax.experimental.pallas import tpu as pltpu

F32, BF16 = jnp.float32, jnp.bfloat16
S = jax.ShapeDtypeStruct
MESH = pl.DeviceIdType.MESH

D = 2048
DA = 1024
HD = 64
NQ, NKV, GRP = 16, 4, 4
BLK = 128
CW = 31
HALO = 32
D_IN = 5632
NCHIP = 4
NB = D_IN // NCHIP
N_BUCKETS, MAX_DIST = 32, 128
LN_EPS = 1e-5
ALPHA = 2.0 ** 0.25
SCALE = HD ** -0.5
NEG = -1e30
LR, B1, B2, EPS, WD, STEP = 0.001, 0.9, 0.999, 1e-08, 0.01, 10

_VMEM = pl.BlockSpec(memory_space=pltpu.VMEM)
_SMEM = pl.BlockSpec(memory_space=pltpu.SMEM)
_ANY = pl.BlockSpec(memory_space=pl.ANY)


def _cp(sem=None, vmem_mb=None):
    kw = {}
    if sem is not None:
        kw["dimension_semantics"] = sem
    if vmem_mb is not None:
        kw["vmem_limit_bytes"] = vmem_mb * 1024 * 1024
    return pltpu.CompilerParams(**kw)


def _sig(v):
    return jax.nn.sigmoid(v)


def _dsilu(g, sg):
    return sg * (1.0 + g * (1.0 - sg))


def _me():
    return lax.axis_index("x"), lax.axis_index("y"), lax.axis_index("c")


def _allgather8(x_shard, name):
    m_per, n = x_shard.shape

    def body(x_ref, out_ref, send_sems, recv_sems, local_sem):
        x, y, c = _me()
        me, sibling = (x, y, c), (x, y, 1 - c)
        chips = [(1 - x, y), (x, 1 - y), (1 - x, 1 - y)]

        def rows(px, py, pc):
            return out_ref.at[pl.ds((4 * px + 2 * py + pc) * m_per, m_per), :]

        def copy(k, block, to, src=None):
            return pltpu.make_async_remote_copy(
                src_ref=rows(*block) if src is None else src, dst_ref=rows(*block),
                send_sem=send_sems.at[k], recv_sem=recv_sems.at[k], device_id=to, device_id_type=MESH)

        mine = pltpu.make_async_copy(x_ref, rows(*me), local_sem)
        mine.start()
        first = [copy(0, me, sibling, src=x_ref)]
        first += [copy(1 + j, me, (*chip, c), src=x_ref) for j, chip in enumerate(chips)]
        for cp in first:
            cp.start()
        passed = [copy(4 + j, (*chip, c), sibling) for j, chip in enumerate(chips)]
        for j, chip in enumerate(chips):
            copy(1 + j, (*chip, c), me).wait_recv()
            passed[j].start()
        copy(0, sibling, me).wait_recv()
        for j, chip in enumerate(chips):
            copy(4 + j, (*chip, 1 - c), me).wait_recv()
        for cp in first + passed:
            cp.wait_send()
        mine.wait()

    return pl.pallas_call(
        body, name=name, out_shape=S((8 * m_per, n), x_shard.dtype),
        in_specs=[_VMEM], out_specs=_VMEM,
        scratch_shapes=[pltpu.SemaphoreType.DMA((7,)), pltpu.SemaphoreType.DMA((7,)), pltpu.SemaphoreType.DMA],
    )(x_shard)


def _inproj_gather(x, scale1p, shift, wi, wo, wp, chip_id):
    t = x.shape[0]
    tm = min(512, t)
    ni = t // tm
    shards = (wi, wo, wp)
    halves = [s.shape[0] // 2 for s in shards]

    def body(cid_ref, x_ref, sc_ref, sh_ref, wi_ref, wo_ref, wp_ref, proj_ref, h_ref, gi_ref, go_ref, gp_ref,
             wbuf, obuf, pbuf, send_sems, recv_sems, ld_sems, st_sems):
        jj, i = pl.program_id(0), pl.program_id(1)
        mx, my, c = _me()
        p = 2 * mx + my
        sibling = (mx, my, 1 - c)
        srcs, dsts, bufs = (wi_ref, wo_ref, wp_ref), (gi_ref, go_ref, gp_ref), (wbuf, obuf, pbuf)
        chips = {1: (mx, 1 - my), 2: (1 - mx, my), 3: (1 - mx, 1 - my)}

        def half(ref, w, hc):
            return ref.at[pl.ds(hc * halves[w], halves[w])]

        def copy(k, src, dst, to):
            return pltpu.make_async_remote_copy(src_ref=src, dst_ref=dst, send_sem=send_sems.at[k],
                                                recv_sem=recv_sems.at[k], device_id=to, device_id_type=MESH)

        def sent(w, m):
            return copy(3 * w + m - 1, half(srcs[w], w, c), half(dsts[w].at[p], w, c), (*chips[m], c))

        def landed(w, m):
            blk = half(dsts[w].at[jnp.bitwise_xor(p, m)], w, c)
            return copy(3 * w + m - 1, blk, blk, (*chips[m], c))

        def passed(w, m, hc):
            blk = half(dsts[w].at[jnp.bitwise_xor(p, m)], w, hc)
            return copy(9 + 3 * w + m - 1, blk, blk, sibling)

        def vm(w, slot):
            return bufs[w].at[slot] if w == 0 else bufs[w]

        def load(w, src, slot=0):
            return pltpu.make_async_copy(src, vm(w, slot), ld_sems.at[w])

        def store(w):
            return pltpu.make_async_copy(vm(w, 0), dsts[w].at[p], st_sems.at[w])

        def block_load(m):
            return load(0, gi_ref.at[jnp.bitwise_xor(p, m)], m % 2)

        @pl.when((jj == 0) & (i == 0))
        def _():
            sent(0, 1).start()
            sent(0, 2).start()
            for w in range(3):
                load(w, srcs[w]).start()
            for w in range(3):
                load(w, srcs[w]).wait()
                store(w).start()

        for m in (1, 2, 3):
            @pl.when((jj == m) & (i == 0))
            def _(m=m):
                if m == 1:
                    sent(0, 3).start()
                    store(0).wait()
                if m == 2:
                    for w in (1, 2):
                        for mm in (1, 2, 3):
                            sent(w, mm).start()
                block_load(m).wait()

        hb = (x_ref[...] * sc_ref[...] + sh_ref[...]).astype(BF16)

        @pl.when(jj == 0)
        def _():
            h_ref[...] = hb

        proj_ref[...] = jnp.dot(hb, wbuf[jj % 2], preferred_element_type=F32).astype(BF16)

        for m in (1, 2, 3):
            @pl.when((jj == m - 1) & (i == ni - 1))
            def _(m=m):
                landed(0, m).wait_recv()
                passed(0, m, c).start()
                passed(0, m, 1 - c).wait_recv()
                block_load(m).start()

        @pl.when((jj == NCHIP - 1) & (i == ni - 1))
        def _():
            for w in (1, 2):
                for m in (1, 2, 3):
                    landed(w, m).wait_recv()
                    passed(w, m, c).start()
            for w in (1, 2):
                for m in (1, 2, 3):
                    passed(w, m, 1 - c).wait_recv()
            for w in range(3):
                for m in (1, 2, 3):
                    sent(w, m).wait_send()
                    passed(w, m, c).wait_send()
            store(1).wait()
            store(2).wait()

    row = pl.BlockSpec((1, D), lambda jj, i, s: (0, 0))
    return pl.pallas_call(
        body, name="inproj_gather",
        grid_spec=pltpu.PrefetchScalarGridSpec(
            num_scalar_prefetch=1, grid=(NCHIP, ni),
            in_specs=[pl.BlockSpec((tm, D), lambda jj, i, s: (i, 0)), row, row, _ANY, _ANY, _ANY],
            out_specs=[pl.BlockSpec((tm, NB), lambda jj, i, s: (i, jnp.bitwise_xor(s[0], jj))),
                       pl.BlockSpec((tm, D), lambda jj, i, s: (jnp.where(jj == 0, i, ni - 1), 0)),
                       _ANY, _ANY, _ANY],
            scratch_shapes=[pltpu.VMEM((2,) + wi.shape, BF16), pltpu.VMEM(wo.shape, BF16), pltpu.VMEM(wp.shape, BF16),
                            pltpu.SemaphoreType.DMA((18,)), pltpu.SemaphoreType.DMA((18,)),
                            pltpu.SemaphoreType.DMA((3,)), pltpu.SemaphoreType.DMA((3,))]),
        out_shape=[S((t, D_IN), BF16), S((t, D), BF16)] + [S((NCHIP,) + s.shape, s.dtype) for s in shards],
        compiler_params=_cp(("arbitrary", "arbitrary"), 48),
    )(chip_id, x, scale1p, shift, wi, wo, wp)


def _pair_exchange(g_in, g_out, g_pw):
    def body(gi, go, gp, li, lo, lp, send_sems, recv_sems):
        x, y, c = _me()
        sibling = (x, y, 1 - c)
        copies = [(gi.at[1 - c], li)]
        for j in range(NCHIP):
            copies.append((go.at[2 * j + 1 - c], lo.at[j]))
        for j in range(NCHIP):
            copies.append((gp.at[2 * j + 1 - c], lp.at[j]))
        cps = [pltpu.make_async_remote_copy(src_ref=s_, dst_ref=d_, send_sem=send_sems.at[k], recv_sem=recv_sems.at[k],
                                            device_id=sibling, device_id_type=MESH) for k, (s_, d_) in enumerate(copies)]
        for cp in cps:
            cp.start()
        for cp in cps:
            cp.wait_recv()
        for cp in cps:
            cp.wait_send()

    n = 1 + 2 * NCHIP
    return pl.pallas_call(
        body, name="grad_pair_exchange",
        out_shape=[S(g_in.shape[1:], g_in.dtype), S((NCHIP,) + g_out.shape[1:], g_out.dtype),
                   S((NCHIP,) + g_pw.shape[1:], g_pw.dtype)],
        in_specs=[_ANY] * 3, out_specs=[_ANY] * 3,
        scratch_shapes=[pltpu.SemaphoreType.DMA((n,)), pltpu.SemaphoreType.DMA((n,))],
    )(g_in, g_out, g_pw)


def _chip_exchange_copies(pi, po, pp, li, lo, lp, send_sems, recv_sems):
    x, y, c = _me()
    chips = [(1 - x, y), (x, 1 - y), (1 - x, 1 - y)]
    cps = []
    for j, chip in enumerate(chips):
        q = 2 * chip[0] + chip[1]
        pairs = ((pi.at[:, pl.ds(q * NB, NB)], li.at[j]), (po.at[q], lo.at[j]), (pp.at[q], lp.at[j]))
        for w, (src, dst) in enumerate(pairs):
            k = 3 * j + w
            cps.append(pltpu.make_async_remote_copy(src_ref=src, dst_ref=dst, send_sem=send_sems.at[k],
                                                    recv_sem=recv_sems.at[k], device_id=(*chip, c), device_id_type=MESH))
    return cps


def _pair_share(f_in, f_out, f_pw):
    bufs = (f_in, f_out, f_pw)

    def body(ai, ao, ap, fi, fo, fp, send_sems, recv_sems):
        x, y, c = _me()
        sibling = (x, y, 1 - c)

        def copy(w, ref, hc):
            return pltpu.make_async_remote_copy(src_ref=ref.at[hc], dst_ref=ref.at[hc], send_sem=send_sems.at[w],
                                                recv_sem=recv_sems.at[w], device_id=sibling, device_id_type=MESH)

        cps = [copy(w, ref, c) for w, ref in enumerate((fi, fo, fp))]
        for cp in cps:
            cp.start()
        for w, ref in enumerate((fi, fo, fp)):
            copy(w, ref, 1 - c).wait_recv()
        for cp in cps:
            cp.wait_send()

    return pl.pallas_call(
        body, name="grad_pair_share",
        out_shape=[S(b.shape, b.dtype) for b in bufs],
        in_specs=[_ANY] * 3, out_specs=[_ANY] * 3, input_output_aliases={0: 0, 1: 1, 2: 2},
        scratch_shapes=[pltpu.SemaphoreType.DMA((3,)), pltpu.SemaphoreType.DMA((3,))],
    )(*bufs)


def _pair_sum(g, l, sel, name):
    n, r, ccols = l.shape
    tr = min(256 if ccols <= D else 128, r)

    def body(sel_ref, g_ref, l_ref, o32_ref, o16_ref):
        v = g_ref[...] + l_ref[...]
        o32_ref[...] = v
        o16_ref[...] = v.astype(BF16)

    spec_l = pl.BlockSpec((None, tr, ccols), lambda j, i, s: (j, i, 0))
    return pl.pallas_call(
        body, name=name,
        grid_spec=pltpu.PrefetchScalarGridSpec(
            num_scalar_prefetch=1, grid=(n, r // tr),
            in_specs=[pl.BlockSpec((None, tr, ccols), lambda j, i, s: (2 * j + s[0], i, 0)), spec_l],
            out_specs=[spec_l, spec_l]),
        out_shape=[S(l.shape, F32), S(l.shape, BF16)],
        compiler_params=_cp(("parallel", "parallel"), 48),
    )(sel, g, l)


def _chip_sum(own, recv, sel, name):
    r, ccols = own.shape
    tr = min(256, r)

    def body(sel_ref, o_ref, r_ref, out_ref):
        v = o_ref[...]
        for j in range(3):
            v = v + r_ref[j].astype(F32)
        out_ref[...] = v

    return pl.pallas_call(
        body, name=name,
        grid_spec=pltpu.PrefetchScalarGridSpec(
            num_scalar_prefetch=1, grid=(r // tr,),
            in_specs=[pl.BlockSpec((tr, ccols), lambda i, s: (i, 0)), pl.BlockSpec((3, tr, ccols), lambda i, s: (0, i, 0))],
            out_specs=pl.BlockSpec((None, tr, ccols), lambda i, s: (s[0], i, 0))),
        out_shape=S((2, r, ccols), F32),
        compiler_params=_cp(("parallel",), 48),
    )(sel, own, recv)


def _ada_fwd(c_all, w_ada, b_ada_p):
    n = w_ada.shape[1]
    tn = 512

    def body(c_ref, w_ref, b_ref, o_ref):
        cv = c_ref[...]
        ca = cv * _sig(cv)
        o_ref[...] = jnp.dot(ca, w_ref[...], preferred_element_type=F32) + b_ref[...]

    return pl.pallas_call(
        body, name="ada_fwd", grid=(n // tn,),
        in_specs=[pl.BlockSpec((8, D), lambda j: (0, 0)), pl.BlockSpec((D, tn), lambda j: (0, j)),
                  pl.BlockSpec((1, tn), lambda j: (0, j))],
        out_specs=pl.BlockSpec((8, tn), lambda j: (0, j)),
        out_shape=S((8, n), F32),
        compiler_params=_cp(("parallel",), 32),
    )(c_all, w_ada, b_ada_p)


def _build_bias(rel_ref, bk_ref, bias_ref):
    bk = bk_ref[...]
    kj = lax.broadcasted_iota(jnp.int32, (2 * BLK, BLK), 0)
    for hd in range(NQ):
        acc = jnp.full((2 * BLK, BLK), NEG, F32)
        for b in range(N_BUCKETS):
            acc = jnp.where(bk == b, rel_ref[b, hd], acc)
        lanes = slice((hd % GRP) * BLK, (hd % GRP + 1) * BLK)
        bias_ref[1, hd // GRP, :, lanes] = acc
        bias_ref[0, hd // GRP, :, lanes] = jnp.where(kj < BLK, NEG, acc)


def _group_rows(ref, h):
    return jnp.concatenate([ref[:, (GRP * h + g) * HD:(GRP * h + g + 1) * HD] for g in range(GRP)], axis=0)


def _sink_row(sink_ref, h):
    return jnp.concatenate([jnp.full((1, BLK), sink_ref[0, GRP * h + g], F32) for g in range(GRP)], axis=1)


def _attn_fwd(proj, rel_bias, sinks, bucket):
    t = proj.shape[0]

    def body(rel_ref, sink_ref, bk_ref, q_ref, kvc_ref, kvp_ref, glo_ref, ghi_ref, ya_ref, o_ref, lse_ref, bias_ref):
        n = pl.program_id(0)

        @pl.when(n == 0)
        def _():
            _build_bias(rel_ref, bk_ref, bias_ref)

        tbl = jnp.where(n == 0, 0, 1)
        kv = jnp.concatenate([kvp_ref[...], kvc_ref[...]], axis=0)
        for h in range(NKV):
            k_h = kv[:, h * HD:(h + 1) * HD]
            v_h = kv[:, NKV * HD + h * HD:NKV * HD + (h + 1) * HD]
            q4 = _group_rows(q_ref, h) * SCALE
            s = lax.dot_general(k_h, q4, (((1,), (1,)), ((), ())), preferred_element_type=F32) + bias_ref[tbl, h]
            sink = _sink_row(sink_ref, h)
            m = jnp.maximum(jnp.max(s, axis=0, keepdims=True), sink)
            p = jnp.exp(s - m)
            l = jnp.sum(p, axis=0, keepdims=True) + jnp.exp(sink - m)
            pn = (p * (1.0 / l)).astype(BF16)
            o4 = lax.dot_general(pn, v_h, (((0,), (0,)), ((), ())), preferred_element_type=F32)
            lse_ref[h:h + 1, :] = m + jnp.log(l)
            for g in range(GRP):
                hd = GRP * h + g
                cols = slice(hd * HD, (hd + 1) * HD)
                o = o4[g * BLK:(g + 1) * BLK]
                g_ref = glo_ref if hd < NQ // 2 else ghi_ref
                gc = slice((hd % (NQ // 2)) * HD, (hd % (NQ // 2) + 1) * HD)
                gt = g_ref[:, gc].astype(F32)
                o_ref[:, cols] = o.astype(BF16)
                ya_ref[:, cols] = (o * (gt * _sig(gt))).astype(BF16)

    blk = lambda w, cidx: pl.BlockSpec((BLK, w), lambda n: (n, cidx))
    return pl.pallas_call(
        body, name="attn_fwd", grid=(t // BLK,),
        in_specs=[_SMEM, _SMEM, pl.BlockSpec((2 * BLK, BLK), lambda n: (0, 0)),
                  blk(DA, 0), blk(512, 2), pl.BlockSpec((BLK, 512), lambda n: (jnp.maximum(n - 1, 0), 2)),
                  blk(512, 3), blk(512, 4)],
        out_specs=[blk(DA, 0), blk(DA, 0), pl.BlockSpec((None, NKV, GRP * BLK), lambda n: (n, 0, 0))],
        out_shape=[S((t, DA), BF16), S((t, DA), BF16), S((t // BLK, NKV, GRP * BLK), F32)],
        scratch_shapes=[pltpu.VMEM((2, NKV, 2 * BLK, GRP * BLK), F32)],
        compiler_params=_cp(("arbitrary",), 32),
    )(rel_bias, sinks, bucket, proj, proj, proj, proj, proj)


def _conv_rows(t):
    return min(256, t)


def _glu_into(u_scr, i, a_refs, b_refs, ah_refs, bh_refs, tc):
    for cc in range(DA // 128):
        half, lc = cc // 4, slice((cc % 4) * 128, (cc % 4 + 1) * 128)
        uh = ah_refs[half][:, lc].astype(F32) * _sig(bh_refs[half][:, lc].astype(F32))
        u_scr[cc, 0:HALO, :] = jnp.where(i == 0, 0.0, uh)
        u_scr[cc, HALO:HALO + tc, :] = a_refs[half][:, lc].astype(F32) * _sig(b_refs[half][:, lc].astype(F32))


def _conv_fwd(proj, cwb, conv_b, ln_g, ln_b, wpw, b_pw):
    t = proj.shape[0]
    tc = _conv_rows(t)
    rc = min(128, tc)

    def body(alo, ahi, blo, bhi, alo_h, ahi_h, blo_h, bhi_h, glo, ghi, cw_ref, cb_ref, lg_ref, lb_ref, wpw_ref, bpw_ref,
             uc_ref, pw_ref, yc_ref, u_scr):
        i = pl.program_id(0)
        _glu_into(u_scr, i, (alo, ahi), (blo, bhi), (alo_h, ahi_h), (blo_h, bhi_h), tc)

        def rows(r, carry):
            r0 = pl.multiple_of(r * rc, rc)
            for cc in range(DA // 128):
                lanes = pl.ds(cc * 128, 128)
                acc = jnp.zeros((rc // 8, 8, 128), F32)
                for k in range(CW):
                    u = u_scr[cc, pl.ds(r0 + (HALO - CW + 1) + k, rc), :].reshape(rc // 8, 8, 128)
                    acc = acc + u * cw_ref[k, :, lanes]
                uc_ref[pl.ds(r0, rc), lanes] = acc.reshape(rc, 128) + cb_ref[:, lanes]
            return carry

        lax.fori_loop(0, tc // rc, rows, 0)

        uc = uc_ref[...]
        mu = jnp.mean(uc, axis=-1, keepdims=True)
        xc = uc - mu
        rstd = lax.rsqrt(jnp.mean(xc * xc, axis=-1, keepdims=True) + LN_EPS)
        ln = xc * rstd * lg_ref[...] + lb_ref[...]
        sw = (ln * _sig(ln)).astype(BF16)
        pw = jnp.dot(sw, wpw_ref[...], preferred_element_type=F32) + bpw_ref[...]
        pw_ref[...] = pw.astype(BF16)
        gt = jnp.concatenate([glo[...], ghi[...]], axis=1).astype(F32)
        yc_ref[...] = (pw * (gt * _sig(gt))).astype(BF16)

    nh = tc // HALO
    cur = lambda cidx: pl.BlockSpec((tc, 512), lambda i: (i, cidx))
    halo = lambda cidx: pl.BlockSpec((HALO, 512), lambda i: (jnp.maximum(i * nh - 1, 0), cidx))
    row = pl.BlockSpec((1, DA), lambda i: (0, 0))
    full = pl.BlockSpec((tc, DA), lambda i: (i, 0))
    return pl.pallas_call(
        body, name="conv_fwd", grid=(t // tc,),
        in_specs=[cur(5), cur(6), cur(7), cur(8), halo(5), halo(6), halo(7), halo(8), cur(9), cur(10),
                  pl.BlockSpec((CW, 8, DA), lambda i: (0, 0, 0)), row, row, row,
                  pl.BlockSpec((DA, DA), lambda i: (0, 0)), row],
        out_specs=[full, full, full],
        out_shape=[S((t, DA), F32), S((t, DA), BF16), S((t, DA), BF16)],
        scratch_shapes=[pltpu.VMEM((DA // 128, HALO + tc, 128), F32)],
        compiler_params=_cp(("arbitrary",), 48),
    )(*([proj] * 10), cwb, conv_b, ln_g, ln_b, wpw, b_pw)


def _outproj_loss(ya, yc, w_out, x, target, gate, ln_g, ln_b):
    t = x.shape[0]
    tm = min(256, t)

    def body(ya_ref, yc_ref, w_ref, x_ref, t_ref, g_ref, lg_ref, lb_ref, gx_ref, dy_ref, st_ref):
        @pl.when(pl.program_id(0) == 0)
        def _():
            st_ref[...] = jnp.zeros_like(st_ref)

        y = jnp.dot(ya_ref[...], w_ref[0:DA, :], preferred_element_type=F32)
        y = y + jnp.dot(yc_ref[...], w_ref[DA:2 * DA, :], preferred_element_type=F32)
        gate_v = g_ref[...]
        z = ALPHA * x_ref[...] + gate_v * y
        mu = jnp.mean(z, axis=-1, keepdims=True)
        zc = z - mu
        rstd = lax.rsqrt(jnp.mean(zc * zc, axis=-1, keepdims=True) + LN_EPS)
        zh = zc * rstd
        diff = zh * lg_ref[...] + lb_ref[...] - t_ref[...]
        dout = diff * (1.0 / D)
        dzh = dout * lg_ref[...]
        m1 = jnp.mean(dzh, axis=-1, keepdims=True)
        m2 = jnp.mean(dzh * zh, axis=-1, keepdims=True)
        dz = rstd * (dzh - m1 - zh * m2)
        gx_ref[...] = ALPHA * dz
        dy_ref[...] = (dz * gate_v).astype(BF16)
        st_ref[0:1, :] += jnp.sum(dout * zh, axis=0, keepdims=True)
        st_ref[1:2, :] += jnp.sum(dout, axis=0, keepdims=True)
        st_ref[2:3, :] += jnp.sum(dz * y, axis=0, keepdims=True)
        st_ref[3:4, :] += jnp.sum(diff * diff, axis=0, keepdims=True) * (0.5 / D)

        @pl.when(pl.program_id(0) == t // tm - 1)
        def _():
            st_ref[3:4, :] = jnp.broadcast_to(jnp.sum(st_ref[3:4, :], axis=-1, keepdims=True), (1, D))

    row = pl.BlockSpec((1, D), lambda i: (0, 0))
    half = pl.BlockSpec((tm, DA), lambda i: (i, 0))
    full = pl.BlockSpec((tm, D), lambda i: (i, 0))
    return pl.pallas_call(
        body, name="outproj_loss", grid=(t // tm,),
        in_specs=[half, half, pl.BlockSpec((D, D), lambda i: (0, 0)), full, full, row, row, row],
        out_specs=[full, full, pl.BlockSpec((8, D), lambda i: (0, 0))],
        out_shape=[S((t, D), F32), S((t, D), BF16), S((8, D), F32)],
        compiler_params=_cp(("arbitrary",), 56),
    )(ya, yc, w_out, x, target, gate, ln_g, ln_b)


COL_DQ, COL_DKV, COL_DGA, COL_DGLU_A, COL_DGLU_B, COL_DGC = 0, 1024, 1536, 2560, 3584, 4608


def _tile_copy(dst, stage, sems, slot, row0, col0):
    rows, width = stage.shape[1:]
    return pltpu.make_async_copy(stage.at[slot], dst.at[pl.ds(row0, rows), pl.ds(col0, width)], sems.at[slot])


def _stage_slot(dst, stage, sems, step):
    slot = step % 2

    @pl.when(step >= 2)
    def _():
        _tile_copy(dst, stage, sems, slot, 0, 0).wait()

    return slot


def _tile_send(dst, stage, sems, step, nsteps, col0):
    rows = stage.shape[1]
    slot = step % 2
    _tile_copy(dst, stage, sems, slot, pl.multiple_of(step * rows, rows), col0).start()

    @pl.when(step == nsteps - 1)
    def _():
        _tile_copy(dst, stage, sems, slot, 0, col0).wait()
        if nsteps > 1:
            _tile_copy(dst, stage, sems, 1 - slot, 0, col0).wait()


def _dycat_gates(dy, w_out, attn_o, pw, proj, dproj):
    t = dy.shape[0]
    tm = min(256, t)
    nt = t // tm

    def body(dy_ref, w_ref, o_ref, pw_ref, galo, gahi, gclo, gchi, dp_in, dao_ref, dpw_ref, dp_ref,
             st_a, st_c, sem_a, sem_c):
        i = pl.program_id(0)
        sl_a = _stage_slot(dp_ref, st_a, sem_a, i)
        sl_c = _stage_slot(dp_ref, st_c, sem_c, i)
        dyc = lax.dot_general(dy_ref[...], w_ref[...], (((1,), (1,)), ((), ())), preferred_element_type=F32)
        da, dc = dyc[:, 0:DA], dyc[:, DA:2 * DA]
        ga = jnp.concatenate([galo[...], gahi[...]], axis=1).astype(F32)
        sa = _sig(ga)
        dao_ref[...] = (da * (ga * sa)).astype(BF16)
        st_a[sl_a] = (da * o_ref[...].astype(F32) * _dsilu(ga, sa)).astype(BF16)
        _tile_send(dp_ref, st_a, sem_a, i, nt, COL_DGA)
        gc = jnp.concatenate([gclo[...], gchi[...]], axis=1).astype(F32)
        sc = _sig(gc)
        dpw_ref[...] = (dc * (gc * sc)).astype(BF16)
        st_c[sl_c] = (dc * pw_ref[...].astype(F32) * _dsilu(gc, sc)).astype(BF16)
        _tile_send(dp_ref, st_c, sem_c, i, nt, COL_DGC)

    half = pl.BlockSpec((tm, DA), lambda i: (i, 0))
    cur = lambda cidx: pl.BlockSpec((tm, 512), lambda i: (i, cidx))
    return pl.pallas_call(
        body, name="dycat_gates", grid=(nt,),
        in_specs=[pl.BlockSpec((tm, D), lambda i: (i, 0)), pl.BlockSpec((D, D), lambda i: (0, 0)), half, half,
                  cur(3), cur(4), cur(9), cur(10), _ANY],
        out_specs=[half, half, _ANY],
        out_shape=[S((t, DA), BF16), S((t, DA), BF16), S(dproj.shape, dproj.dtype)],
        input_output_aliases={8: 2},
        scratch_shapes=[pltpu.VMEM((2, tm, DA), BF16), pltpu.VMEM((2, tm, DA), BF16),
                        pltpu.SemaphoreType.DMA((2,)), pltpu.SemaphoreType.DMA((2,))],
        compiler_params=_cp(("arbitrary",), 48),
    )(dy, w_out, attn_o, pw, proj, proj, proj, proj, dproj)


def _conv_bwd_ln(dpw, wpw, uc, ln_g, ln_b):
    t = dpw.shape[0]
    tc = min(256, t)

    def body(dpw_ref, w_ref, uc_ref, lg_ref, lb_ref, duc_ref, sw_ref, st_ref):
        @pl.when(pl.program_id(0) == 0)
        def _():
            st_ref[...] = jnp.zeros_like(st_ref)

        dpw_v = dpw_ref[...]
        ds = lax.dot_general(dpw_v, w_ref[...], (((1,), (1,)), ((), ())), preferred_element_type=F32)
        uc = uc_ref[...]
        mu = jnp.mean(uc, axis=-1, keepdims=True)
        xc = uc - mu
        rstd = lax.rsqrt(jnp.mean(xc * xc, axis=-1, keepdims=True) + LN_EPS)
        uh = xc * rstd
        ln = uh * lg_ref[...] + lb_ref[...]
        sg = _sig(ln)
        sw_ref[...] = (ln * sg).astype(BF16)
        dln = ds * _dsilu(ln, sg)
        dxh = dln * lg_ref[...]
        m1 = jnp.mean(dxh, axis=-1, keepdims=True)
        m2 = jnp.mean(dxh * uh, axis=-1, keepdims=True)
        duc = rstd * (dxh - m1 - uh * m2)
        duc_ref[...] = duc
        st_ref[0:1, :] += jnp.sum(dln * uh, axis=0, keepdims=True)
        st_ref[1:2, :] += jnp.sum(dln, axis=0, keepdims=True)
        st_ref[2:3, :] += jnp.sum(duc, axis=0, keepdims=True)
        st_ref[3:4, :] += jnp.sum(dpw_v.astype(F32), axis=0, keepdims=True)

    row = pl.BlockSpec((1, DA), lambda i: (0, 0))
    full = pl.BlockSpec((tc, DA), lambda i: (i, 0))
    return pl.pallas_call(
        body, name="conv_bwd_ln", grid=(t // tc,),
        in_specs=[full, pl.BlockSpec((DA, DA), lambda i: (0, 0)), full, row, row],
        out_specs=[full, full, pl.BlockSpec((8, DA), lambda i: (0, 0))],
        out_shape=[S((t, DA), F32), S((t, DA), BF16), S((8, DA), F32)],
        compiler_params=_cp(("arbitrary",), 48),
    )(dpw, wpw, uc, ln_g, ln_b)


def _conv_bwd_dw(duc, proj, cwb, dproj):
    t = duc.shape[0]
    tc = _conv_rows(t)
    rc = min(128, tc)
    nt = t // tc
    off = HALO - CW + 1

    def body(dcur, dnext, alo, ahi, blo, bhi, alo_h, ahi_h, blo_h, bhi_h, cw_ref, dp_in, gw_ref, dp_ref,
             u_scr, d_scr, g_scr, st_ab, sem_ab):
        i = pl.program_id(0)
        slot = _stage_slot(dp_ref, st_ab, sem_ab, i)

        @pl.when(i == 0)
        def _():
            g_scr[...] = jnp.zeros_like(g_scr)

        _glu_into(u_scr, i, (alo, ahi), (blo, bhi), (alo_h, ahi_h), (blo_h, bhi_h), tc)
        for cc in range(DA // 128):
            d_scr[cc, 0:tc, :] = dcur[:, cc * 128:(cc + 1) * 128]
            d_scr[cc, tc:tc + HALO, :] = jnp.where(i == nt - 1, 0.0, dnext[:, cc * 128:(cc + 1) * 128])

        def rows(r, carry):
            r0 = pl.multiple_of(r * rc, rc)
            for cc in range(DA // 128):
                lanes = pl.ds(cc * 128, 128)
                acc = jnp.zeros((rc // 8, 8, 128), F32)
                for j in range(CW):
                    dv = d_scr[cc, pl.ds(r0 + j, rc), :].reshape(rc // 8, 8, 128)
                    acc = acc + dv * cw_ref[CW - 1 - j, :, lanes]
                du = acc.reshape(rc, 128)
                d0 = d_scr[cc, pl.ds(r0, rc), :].reshape(rc // 8, 8, 128)
                for k in range(CW):
                    u = u_scr[cc, pl.ds(r0 + off + k, rc), :].reshape(rc // 8, 8, 128)
                    g_scr[k, :, lanes] += jnp.sum(d0 * u, axis=0)
                a_ref, b_ref = (alo, blo) if cc < 4 else (ahi, bhi)
                lc = pl.ds((cc % 4) * 128, 128)
                av = a_ref[pl.ds(r0, rc), lc].astype(F32)
                sb = _sig(b_ref[pl.ds(r0, rc), lc].astype(F32))
                st_ab[slot, pl.ds(r0, rc), lanes] = (du * sb).astype(BF16)
                st_ab[slot, pl.ds(r0, rc), pl.ds(DA + cc * 128, 128)] = (du * av * sb * (1.0 - sb)).astype(BF16)
            return carry

        lax.fori_loop(0, tc // rc, rows, 0)
        _tile_send(dp_ref, st_ab, sem_ab, i, nt, COL_DGLU_A)

        @pl.when(i == nt - 1)
        def _():
            gw_ref[0:CW, :] = jnp.sum(g_scr[...], axis=1)
            gw_ref[CW:32, :] = jnp.zeros((32 - CW, DA), F32)

    nh = tc // HALO
    nhb = t // HALO
    cur = lambda cidx: pl.BlockSpec((tc, 512), lambda i: (i, cidx))
    halo = lambda cidx: pl.BlockSpec((HALO, 512), lambda i: (jnp.maximum(i * nh - 1, 0), cidx))
    full = pl.BlockSpec((tc, DA), lambda i: (i, 0))
    return pl.pallas_call(
        body, name="conv_bwd_dw", grid=(nt,),
        in_specs=[full, pl.BlockSpec((HALO, DA), lambda i: (jnp.minimum((i + 1) * nh, nhb - 1), 0)),
                  cur(5), cur(6), cur(7), cur(8), halo(5), halo(6), halo(7), halo(8),
                  pl.BlockSpec((CW, 8, DA), lambda i: (0, 0, 0)), _ANY],
        out_specs=[pl.BlockSpec((32, DA), lambda i: (0, 0)), _ANY],
        out_shape=[S((32, DA), F32), S(dproj.shape, dproj.dtype)],
        input_output_aliases={11: 1},
        scratch_shapes=[pltpu.VMEM((DA // 128, HALO + tc, 128), F32), pltpu.VMEM((DA // 128, tc + HALO, 128), F32),
                        pltpu.VMEM((CW, 8, DA), F32), pltpu.VMEM((2, tc, 2 * DA), BF16), pltpu.SemaphoreType.DMA((2,))],
        compiler_params=_cp(("arbitrary",), 48),
    )(duc, duc, *([proj] * 8), cwb, dproj)


def _attn_bwd(proj, dao, lse, rel_bias, sinks, bucket, dproj):
    t = proj.shape[0]

    def body(rel_ref, sink_ref, bk_ref, q_ref, kvc_ref, kvp_ref, do_ref, lse_ref, dp_in,
             dcur_ref, dprev_ref, dsacc_ref, dsk_ref, dp_ref, bias_ref, st_q, sem_q):
        n = pl.program_id(0)
        slot = _stage_slot(dp_ref, st_q, sem_q, n)

        @pl.when(n == 0)
        def _():
            _build_bias(rel_ref, bk_ref, bias_ref)
            dsacc_ref[...] = jnp.zeros_like(dsacc_ref)
            dsk_ref[...] = jnp.zeros_like(dsk_ref)

        tbl = jnp.where(n == 0, 0, 1)
        kv = jnp.concatenate([kvp_ref[...], kvc_ref[...]], axis=0)
        for h in range(NKV):
            k_h = kv[:, h * HD:(h + 1) * HD]
            v_h = kv[:, NKV * HD + h * HD:NKV * HD + (h + 1) * HD]
            q4 = _group_rows(q_ref, h) * SCALE
            do4 = _group_rows(do_ref, h)
            lse_h = lse_ref[h:h + 1, :]
            s = lax.dot_general(k_h, q4, (((1,), (1,)), ((), ())), preferred_element_type=F32)
            p = jnp.exp(s + bias_ref[tbl, h] - lse_h)
            dp = lax.dot_general(v_h, do4, (((1,), (1,)), ((), ())), preferred_element_type=F32)
            delta = jnp.sum(p * dp, axis=0, keepdims=True)
            ds = p * (dp - delta)
            dsacc_ref[h] += ds
            dsk_ref[h:h + 1, :] += -jnp.exp(_sink_row(sink_ref, h) - lse_h) * delta
            dsb = ds.astype(BF16)
            dq4 = lax.dot_general(dsb, k_h, (((0,), (0,)), ((), ())), preferred_element_type=F32) * SCALE
            for g in range(GRP):
                hd = GRP * h + g
                st_q[slot, :, hd * HD:(hd + 1) * HD] = dq4[g * BLK:(g + 1) * BLK].astype(BF16)
            dk = jnp.dot(dsb, q4, preferred_element_type=F32)
            dv = jnp.dot(p.astype(BF16), do4, preferred_element_type=F32)
            kc = slice(h * HD, (h + 1) * HD)
            vc = slice(NKV * HD + h * HD, NKV * HD + (h + 1) * HD)
            dprev_ref[:, kc] = dk[0:BLK]
            dcur_ref[:, kc] = dk[BLK:2 * BLK]
            dprev_ref[:, vc] = dv[0:BLK]
            dcur_ref[:, vc] = dv[BLK:2 * BLK]

        _tile_send(dp_ref, st_q, sem_q, n, t // BLK, COL_DQ)

        @pl.when(n == t // BLK - 1)
        def _():
            for g in range(GRP):
                lanes = slice(g * BLK, (g + 1) * BLK)
                dsk_ref[:, lanes] = jnp.broadcast_to(jnp.sum(dsk_ref[:, lanes], axis=-1, keepdims=True), (NKV, BLK))

    blk = lambda w, cidx: pl.BlockSpec((BLK, w), lambda n: (n, cidx))
    return pl.pallas_call(
        body, name="attn_bwd", grid=(t // BLK,),
        in_specs=[_SMEM, _SMEM, pl.BlockSpec((2 * BLK, BLK), lambda n: (0, 0)),
                  blk(DA, 0), blk(512, 2), pl.BlockSpec((BLK, 512), lambda n: (jnp.maximum(n - 1, 0), 2)),
                  blk(DA, 0), pl.BlockSpec((None, NKV, GRP * BLK), lambda n: (n, 0, 0)), _ANY],
        out_specs=[blk(512, 0), blk(512, 0),
                   pl.BlockSpec((NKV, 2 * BLK, GRP * BLK), lambda n: (0, 0, 0)), pl.BlockSpec((NKV, GRP * BLK), lambda n: (0, 0)),
                   _ANY],
        out_shape=[S((t, 512), F32), S((t, 512), F32), S((NKV, 2 * BLK, GRP * BLK), F32),
                   S((NKV, GRP * BLK), F32), S(dproj.shape, dproj.dtype)],
        input_output_aliases={8: 4},
        scratch_shapes=[pltpu.VMEM((2, NKV, 2 * BLK, GRP * BLK), F32), pltpu.VMEM((2, BLK, DA), BF16),
                        pltpu.SemaphoreType.DMA((2,))],
        compiler_params=_cp(("arbitrary",), 40),
    )(rel_bias, sinks, bucket, proj, proj, proj, dao, lse, dproj)


def _dkv_fix(dcur, dprev, dproj):
    t = dcur.shape[0]
    tb = min(1024, t)
    nt = t // tb
    per = tb // BLK

    def body(c_ref, p_ref, pn_ref, dp_in, dp_ref, st_kv, sem_kv):
        i = pl.program_id(0)
        slot = _stage_slot(dp_ref, st_kv, sem_kv, i)
        if tb > BLK:
            st_kv[slot, 0:tb - BLK, :] = (c_ref[0:tb - BLK, :] + p_ref[BLK:tb, :]).astype(BF16)
        nxt = jnp.where(i == nt - 1, 0.0, pn_ref[...])
        st_kv[slot, tb - BLK:tb, :] = (c_ref[tb - BLK:tb, :] + nxt).astype(BF16)
        _tile_send(dp_ref, st_kv, sem_kv, i, nt, COL_DKV)

    tile = pl.BlockSpec((tb, 512), lambda i: (i, 0))
    return pl.pallas_call(
        body, name="dkv_fix", grid=(nt,),
        in_specs=[tile, tile, pl.BlockSpec((BLK, 512), lambda i: (jnp.minimum((i + 1) * per, t // BLK - 1), 0)), _ANY],
        out_specs=_ANY,
        out_shape=S(dproj.shape, dproj.dtype),
        input_output_aliases={3: 0},
        scratch_shapes=[pltpu.VMEM((2, tb, 512), BF16), pltpu.SemaphoreType.DMA((2,))],
        compiler_params=_cp(("arbitrary",), 32),
    )(dcur, dprev, dprev, dproj)


def _bias_grad(dsacc, bucket):
    def body(ds_ref, bk_ref, o_ref, row_scr):
        bk = bk_ref[...]

        def group(h, carry):
            for g in range(GRP):
                dsv = ds_ref[h, :, g * BLK:(g + 1) * BLK]
                for b in range(N_BUCKETS):
                    row_scr[GRP * h + g, b:b + 1, :] = jnp.sum(jnp.where(bk == b, dsv, 0.0), axis=0, keepdims=True)
            return carry

        lax.fori_loop(0, NKV, group, 0)
        for hd in range(NQ):
            o_ref[hd] = jnp.sum(row_scr[hd], axis=-1, keepdims=True)

    return pl.pallas_call(
        body, name="bias_grad", out_shape=S((NQ, N_BUCKETS, 1), F32),
        in_specs=[_VMEM, _VMEM], out_specs=_VMEM,
        scratch_shapes=[pltpu.VMEM((NQ, N_BUCKETS, BLK), F32)],
        compiler_params=_cp(None, 32),
    )(dsacc, bucket)


def _dh_gradx(dproj, wg, gx0, x, scale1p, p_in, p_out, p_pw):
    t = x.shape[0]
    tm = min(256, t)
    ni = t // tm
    bn = D // 2
    nn = D // bn

    def body(dp_ref, w_ref, gx_ref, x_ref, sc_ref, pi, po, pp, out_ref, st_ref, li, lo, lp, send_sems, recv_sems):
        n, i = pl.program_id(0), pl.program_id(1)

        @pl.when((n == 0) & (i == 0))
        def _():
            for cp in _chip_exchange_copies(pi, po, pp, li, lo, lp, send_sems, recv_sems):
                cp.start()

        @pl.when(i == 0)
        def _():
            st_ref[...] = jnp.zeros_like(st_ref)

        dh = None
        for j in range(NCHIP):
            part = lax.dot_general(dp_ref[:, j * NB:(j + 1) * NB], w_ref[j], (((1,), (1,)), ((), ())),
                                   preferred_element_type=F32)
            dh = part if dh is None else dh + part
        out_ref[...] = gx_ref[...] + dh * sc_ref[...]
        st_ref[0:1, :] += jnp.sum(dh, axis=0, keepdims=True)
        st_ref[1:2, :] += jnp.sum(dh * x_ref[...], axis=0, keepdims=True)

        @pl.when((n == nn - 1) & (i == ni - 1))
        def _():
            cps = _chip_exchange_copies(pi, po, pp, li, lo, lp, send_sems, recv_sems)
            for cp in cps:
                cp.wait_recv()
            for cp in cps:
                cp.wait_send()

    tile = pl.BlockSpec((tm, bn), lambda n, i: (i, n))
    return pl.pallas_call(
        body, name="dh_gradx", grid=(nn, ni),
        in_specs=[pl.BlockSpec((tm, D_IN), lambda n, i: (i, 0)), pl.BlockSpec((NCHIP, bn, NB), lambda n, i: (0, n, 0)),
                  tile, tile, pl.BlockSpec((1, bn), lambda n, i: (0, n)), _ANY, _ANY, _ANY],
        out_specs=[tile, pl.BlockSpec((8, bn), lambda n, i: (0, n)), _ANY, _ANY, _ANY],
        out_shape=[S((t, D), F32), S((8, D), F32), S((3, p_in.shape[0], NB), p_in.dtype),
                   S((3,) + p_out.shape[1:], p_out.dtype), S((3,) + p_pw.shape[1:], p_pw.dtype)],
        scratch_shapes=[pltpu.SemaphoreType.DMA((9,)), pltpu.SemaphoreType.DMA((9,))],
        compiler_params=_cp(("arbitrary", "arbitrary"), 56),
    )(dproj, wg, gx0, x, scale1p, p_in, p_out, p_pw)


def _atb(a_parts, b, bn, name):
    bm = DA
    t = b.shape[0]
    n = b.shape[1]
    tk = min(2048, t)
    nk = t // tk
    na = len(a_parts)

    def body(*refs):
        a_refs, b_ref, o_ref = refs[:na], refs[na], refs[na + 1]
        mi, k = pl.program_id(0), pl.program_id(2)
        for q in range(na):
            @pl.when(mi == q)
            def _(q=q):
                part = lax.dot_general(a_refs[q][...], b_ref[...], (((0,), (0,)), ((), ())), preferred_element_type=F32)

                @pl.when(k == 0)
                def _():
                    o_ref[...] = part

                @pl.when(k > 0)
                def _():
                    o_ref[...] += part

    a_spec = lambda q, cidx: pl.BlockSpec((tk, bm), lambda mi, j, k: (jnp.where(mi == q, k, 0), cidx))
    return pl.pallas_call(
        body, name=name, grid=(na, n // bn, nk),
        in_specs=[a_spec(q, cidx) for q, (_, cidx) in enumerate(a_parts)] + [pl.BlockSpec((tk, bn), lambda mi, j, k: (k, j))],
        out_specs=pl.BlockSpec((bm, bn), lambda mi, j, k: (mi, j)),
        out_shape=S((na * bm, n), F32),
        compiler_params=_cp(("parallel", "parallel", "arbitrary"), 56),
    )(*[arr for arr, _ in a_parts], b)


def _sum8(parts):
    _, r, n = parts.shape

    def body(p_ref, o_ref):
        v = p_ref[0]
        for d in range(1, 8):
            v = v + p_ref[d]
        o_ref[...] = v

    return pl.pallas_call(body, name="sum8", out_shape=S((r, n), F32), in_specs=[_VMEM], out_specs=_VMEM,
                          compiler_params=_cp(None, 32))(parts)


def _adam_math(w, g, m, v):
    m = B1 * m + (1.0 - B1) * g
    v = B2 * v + (1.0 - B2) * (g * g)
    m_hat = m / (1.0 - B1 ** STEP)
    v_hat = v / (1.0 - B2 ** STEP)
    delta = -LR * (m_hat / (jnp.sqrt(v_hat) + EPS) + WD * w)
    return delta, m, v


def _adamw(w, g, m, v, name):
    r, n = w.shape
    tr = min(256, r)

    def body(w_ref, g_ref, m_ref, v_ref, d_ref, nm_ref, nv_ref):
        d_ref[...], nm_ref[...], nv_ref[...] = _adam_math(w_ref[...], g_ref[...], m_ref[...], v_ref[...])

    spec = pl.BlockSpec((tr, n), lambda i: (i, 0))
    return pl.pallas_call(
        body, name=name, grid=(r // tr,), in_specs=[spec] * 4, out_specs=[spec] * 3,
        out_shape=[S((r, n), F32)] * 3, compiler_params=_cp(("parallel",), 48),
    )(w, g, m, v)


def _adamw_ada(ct, dmod_p, w, m, v):
    r, n = w.shape
    tr = min(256, r)

    def body(ct_ref, dm_ref, w_ref, m_ref, v_ref, g_ref, d_ref, nm_ref, nv_ref):
        cv = ct_ref[...]
        g = jnp.dot(cv * _sig(cv), dm_ref[...], preferred_element_type=F32)
        g_ref[...] = g
        d_ref[...], nm_ref[...], nv_ref[...] = _adam_math(w_ref[...], g, m_ref[...], v_ref[...])

    spec = pl.BlockSpec((tr, n), lambda i: (i, 0))
    return pl.pallas_call(
        body, name="adamw_ada", grid=(r // tr,),
        in_specs=[pl.BlockSpec((tr, 8), lambda i: (i, 0)), pl.BlockSpec((8, n), lambda i: (0, 0)), spec, spec, spec],
        out_specs=[spec] * 4, out_shape=[S((r, n), F32)] * 4, compiler_params=_cp(("parallel",), 48),
    )(ct, dmod_p, w, m, v)


def _adamw_small(ws, gs, ms, vs):
    k = len(ws)

    def body(*refs):
        ins, outs = refs[:4 * k], refs[4 * k:]
        for i in range(k):
            d, nm, nv = _adam_math(ins[i][...], ins[k + i][...], ins[2 * k + i][...], ins[3 * k + i][...])
            outs[i][...] = d
            outs[k + i][...] = nm
            outs[2 * k + i][...] = nv

    shapes = [S(w.shape, F32) for w in ws]
    return pl.pallas_call(body, name="adamw_small", out_shape=shapes * 3, in_specs=[_VMEM] * (4 * k),
                          out_specs=[_VMEM] * (3 * k), compiler_params=_cp(None, 32))(*ws, *gs, *ms, *vs)


def _bucket_map():
    qi = jnp.arange(BLK, dtype=jnp.int32)[None, :]
    kj = jnp.arange(2 * BLK, dtype=jnp.int32)[:, None]
    dist = qi + BLK - kj
    dd = jnp.maximum(dist, 0)
    max_exact = N_BUCKETS // 2
    dfl = jnp.maximum(dd, 1).astype(F32)
    large = max_exact + (jnp.log(dfl / max_exact) / math.log(MAX_DIST / max_exact) * (N_BUCKETS - max_exact)).astype(jnp.int32)
    large = jnp.minimum(large, N_BUCKETS - 1)
    bucket = jnp.where(dd < max_exact, dd, large)
    return jnp.where((dist >= 0) & (dist < BLK), bucket, -1).astype(jnp.int32)


def _pad_rows(a, rows):
    return jnp.pad(a, ((0, rows - a.shape[0]), (0, 0)))


def kernel(x, c, w_ada, b_ada, w_in, rel_bias, sinks, conv_w, conv_b, conv_ln_g, conv_ln_b, w_pw, b_pw, w_out, ln_g, ln_b, loss_target, m_w_ada, m_b_ada, m_w_in, m_rel_bias, m_sinks, m_conv_w, m_conv_b, m_conv_ln_g, m_conv_ln_b, m_w_pw, m_b_pw, m_w_out, m_ln_g, m_ln_b, v_w_ada, v_b_ada, v_w_in, v_rel_bias, v_sinks, v_conv_w, v_conv_b, v_conv_ln_g, v_conv_ln_b, v_w_pw, v_b_pw, v_w_out, v_ln_g, v_ln_b):
    mx, my, mc = _me()
    chip = 2 * mx + my
    dev = 2 * chip + mc
    t = x.shape[1]
    x2 = x.reshape(t, D)
    tgt = loss_target.reshape(t, D)
    n_ada = w_ada.shape[2]
    cw_cols = conv_w.shape[2]

    pack0 = jnp.concatenate([_pad_rows(c, 8), _pad_rows(_pad_rows(conv_w[0], 32).reshape(-1, D), 8)], axis=0)
    g0 = _allgather8(pack0, "gather_c_convw").reshape(8, 16, D)
    c_all = g0[:, 0, :]
    cw_rows = 32 * cw_cols // D
    cw_full = jnp.concatenate([g0[2 * q, 8:8 + cw_rows, :].reshape(32, cw_cols) for q in range(NCHIP)], axis=1)[:CW]
    cwb = jnp.broadcast_to(cw_full[:, None, :], (CW, 8, DA))

    b_ada_p = lax.dynamic_slice(b_ada, (0, chip * n_ada), (1, n_ada))
    mod_part = _ada_fwd(c_all, w_ada[0], b_ada_p)
    mod_all = _allgather8(mod_part, "gather_mod").reshape(8, 8, n_ada)
    mod = jnp.concatenate([lax.dynamic_slice(mod_all[2 * q], (dev, 0), (1, n_ada)) for q in range(NCHIP)], axis=1)
    shift, scale, gate = mod[:, 0:D], mod[:, D:2 * D], mod[:, 2 * D:3 * D]
    scale1p = 1.0 + scale

    bucket = _bucket_map()
    sel_chip = jnp.reshape(chip, (1,)).astype(jnp.int32)
    sel = jnp.reshape(mc, (1,)).astype(jnp.int32)
    proj, h, wg_in, wg_out, wg_pw = _inproj_gather(
        x2, scale1p, shift, w_in[0].astype(BF16), w_out[0].astype(BF16), w_pw[0].astype(BF16), sel_chip)
    w_out_f = wg_out.reshape(D, D)
    w_pw_f = wg_pw.reshape(DA, DA)
    ya, attn_o, lse = _attn_fwd(proj, rel_bias, sinks, bucket)
    uc, pw, yc = _conv_fwd(proj, cwb, conv_b, conv_ln_g, conv_ln_b, w_pw_f, b_pw)
    gx0, dy, st_out = _outproj_loss(ya, yc, w_out_f, x2, tgt, gate, ln_g, ln_b)

    dproj = lax.empty((t, D_IN), BF16)
    dao, dpw, dproj = _dycat_gates(dy, w_out_f, attn_o, pw, proj, dproj)
    duc, sw, st_conv = _conv_bwd_ln(dpw, w_pw_f, uc, conv_ln_g, conv_ln_b)
    gw_conv, dproj = _conv_bwd_dw(duc, proj, cwb, dproj)
    dkv_cur, dkv_prev, dsacc, dsk, dproj = _attn_bwd(proj, dao, lse, rel_bias, sinks, bucket, dproj)
    dproj = _dkv_fix(dkv_cur, dkv_prev, dproj)
    g_rel = _bias_grad(dsacc, bucket).reshape(NQ, N_BUCKETS).T
    gp_in = _atb([(h, 0), (h, 1)], dproj, NB, "grad_w_in")
    gp_out = _atb([(ya, 0), (yc, 0)], dy, 1024, "grad_w_out")
    gp_pw = _atb([(sw, 0)], dpw, 1024, "grad_w_pw")

    r_out, r_pw = D // NCHIP // 2, DA // NCHIP // 2
    gi3, go3, gp3 = gp_in.reshape(2, D // 2, D_IN), gp_out.reshape(8, r_out, D), gp_pw.reshape(8, r_pw, DA)
    l_in, l_out, l_pw = _pair_exchange(gi3, go3, gp3)
    pi32, pi16 = _pair_sum(gi3, l_in.reshape(1, D // 2, D_IN), sel, "pair_sum_in")
    po32, po16 = _pair_sum(go3, l_out, sel, "pair_sum_out")
    pp32, pp16 = _pair_sum(gp3, l_pw, sel, "pair_sum_pw")
    grad_x, st_in, rc_in, rc_out, rc_pw = _dh_gradx(dproj, wg_in, gx0, x2, scale1p, pi16[0], po16, pp16)
    own_in = lax.dynamic_slice(pi32[0], (0, chip * NB), (D // 2, NB))
    own_out = lax.dynamic_index_in_dim(po32, chip, 0, keepdims=False)
    own_pw = lax.dynamic_index_in_dim(pp32, chip, 0, keepdims=False)
    h_in = _chip_sum(own_in, rc_in, sel, "chip_sum_in")
    h_out = _chip_sum(own_out, rc_out, sel, "chip_sum_out")
    h_pw = _chip_sum(own_pw, rc_pw, sel, "chip_sum_pw")
    f_in, f_out, f_pw = _pair_share(h_in, h_out, h_pw)
    g_w_in = f_in.reshape(D, NB)
    g_w_out = f_out.reshape(D // NCHIP, D)
    g_w_pw = f_pw.reshape(DA // NCHIP, DA)

    dmod = jnp.concatenate([st_in[0:1], st_in[1:2], st_out[2:3]], axis=1)
    loss_row = st_out[3:4, 0:1]
    small = jnp.concatenate([
        dmod, st_out[0:1], st_out[1:2],
        st_conv[0:1], st_conv[1:2], st_conv[2:3], st_conv[3:4],
        g_rel.reshape(1, N_BUCKETS * NQ), dsk.reshape(NKV, GRP, BLK)[:, :, 0].reshape(1, NQ), loss_row,
        gw_conv[:CW].reshape(1, CW * DA)], axis=1)
    n_small = small.shape[1]
    rows_small = -(-n_small // (8 * D)) * 8
    small = jnp.pad(small, ((0, 0), (0, rows_small * D - n_small))).reshape(rows_small, D)
    parts = _allgather8(small, "gather_small").reshape(8, rows_small, D)
    tot = _sum8(parts).reshape(1, rows_small * D)
    dmod_all = parts.reshape(8, rows_small * D)[:, 0:3 * D]

    o = 3 * D
    def take(nn):
        nonlocal o
        v = tot[:, o:o + nn]
        o += nn
        return v
    g_b_ada = tot[:, 0:3 * D]
    g_ln_g, g_ln_b = take(D), take(D)
    g_cln_g, g_cln_b, g_conv_b, g_b_pw = take(DA), take(DA), take(DA), take(DA)
    g_rel_bias = take(N_BUCKETS * NQ).reshape(N_BUCKETS, NQ)
    g_sinks = take(NQ)
    loss = take(1).reshape(())
    g_conv_w_full = take(CW * DA).reshape(CW, DA)
    g_conv_w = lax.dynamic_slice(g_conv_w_full, (0, chip * cw_cols), (CW, cw_cols))

    dmod_p = lax.dynamic_slice(dmod_all, (0, chip * n_ada), (8, n_ada))
    g_w_ada, d_w_ada, nm_w_ada, nv_w_ada = _adamw_ada(c_all.T, dmod_p, w_ada[0], m_w_ada[0], v_w_ada[0])
    d_w_in, nm_w_in, nv_w_in = _adamw(w_in[0], g_w_in, m_w_in[0], v_w_in[0], "adamw_in")
    d_w_out, nm_w_out, nv_w_out = _adamw(w_out[0], g_w_out, m_w_out[0], v_w_out[0], "adamw_out")
    d_w_pw, nm_w_pw, nv_w_pw = _adamw(w_pw[0], g_w_pw, m_w_pw[0], v_w_pw[0], "adamw_pw")
    small_w = [b_ada, rel_bias, sinks, conv_w[0], conv_b, conv_ln_g, conv_ln_b, b_pw, ln_g, ln_b]
    small_g = [g_b_ada, g_rel_bias, g_sinks, g_conv_w, g_conv_b, g_cln_g, g_cln_b, g_b_pw, g_ln_g, g_ln_b]
    small_m = [m_b_ada, m_rel_bias, m_sinks, m_conv_w[0], m_conv_b, m_conv_ln_g, m_conv_ln_b, m_b_pw, m_ln_g, m_ln_b]
    small_v = [v_b_ada, v_rel_bias, v_sinks, v_conv_w[0], v_conv_b, v_conv_ln_g, v_conv_ln_b, v_b_pw, v_ln_g, v_ln_b]
    res = _adamw_small(small_w, small_g, small_m, small_v)
    ns = len(small_w)
    d_s, nm_s, nv_s = res[:ns], res[ns:2 * ns], res[2 * ns:]

    def ordered(w_ada_, w_in_, w_pw_, w_out_, sm):
        b_ada_, rel_, sinks_, conv_w_, conv_b_, cln_g_, cln_b_, b_pw_, ln_g_, ln_b_ = sm
        return (w_ada_[None], b_ada_, w_in_[None], rel_, sinks_, conv_w_[None], conv_b_, cln_g_, cln_b_,
                w_pw_[None], b_pw_, w_out_[None], ln_g_, ln_b_)

    grads = ordered(g_w_ada, g_w_in, g_w_pw, g_w_out, small_g)
    deltas = ordered(d_w_ada, d_w_in, d_w_pw, d_w_out, d_s)
    new_m = ordered(nm_w_ada, nm_w_in, nm_w_pw, nm_w_out, nm_s)
    new_v = ordered(nv_w_ada, nv_w_in, nv_w_pw, nv_w_out, nv_s)
    return (loss, grad_x.reshape(1, t, D), *grads, *deltas, *new_m, *new_v)
```

```python
import functools
import math

import jax
import jax.numpy as jnp
from jax import lax
from jax.experimental import pallas as pl
from jax.experimental.pallas import tpu as pltpu

F32, BF16 = jnp.float32, jnp.bfloat16
S = jax.ShapeDtypeStruct
MESH = pl.DeviceIdType.MESH

D = 2048
DA = 1024
HD = 64
NQ, NKV, GRP = 16, 4, 4
BLK = 128
CW = 31
HALO = 32
D_IN = 5632
NCHIP = 4
NB = D_IN // NCHIP
N_BUCKETS, MAX_DIST = 32, 128
LN_EPS = 1e-5
ALPHA = 2.0 ** 0.25
SCALE = HD ** -0.5
NEG = -1e30
LR, B1, B2, EPS, WD, STEP = 0.001, 0.9, 0.999, 1e-08, 0.01, 10

_VMEM = pl.BlockSpec(memory_space=pltpu.VMEM)
_SMEM = pl.BlockSpec(memory_space=pltpu.SMEM)
_ANY = pl.BlockSpec(memory_space=pl.ANY)


def _cp(sem=None, vmem_mb=None):
    kw = {}
    if sem is not None:
        kw["dimension_semantics"] = sem
    if vmem_mb is not None:
        kw["vmem_limit_bytes"] = vmem_mb * 1024 * 1024
    return pltpu.CompilerParams(**kw)


def _sig(v):
    return jax.nn.sigmoid(v)


def _dsilu(g, sg):
    return sg * (1.0 + g * (1.0 - sg))


def _me():
    return lax.axis_index("x"), lax.axis_index("y"), lax.axis_index("c")


def _allgather8(x_shard, name):
    m_per, n = x_shard.shape

    def body(x_ref, out_ref, send_sems, recv_sems, local_sem):
        x, y, c = _me()
        me, sibling = (x, y, c), (x, y, 1 - c)
        chips = [(1 - x, y), (x, 1 - y), (1 - x, 1 - y)]

        def rows(px, py, pc):
            return out_ref.at[pl.ds((4 * px + 2 * py + pc) * m_per, m_per), :]

        def copy(k, block, to, src=None):
            return pltpu.make_async_remote_copy(
                src_ref=rows(*block) if src is None else src, dst_ref=rows(*block),
                send_sem=send_sems.at[k], recv_sem=recv_sems.at[k], device_id=to, device_id_type=MESH)

        mine = pltpu.make_async_copy(x_ref, rows(*me), local_sem)
        mine.start()
        first = [copy(0, me, sibling, src=x_ref)]
        first += [copy(1 + j, me, (*chip, c), src=x_ref) for j, chip in enumerate(chips)]
        for cp in first:
            cp.start()
        passed = [copy(4 + j, (*chip, c), sibling) for j, chip in enumerate(chips)]
        for j, chip in enumerate(chips):
            copy(1 + j, (*chip, c), me).wait_recv()
            passed[j].start()
        copy(0, sibling, me).wait_recv()
        for j, chip in enumerate(chips):
            copy(4 + j, (*chip, 1 - c), me).wait_recv()
        for cp in first + passed:
            cp.wait_send()
        mine.wait()

    return pl.pallas_call(
        body, name=name, out_shape=S((8 * m_per, n), x_shard.dtype),
        in_specs=[_VMEM], out_specs=_VMEM,
        scratch_shapes=[pltpu.SemaphoreType.DMA((7,)), pltpu.SemaphoreType.DMA((7,)), pltpu.SemaphoreType.DMA],
    )(x_shard)


def _inproj_gather(x, scale1p, shift, wi, wo, wp, chip_id):
    t = x.shape[0]
    tm = min(512, t)
    ni = t // tm
    shards = (wi, wo, wp)
    halves = [s.shape[0] // 2 for s in shards]

    def body(cid_ref, x_ref, sc_ref, sh_ref, wi_ref, wo_ref, wp_ref, proj_ref, h_ref, gi_ref, go_ref, gp_ref,
             wbuf, obuf, pbuf, hb_scr, send_sems, recv_sems, ld_sems, st_sems):
        jj, i = pl.program_id(0), pl.program_id(1)
        mx, my, c = _me()
        p = 2 * mx + my
        sibling = (mx, my, 1 - c)
        srcs, dsts, bufs = (wi_ref, wo_ref, wp_ref), (gi_ref, go_ref, gp_ref), (wbuf, obuf, pbuf)
        chips = {1: (mx, 1 - my), 2: (1 - mx, my), 3: (1 - mx, 1 - my)}

        def half(ref, w, hc):
            return ref.at[pl.ds(hc * halves[w], halves[w])]

        def copy(k, src, dst, to):
            return pltpu.make_async_remote_copy(src_ref=src, dst_ref=dst, send_sem=send_sems.at[k],
                                                recv_sem=recv_sems.at[k], device_id=to, device_id_type=MESH)

        def sent(w, m):
            return copy(3 * w + m - 1, half(srcs[w], w, c), half(dsts[w].at[p], w, c), (*chips[m], c))

        def landed(w, m):
            blk = half(dsts[w].at[jnp.bitwise_xor(p, m)], w, c)
            return copy(3 * w + m - 1, blk, blk, (*chips[m], c))

        def passed(w, m, hc):
            blk = half(dsts[w].at[jnp.bitwise_xor(p, m)], w, hc)
            return copy(9 + 3 * w + m - 1, blk, blk, sibling)

        def vm(w, slot):
            return bufs[w].at[slot] if w == 0 else bufs[w]

        def load(w, src, slot=0):
            return pltpu.make_async_copy(src, vm(w, slot), ld_sems.at[w])

        def store(w):
            return pltpu.make_async_copy(vm(w, 0), dsts[w].at[p], st_sems.at[w])

        def block_load(m):
            return load(0, gi_ref.at[jnp.bitwise_xor(p, m)], m % 2)

        @pl.when((jj == 0) & (i == 0))
        def _():
            sent(0, 1).start()
            sent(0, 2).start()
            for w in range(3):
                load(w, srcs[w]).start()
            for w in range(3):
                load(w, srcs[w]).wait()
                store(w).start()

        for m in (1, 2, 3):
            @pl.when((jj == m) & (i == 0))
            def _(m=m):
                if m == 1:
                    sent(0, 3).start()
                    store(0).wait()
                if m == 2:
                    for w in (1, 2):
                        for mm in (1, 2, 3):
                            sent(w, mm).start()
                block_load(m).wait()

        hb_scr[...] = (x_ref[...] * sc_ref[...] + sh_ref[...]).astype(BF16)

        @pl.when(jj == 0)
        def _():
            h_ref[...] = hb_scr[...]

        proj_ref[...] = jnp.dot(hb_scr[...], wbuf[jj % 2], preferred_element_type=F32).astype(BF16)

        for m in (1, 2, 3):
            @pl.when((jj == m - 1) & (i == ni - 1))
            def _(m=m):
                landed(0, m).wait_recv()
                passed(0, m, c).start()
                passed(0, m, 1 - c).wait_recv()
                block_load(m).start()

        @pl.when((jj == NCHIP - 1) & (i == ni - 1))
        def _():
            for w in (1, 2):
                for m in (1, 2, 3):
                    landed(w, m).wait_recv()
                    passed(w, m, c).start()
            for w in (1, 2):
                for m in (1, 2, 3):
                    passed(w, m, 1 - c).wait_recv()
            for w in range(3):
                for m in (1, 2, 3):
                    sent(w, m).wait_send()
                    passed(w, m, c).wait_send()
            store(1).wait()
            store(2).wait()

    row = pl.BlockSpec((1, D), lambda jj, i, s: (0, 0))
    return pl.pallas_call(
        body, name="inproj_gather",
        grid_spec=pltpu.PrefetchScalarGridSpec(
            num_scalar_prefetch=1, grid=(NCHIP, ni),
            in_specs=[pl.BlockSpec((tm, D), lambda jj, i, s: (i, 0)), row, row, _ANY, _ANY, _ANY],
            out_specs=[pl.BlockSpec((tm, NB), lambda jj, i, s: (i, jnp.bitwise_xor(s[0], jj))),
                       pl.BlockSpec((tm, D), lambda jj, i, s: (jnp.where(jj == 0, i, ni - 1), 0)),
                       _ANY, _ANY, _ANY],
            scratch_shapes=[pltpu.VMEM((2,) + wi.shape, BF16), pltpu.VMEM(wo.shape, BF16), pltpu.VMEM(wp.shape, BF16),
                            pltpu.VMEM((tm, D), BF16),
                            pltpu.SemaphoreType.DMA((18,)), pltpu.SemaphoreType.DMA((18,)),
                            pltpu.SemaphoreType.DMA((3,)), pltpu.SemaphoreType.DMA((3,))]),
        out_shape=[S((t, D_IN), BF16), S((t, D), BF16)] + [S((NCHIP,) + s.shape, s.dtype) for s in shards],
        compiler_params=_cp(("arbitrary", "arbitrary"), 48),
    )(chip_id, x, scale1p, shift, wi, wo, wp)


def _pair_exchange(g_in, g_out, g_pw):
    def body(gi, go, gp, li, lo, lp, send_sems, recv_sems):
        x, y, c = _me()
        sibling = (x, y, 1 - c)
        copies = [(gi.at[1 - c], li)]
        for j in range(NCHIP):
            copies.append((go.at[2 * j + 1 - c], lo.at[j]))
        for j in range(NCHIP):
            copies.append((gp.at[2 * j + 1 - c], lp.at[j]))
        cps = [pltpu.make_async_remote_copy(src_ref=s_, dst_ref=d_, send_sem=send_sems.at[k], recv_sem=recv_sems.at[k],
                                            device_id=sibling, device_id_type=MESH) for k, (s_, d_) in enumerate(copies)]
        for cp in cps:
            cp.start()
        for cp in cps:
            cp.wait_recv()
        for cp in cps:
            cp.wait_send()

    n = 1 + 2 * NCHIP
    return pl.pallas_call(
        body, name="grad_pair_exchange",
        out_shape=[S(g_in.shape[1:], g_in.dtype), S((NCHIP,) + g_out.shape[1:], g_out.dtype),
                   S((NCHIP,) + g_pw.shape[1:], g_pw.dtype)],
        in_specs=[_ANY] * 3, out_specs=[_ANY] * 3,
        scratch_shapes=[pltpu.SemaphoreType.DMA((n,)), pltpu.SemaphoreType.DMA((n,))],
    )(g_in, g_out, g_pw)


def _chip_exchange_copies(pi, po, pp, li, lo, lp, send_sems, recv_sems):
    x, y, c = _me()
    chips = [(1 - x, y), (x, 1 - y), (1 - x, 1 - y)]
    cps = []
    for j, chip in enumerate(chips):
        q = 2 * chip[0] + chip[1]
        pairs = ((pi.at[:, pl.ds(q * NB, NB)], li.at[j]), (po.at[q], lo.at[j]), (pp.at[q], lp.at[j]))
        for w, (src, dst) in enumerate(pairs):
            k = 3 * j + w
            cps.append(pltpu.make_async_remote_copy(src_ref=src, dst_ref=dst, send_sem=send_sems.at[k],
                                                    recv_sem=recv_sems.at[k], device_id=(*chip, c), device_id_type=MESH))
    return cps


def _pair_share(f_in, f_out, f_pw):
    bufs = (f_in, f_out, f_pw)

    def body(ai, ao, ap, fi, fo, fp, send_sems, recv_sems):
        x, y, c = _me()
        sibling = (x, y, 1 - c)

        def copy(w, ref, hc):
            return pltpu.make_async_remote_copy(src_ref=ref.at[hc], dst_ref=ref.at[hc], send_sem=send_sems.at[w],
                                                recv_sem=recv_sems.at[w], device_id=sibling, device_id_type=MESH)

        cps = [copy(w, ref, c) for w, ref in enumerate((fi, fo, fp))]
        for cp in cps:
            cp.start()
        for w, ref in enumerate((fi, fo, fp)):
            copy(w, ref, 1 - c).wait_recv()
        for cp in cps:
            cp.wait_send()

    return pl.pallas_call(
        body, name="grad_pair_share",
        out_shape=[S(b.shape, b.dtype) for b in bufs],
        in_specs=[_ANY] * 3, out_specs=[_ANY] * 3, input_output_aliases={0: 0, 1: 1, 2: 2},
        scratch_shapes=[pltpu.SemaphoreType.DMA((3,)), pltpu.SemaphoreType.DMA((3,))],
    )(*bufs)


def _pair_sum(g, l, sel, name):
    n, r, ccols = l.shape
    tr = min(256 if ccols <= D else 128, r)

    def body(sel_ref, g_ref, l_ref, o32_ref, o16_ref):
        v = g_ref[...] + l_ref[...]
        o32_ref[...] = v
        o16_ref[...] = v.astype(BF16)

    spec_l = pl.BlockSpec((None, tr, ccols), lambda j, i, s: (j, i, 0))
    return pl.pallas_call(
        body, name=name,
        grid_spec=pltpu.PrefetchScalarGridSpec(
            num_scalar_prefetch=1, grid=(n, r // tr),
            in_specs=[pl.BlockSpec((None, tr, ccols), lambda j, i, s: (2 * j + s[0], i, 0)), spec_l],
            out_specs=[spec_l, spec_l]),
        out_shape=[S(l.shape, F32), S(l.shape, BF16)],
        compiler_params=_cp(("parallel", "parallel"), 48),
    )(sel, g, l)


def _chip_sum(own, recv, sel, name):
    r, ccols = own.shape
    tr = min(256, r)

    def body(sel_ref, o_ref, r_ref, out_ref):
        v = o_ref[...]
        for j in range(3):
            v = v + r_ref[j].astype(F32)
        out_ref[...] = v

    return pl.pallas_call(
        body, name=name,
        grid_spec=pltpu.PrefetchScalarGridSpec(
            num_scalar_prefetch=1, grid=(r // tr,),
            in_specs=[pl.BlockSpec((tr, ccols), lambda i, s: (i, 0)), pl.BlockSpec((3, tr, ccols), lambda i, s: (0, i, 0))],
            out_specs=pl.BlockSpec((None, tr, ccols), lambda i, s: (s[0], i, 0))),
        out_shape=S((2, r, ccols), F32),
        compiler_params=_cp(("parallel",), 48),
    )(sel, own, recv)


def _ada_fwd(c_all, w_ada, b_ada_p):
    n = w_ada.shape[1]
    tn = 512

    def body(c_ref, w_ref, b_ref, o_ref):
        cv = c_ref[...]
        ca = cv * _sig(cv)
        o_ref[...] = jnp.dot(ca, w_ref[...], preferred_element_type=F32) + b_ref[...]

    return pl.pallas_call(
        body, name="ada_fwd", grid=(n // tn,),
        in_specs=[pl.BlockSpec((8, D), lambda j: (0, 0)), pl.BlockSpec((D, tn), lambda j: (0, j)),
                  pl.BlockSpec((1, tn), lambda j: (0, j))],
        out_specs=pl.BlockSpec((8, tn), lambda j: (0, j)),
        out_shape=S((8, n), F32),
        compiler_params=_cp(("parallel",), 32),
    )(c_all, w_ada, b_ada_p)


def _build_bias(rel_ref, bk_ref, bias_ref):
    bk = bk_ref[...]
    kj = lax.broadcasted_iota(jnp.int32, (2 * BLK, BLK), 0)
    for hd in range(NQ):
        acc = jnp.full((2 * BLK, BLK), NEG, F32)
        for b in range(N_BUCKETS):
            acc = jnp.where(bk == b, rel_ref[b, hd], acc)
        lanes = slice((hd % GRP) * BLK, (hd % GRP + 1) * BLK)
        bias_ref[1, hd // GRP, :, lanes] = acc
        bias_ref[0, hd // GRP, :, lanes] = jnp.where(kj < BLK, NEG, acc)


def _group_rows(ref, h):
    return jnp.concatenate([ref[:, (GRP * h + g) * HD:(GRP * h + g + 1) * HD] for g in range(GRP)], axis=0)


def _sink_row(sink_ref, h):
    return jnp.concatenate([jnp.full((1, BLK), sink_ref[0, GRP * h + g], F32) for g in range(GRP)], axis=1)


def _attn_fwd(proj, rel_bias, sinks, bucket):
    t = proj.shape[0]

    def body(rel_ref, sink_ref, bk_ref, q_ref, kvc_ref, kvp_ref, glo_ref, ghi_ref, ya_ref, o_ref, lse_ref, bias_ref):
        n = pl.program_id(0)

        @pl.when(n == 0)
        def _():
            _build_bias(rel_ref, bk_ref, bias_ref)

        tbl = jnp.where(n == 0, 0, 1)
        kv = jnp.concatenate([kvp_ref[...], kvc_ref[...]], axis=0)
        for h in range(NKV):
            k_h = kv[:, h * HD:(h + 1) * HD]
            v_h = kv[:, NKV * HD + h * HD:NKV * HD + (h + 1) * HD]
            q4 = _group_rows(q_ref, h) * SCALE
            s = lax.dot_general(k_h, q4, (((1,), (1,)), ((), ())), preferred_element_type=F32) + bias_ref[tbl, h]
            sink = _sink_row(sink_ref, h)
            m = jnp.maximum(jnp.max(s, axis=0, keepdims=True), sink)
            p = jnp.exp(s - m)
            l = jnp.sum(p, axis=0, keepdims=True) + jnp.exp(sink - m)
            pn = (p * (1.0 / l)).astype(BF16)
            o4 = lax.dot_general(pn, v_h, (((0,), (0,)), ((), ())), preferred_element_type=F32)
            lse_ref[h:h + 1, :] = m + jnp.log(l)
            for g in range(GRP):
                hd = GRP * h + g
                cols = slice(hd * HD, (hd + 1) * HD)
                o = o4[g * BLK:(g + 1) * BLK]
                g_ref = glo_ref if hd < NQ // 2 else ghi_ref
                gc = slice((hd % (NQ // 2)) * HD, (hd % (NQ // 2) + 1) * HD)
                gt = g_ref[:, gc].astype(F32)
                o_ref[:, cols] = o.astype(BF16)
                ya_ref[:, cols] = (o * (gt * _sig(gt))).astype(BF16)

    blk = lambda w, cidx: pl.BlockSpec((BLK, w), lambda n: (n, cidx))
    return pl.pallas_call(
        body, name="attn_fwd", grid=(t // BLK,),
        in_specs=[_SMEM, _SMEM, pl.BlockSpec((2 * BLK, BLK), lambda n: (0, 0)),
                  blk(DA, 0), blk(512, 2), pl.BlockSpec((BLK, 512), lambda n: (jnp.maximum(n - 1, 0), 2)),
                  blk(512, 3), blk(512, 4)],
        out_specs=[blk(DA, 0), blk(DA, 0), pl.BlockSpec((None, NKV, GRP * BLK), lambda n: (n, 0, 0))],
        out_shape=[S((t, DA), BF16), S((t, DA), BF16), S((t // BLK, NKV, GRP * BLK), F32)],
        scratch_shapes=[pltpu.VMEM((2, NKV, 2 * BLK, GRP * BLK), F32)],
        compiler_params=_cp(("arbitrary",), 32),
    )(rel_bias, sinks, bucket, proj, proj, proj, proj, proj)


def _conv_rows(t):
    return min(256, t)


def _glu_into(u_scr, i, a_refs, b_refs, ah_refs, bh_refs, tc):
    for cc in range(DA // 128):
        half, lc = cc // 4, slice((cc % 4) * 128, (cc % 4 + 1) * 128)
        uh = ah_refs[half][:, lc].astype(F32) * _sig(bh_refs[half][:, lc].astype(F32))
        u_scr[cc, 0:HALO, :] = jnp.where(i == 0, 0.0, uh)
        u_scr[cc, HALO:HALO + tc, :] = a_refs[half][:, lc].astype(F32) * _sig(b_refs[half][:, lc].astype(F32))


def _conv_fwd(proj, cwb, conv_b, ln_g, ln_b, wpw, b_pw):
    t = proj.shape[0]
    tc = _conv_rows(t)
    rc = min(128, tc)

    def body(alo, ahi, blo, bhi, alo_h, ahi_h, blo_h, bhi_h, glo, ghi, cw_ref, cb_ref, lg_ref, lb_ref, wpw_ref, bpw_ref,
             uc_ref, pw_ref, yc_ref, u_scr):
        i = pl.program_id(0)
        _glu_into(u_scr, i, (alo, ahi), (blo, bhi), (alo_h, ahi_h), (blo_h, bhi_h), tc)

        def rows(r, carry):
            r0 = pl.multiple_of(r * rc, rc)
            for cc in range(DA // 128):
                lanes = pl.ds(cc * 128, 128)
                acc = jnp.zeros((rc // 8, 8, 128), F32)
                for k in range(CW):
                    u = u_scr[cc, pl.ds(r0 + (HALO - CW + 1) + k, rc), :].reshape(rc // 8, 8, 128)
                    acc = acc + u * cw_ref[k, :, lanes]
                uc_ref[pl.ds(r0, rc), lanes] = acc.reshape(rc, 128) + cb_ref[:, lanes]
            return carry

        lax.fori_loop(0, tc // rc, rows, 0)

        uc = uc_ref[...]
        mu = jnp.mean(uc, axis=-1, keepdims=True)
        xc = uc - mu
        rstd = lax.rsqrt(jnp.mean(xc * xc, axis=-1, keepdims=True) + LN_EPS)
        ln = xc * rstd * lg_ref[...] + lb_ref[...]
        sw = (ln * _sig(ln)).astype(BF16)
        pw = jnp.dot(sw, wpw_ref[...], preferred_element_type=F32) + bpw_ref[...]
        pw_ref[...] = pw.astype(BF16)
        gt = jnp.concatenate([glo[...], ghi[...]], axis=1).astype(F32)
        yc_ref[...] = (pw * (gt * _sig(gt))).astype(BF16)

    nh = tc // HALO
    cur = lambda cidx: pl.BlockSpec((tc, 512), lambda i: (i, cidx))
    halo = lambda cidx: pl.BlockSpec((HALO, 512), lambda i: (jnp.maximum(i * nh - 1, 0), cidx))
    row = pl.BlockSpec((1, DA), lambda i: (0, 0))
    full = pl.BlockSpec((tc, DA), lambda i: (i, 0))
    return pl.pallas_call(
        body, name="conv_fwd", grid=(t // tc,),
        in_specs=[cur(5), cur(6), cur(7), cur(8), halo(5), halo(6), halo(7), halo(8), cur(9), cur(10),
                  pl.BlockSpec((CW, 8, DA), lambda i: (0, 0, 0)), row, row, row,
                  pl.BlockSpec((DA, DA), lambda i: (0, 0)), row],
        out_specs=[full, full, full],
        out_shape=[S((t, DA), F32), S((t, DA), BF16), S((t, DA), BF16)],
        scratch_shapes=[pltpu.VMEM((DA // 128, HALO + tc, 128), F32)],
        compiler_params=_cp(("arbitrary",), 48),
    )(*([proj] * 10), cwb, conv_b, ln_g, ln_b, wpw, b_pw)


def _outproj_loss(ya, yc, w_out, x, target, gate, ln_g, ln_b):
    t = x.shape[0]
    tm = min(256, t)

    def body(ya_ref, yc_ref, w_ref, x_ref, t_ref, g_ref, lg_ref, lb_ref, gx_ref, dy_ref, st_ref):
        @pl.when(pl.program_id(0) == 0)
        def _():
            st_ref[...] = jnp.zeros_like(st_ref)

        y = jnp.dot(ya_ref[...], w_ref[0:DA, :], preferred_element_type=F32)
        y = y + jnp.dot(yc_ref[...], w_ref[DA:2 * DA, :], preferred_element_type=F32)
        gate_v = g_ref[...]
        z = ALPHA * x_ref[...] + gate_v * y
        mu = jnp.mean(z, axis=-1, keepdims=True)
        zc = z - mu
        rstd = lax.rsqrt(jnp.mean(zc * zc, axis=-1, keepdims=True) + LN_EPS)
        zh = zc * rstd
        diff = zh * lg_ref[...] + lb_ref[...] - t_ref[...]
        dout = diff * (1.0 / D)
        dzh = dout * lg_ref[...]
        m1 = jnp.mean(dzh, axis=-1, keepdims=True)
        m2 = jnp.mean(dzh * zh, axis=-1, keepdims=True)
        dz = rstd * (dzh - m1 - zh * m2)
        gx_ref[...] = ALPHA * dz
        dy_ref[...] = (dz * gate_v).astype(BF16)
        st_ref[0:1, :] += jnp.sum(dout * zh, axis=0, keepdims=True)
        st_ref[1:2, :] += jnp.sum(dout, axis=0, keepdims=True)
        st_ref[2:3, :] += jnp.sum(dz * y, axis=0, keepdims=True)
        st_ref[3:4, :] += jnp.sum(diff * diff, axis=0, keepdims=True) * (0.5 / D)

        @pl.when(pl.program_id(0) == t // tm - 1)
        def _():
            st_ref[3:4, :] = jnp.broadcast_to(jnp.sum(st_ref[3:4, :], axis=-1, keepdims=True), (1, D))

    row = pl.BlockSpec((1, D), lambda i: (0, 0))
    half = pl.BlockSpec((tm, DA), lambda i: (i, 0))
    full = pl.BlockSpec((tm, D), lambda i: (i, 0))
    return pl.pallas_call(
        body, name="outproj_loss", grid=(t // tm,),
        in_specs=[half, half, pl.BlockSpec((D, D), lambda i: (0, 0)), full, full, row, row, row],
        out_specs=[full, full, pl.BlockSpec((8, D), lambda i: (0, 0))],
        out_shape=[S((t, D), F32), S((t, D), BF16), S((8, D), F32)],
        compiler_params=_cp(("arbitrary",), 56),
    )(ya, yc, w_out, x, target, gate, ln_g, ln_b)


COL_DQ, COL_DKV, COL_DGA, COL_DGLU_A, COL_DGLU_B, COL_DGC = 0, 1024, 1536, 2560, 3584, 4608


def _tile_copy(dst, stage, sems, slot, row0, col0):
    rows, width = stage.shape[1:]
    return pltpu.make_async_copy(stage.at[slot], dst.at[pl.ds(row0, rows), pl.ds(col0, width)], sems.at[slot])


def _stage_slot(dst, stage, sems, step):
    slot = step % 2

    @pl.when(step >= 2)
    def _():
        _tile_copy(dst, stage, sems, slot, 0, 0).wait()

    return slot


def _tile_send(dst, stage, sems, step, nsteps, col0):
    rows = stage.shape[1]
    slot = step % 2
    _tile_copy(dst, stage, sems, slot, pl.multiple_of(step * rows, rows), col0).start()

    @pl.when(step == nsteps - 1)
    def _():
        _tile_copy(dst, stage, sems, slot, 0, col0).wait()
        if nsteps > 1:
            _tile_copy(dst, stage, sems, 1 - slot, 0, col0).wait()


def _dycat_gates(dy, w_out, attn_o, pw, proj, dproj):
    t = dy.shape[0]
    tm = min(256, t)
    nt = t // tm

    def body(dy_ref, w_ref, o_ref, pw_ref, galo, gahi, gclo, gchi, dp_in, dao_ref, dpw_ref, dp_ref,
             st_a, st_c, sem_a, sem_c):
        i = pl.program_id(0)
        sl_a = _stage_slot(dp_ref, st_a, sem_a, i)
        sl_c = _stage_slot(dp_ref, st_c, sem_c, i)
        dyc = lax.dot_general(dy_ref[...], w_ref[...], (((1,), (1,)), ((), ())), preferred_element_type=F32)
        da, dc = dyc[:, 0:DA], dyc[:, DA:2 * DA]
        ga = jnp.concatenate([galo[...], gahi[...]], axis=1).astype(F32)
        sa = _sig(ga)
        dao_ref[...] = (da * (ga * sa)).astype(BF16)
        st_a[sl_a] = (da * o_ref[...].astype(F32) * _dsilu(ga, sa)).astype(BF16)
        _tile_send(dp_ref, st_a, sem_a, i, nt, COL_DGA)
        gc = jnp.concatenate([gclo[...], gchi[...]], axis=1).astype(F32)
        sc = _sig(gc)
        dpw_ref[...] = (dc * (gc * sc)).astype(BF16)
        st_c[sl_c] = (dc * pw_ref[...].astype(F32) * _dsilu(gc, sc)).astype(BF16)
        _tile_send(dp_ref, st_c, sem_c, i, nt, COL_DGC)

    half = pl.BlockSpec((tm, DA), lambda i: (i, 0))
    cur = lambda cidx: pl.BlockSpec((tm, 512), lambda i: (i, cidx))
    return pl.pallas_call(
        body, name="dycat_gates", grid=(nt,),
        in_specs=[pl.BlockSpec((tm, D), lambda i: (i, 0)), pl.BlockSpec((D, D), lambda i: (0, 0)), half, half,
                  cur(3), cur(4), cur(9), cur(10), _ANY],
        out_specs=[half, half, _ANY],
        out_shape=[S((t, DA), BF16), S((t, DA), BF16), S(dproj.shape, dproj.dtype)],
        input_output_aliases={8: 2},
        scratch_shapes=[pltpu.VMEM((2, tm, DA), BF16), pltpu.VMEM((2, tm, DA), BF16),
                        pltpu.SemaphoreType.DMA((2,)), pltpu.SemaphoreType.DMA((2,))],
        compiler_params=_cp(("arbitrary",), 48),
    )(dy, w_out, attn_o, pw, proj, proj, proj, proj, dproj)


def _conv_bwd_ln(dpw, wpw, uc, ln_g, ln_b):
    t = dpw.shape[0]
    tc = min(256, t)

    def body(dpw_ref, w_ref, uc_ref, lg_ref, lb_ref, duc_ref, sw_ref, st_ref):
        @pl.when(pl.program_id(0) == 0)
        def _():
            st_ref[...] = jnp.zeros_like(st_ref)

        dpw_v = dpw_ref[...]
        ds = lax.dot_general(dpw_v, w_ref[...], (((1,), (1,)), ((), ())), preferred_element_type=F32)
        uc = uc_ref[...]
        mu = jnp.mean(uc, axis=-1, keepdims=True)
        xc = uc - mu
        rstd = lax.rsqrt(jnp.mean(xc * xc, axis=-1, keepdims=True) + LN_EPS)
        uh = xc * rstd
        ln = uh * lg_ref[...] + lb_ref[...]
        sg = _sig(ln)
        sw_ref[...] = (ln * sg).astype(BF16)
        dln = ds * _dsilu(ln, sg)
        dxh = dln * lg_ref[...]
        m1 = jnp.mean(dxh, axis=-1, keepdims=True)
        m2 = jnp.mean(dxh * uh, axis=-1, keepdims=True)
        duc = rstd * (dxh - m1 - uh * m2)
        duc_ref[...] = duc
        st_ref[0:1, :] += jnp.sum(dln * uh, axis=0, keepdims=True)
        st_ref[1:2, :] += jnp.sum(dln, axis=0, keepdims=True)
        st_ref[2:3, :] += jnp.sum(duc, axis=0, keepdims=True)
        st_ref[3:4, :] += jnp.sum(dpw_v.astype(F32), axis=0, keepdims=True)

    row = pl.BlockSpec((1, DA), lambda i: (0, 0))
    full = pl.BlockSpec((tc, DA), lambda i: (i, 0))
    return pl.pallas_call(
        body, name="conv_bwd_ln", grid=(t // tc,),
        in_specs=[full, pl.BlockSpec((DA, DA), lambda i: (0, 0)), full, row, row],
        out_specs=[full, full, pl.BlockSpec((8, DA), lambda i: (0, 0))],
        out_shape=[S((t, DA), F32), S((t, DA), BF16), S((8, DA), F32)],
        compiler_params=_cp(("arbitrary",), 48),
    )(dpw, wpw, uc, ln_g, ln_b)


def _conv_bwd_dw(duc, proj, cwb, dproj):
    t = duc.shape[0]
    tc = _conv_rows(t)
    rc = min(128, tc)
    nt = t // tc
    off = HALO - CW + 1

    def body(dcur, dnext, alo, ahi, blo, bhi, alo_h, ahi_h, blo_h, bhi_h, cw_ref, dp_in, gw_ref, dp_ref,
             u_scr, d_scr, g_scr, st_ab, sem_ab):
        i = pl.program_id(0)
        slot = _stage_slot(dp_ref, st_ab, sem_ab, i)

        @pl.when(i == 0)
        def _():
            g_scr[...] = jnp.zeros_like(g_scr)

        _glu_into(u_scr, i, (alo, ahi), (blo, bhi), (alo_h, ahi_h), (blo_h, bhi_h), tc)
        for cc in range(DA // 128):
            d_scr[cc, 0:tc, :] = dcur[:, cc * 128:(cc + 1) * 128]
            d_scr[cc, tc:tc + HALO, :] = jnp.where(i == nt - 1, 0.0, dnext[:, cc * 128:(cc + 1) * 128])

        def rows(r, carry):
            r0 = pl.multiple_of(r * rc, rc)
            for cc in range(DA // 128):
                lanes = pl.ds(cc * 128, 128)
                acc = jnp.zeros((rc // 8, 8, 128), F32)
                for j in range(CW):
                    dv = d_scr[cc, pl.ds(r0 + j, rc), :].reshape(rc // 8, 8, 128)
                    acc = acc + dv * cw_ref[CW - 1 - j, :, lanes]
                du = acc.reshape(rc, 128)
                d0 = d_scr[cc, pl.ds(r0, rc), :].reshape(rc // 8, 8, 128)
                for k in range(CW):
                    u = u_scr[cc, pl.ds(r0 + off + k, rc), :].reshape(rc // 8, 8, 128)
                    g_scr[k, :, lanes] += jnp.sum(d0 * u, axis=0)
                a_ref, b_ref = (alo, blo) if cc < 4 else (ahi, bhi)
                lc = pl.ds((cc % 4) * 128, 128)
                av = a_ref[pl.ds(r0, rc), lc].astype(F32)
                sb = _sig(b_ref[pl.ds(r0, rc), lc].astype(F32))
                st_ab[slot, pl.ds(r0, rc), lanes] = (du * sb).astype(BF16)
                st_ab[slot, pl.ds(r0, rc), pl.ds(DA + cc * 128, 128)] = (du * av * sb * (1.0 - sb)).astype(BF16)
            return carry

        lax.fori_loop(0, tc // rc, rows, 0)
        _tile_send(dp_ref, st_ab, sem_ab, i, nt, COL_DGLU_A)

        @pl.when(i == nt - 1)
        def _():
            gw_ref[0:CW, :] = jnp.sum(g_scr[...], axis=1)
            gw_ref[CW:32, :] = jnp.zeros((32 - CW, DA), F32)

    nh = tc // HALO
    nhb = t // HALO
    cur = lambda cidx: pl.BlockSpec((tc, 512), lambda i: (i, cidx))
    halo = lambda cidx: pl.BlockSpec((HALO, 512), lambda i: (jnp.maximum(i * nh - 1, 0), cidx))
    full = pl.BlockSpec((tc, DA), lambda i: (i, 0))
    return pl.pallas_call(
        body, name="conv_bwd_dw", grid=(nt,),
        in_specs=[full, pl.BlockSpec((HALO, DA), lambda i: (jnp.minimum((i + 1) * nh, nhb - 1), 0)),
                  cur(5), cur(6), cur(7), cur(8), halo(5), halo(6), halo(7), halo(8),
                  pl.BlockSpec((CW, 8, DA), lambda i: (0, 0, 0)), _ANY],
        out_specs=[pl.BlockSpec((32, DA), lambda i: (0, 0)), _ANY],
        out_shape=[S((32, DA), F32), S(dproj.shape, dproj.dtype)],
        input_output_aliases={11: 1},
        scratch_shapes=[pltpu.VMEM((DA // 128, HALO + tc, 128), F32), pltpu.VMEM((DA // 128, tc + HALO, 128), F32),
                        pltpu.VMEM((CW, 8, DA), F32), pltpu.VMEM((2, tc, 2 * DA), BF16), pltpu.SemaphoreType.DMA((2,))],
        compiler_params=_cp(("arbitrary",), 48),
    )(duc, duc, *([proj] * 8), cwb, dproj)


def _attn_bwd(proj, dao, lse, rel_bias, sinks, bucket, dproj):
    t = proj.shape[0]

    def body(rel_ref, sink_ref, bk_ref, q_ref, kvc_ref, kvp_ref, do_ref, lse_ref, dp_in,
             dcur_ref, dprev_ref, dsacc_ref, dsk_ref, dp_ref, bias_ref, st_q, sem_q):
        n = pl.program_id(0)
        slot = _stage_slot(dp_ref, st_q, sem_q, n)

        @pl.when(n == 0)
        def _():
            _build_bias(rel_ref, bk_ref, bias_ref)
            dsacc_ref[...] = jnp.zeros_like(dsacc_ref)
            dsk_ref[...] = jnp.zeros_like(dsk_ref)

        tbl = jnp.where(n == 0, 0, 1)
        kv = jnp.concatenate([kvp_ref[...], kvc_ref[...]], axis=0)
        for h in range(NKV):
            k_h = kv[:, h * HD:(h + 1) * HD]
            v_h = kv[:, NKV * HD + h * HD:NKV * HD + (h + 1) * HD]
            q4 = _group_rows(q_ref, h) * SCALE
            do4 = _group_rows(do_ref, h)
            lse_h = lse_ref[h:h + 1, :]
            s = lax.dot_general(k_h, q4, (((1,), (1,)), ((), ())), preferred_element_type=F32)
            p = jnp.exp(s + bias_ref[tbl, h] - lse_h)
            dp = lax.dot_general(v_h, do4, (((1,), (1,)), ((), ())), preferred_element_type=F32)
            delta = jnp.sum(p * dp, axis=0, keepdims=True)
            ds = p * (dp - delta)
            dsacc_ref[h] += ds
            dsk_ref[h:h + 1, :] += -jnp.exp(_sink_row(sink_ref, h) - lse_h) * delta
            dsb = ds.astype(BF16)
            dq4 = lax.dot_general(dsb, k_h, (((0,), (0,)), ((), ())), preferred_element_type=F32) * SCALE
            for g in range(GRP):
                hd = GRP * h + g
                st_q[slot, :, hd * HD:(hd + 1) * HD] = dq4[g * BLK:(g + 1) * BLK].astype(BF16)
            dk = jnp.dot(dsb, q4, preferred_element_type=F32)
            dv = jnp.dot(p.astype(BF16), do4, preferred_element_type=F32)
            kc = slice(h * HD, (h + 1) * HD)
            vc = slice(NKV * HD + h * HD, NKV * HD + (h + 1) * HD)
            dprev_ref[:, kc] = dk[0:BLK]
            dcur_ref[:, kc] = dk[BLK:2 * BLK]
            dprev_ref[:, vc] = dv[0:BLK]
            dcur_ref[:, vc] = dv[BLK:2 * BLK]

        _tile_send(dp_ref, st_q, sem_q, n, t // BLK, COL_DQ)

        @pl.when(n == t // BLK - 1)
        def _():
            for g in range(GRP):
                lanes = slice(g * BLK, (g + 1) * BLK)
                dsk_ref[:, lanes] = jnp.broadcast_to(jnp.sum(dsk_ref[:, lanes], axis=-1, keepdims=True), (NKV, BLK))

    blk = lambda w, cidx: pl.BlockSpec((BLK, w), lambda n: (n, cidx))
    return pl.pallas_call(
        body, name="attn_bwd", grid=(t // BLK,),
        in_specs=[_SMEM, _SMEM, pl.BlockSpec((2 * BLK, BLK), lambda n: (0, 0)),
                  blk(DA, 0), blk(512, 2), pl.BlockSpec((BLK, 512), lambda n: (jnp.maximum(n - 1, 0), 2)),
                  blk(DA, 0), pl.BlockSpec((None, NKV, GRP * BLK), lambda n: (n, 0, 0)), _ANY],
        out_specs=[blk(512, 0), blk(512, 0),
                   pl.BlockSpec((NKV, 2 * BLK, GRP * BLK), lambda n: (0, 0, 0)), pl.BlockSpec((NKV, GRP * BLK), lambda n: (0, 0)),
                   _ANY],
        out_shape=[S((t, 512), F32), S((t, 512), F32), S((NKV, 2 * BLK, GRP * BLK), F32),
                   S((NKV, GRP * BLK), F32), S(dproj.shape, dproj.dtype)],
        input_output_aliases={8: 4},
        scratch_shapes=[pltpu.VMEM((2, NKV, 2 * BLK, GRP * BLK), F32), pltpu.VMEM((2, BLK, DA), BF16),
                        pltpu.SemaphoreType.DMA((2,))],
        compiler_params=_cp(("arbitrary",), 40),
    )(rel_bias, sinks, bucket, proj, proj, proj, dao, lse, dproj)


def _dkv_fix(dcur, dprev, dproj):
    t = dcur.shape[0]
    tb = min(1024, t)
    nt = t // tb
    per = tb // BLK

    def body(c_ref, p_ref, pn_ref, dp_in, dp_ref, st_kv, sem_kv):
        i = pl.program_id(0)
        slot = _stage_slot(dp_ref, st_kv, sem_kv, i)
        if tb > BLK:
            st_kv[slot, 0:tb - BLK, :] = (c_ref[0:tb - BLK, :] + p_ref[BLK:tb, :]).astype(BF16)
        nxt = jnp.where(i == nt - 1, 0.0, pn_ref[...])
        st_kv[slot, tb - BLK:tb, :] = (c_ref[tb - BLK:tb, :] + nxt).astype(BF16)
        _tile_send(dp_ref, st_kv, sem_kv, i, nt, COL_DKV)

    tile = pl.BlockSpec((tb, 512), lambda i: (i, 0))
    return pl.pallas_call(
        body, name="dkv_fix", grid=(nt,),
        in_specs=[tile, tile, pl.BlockSpec((BLK, 512), lambda i: (jnp.minimum((i + 1) * per, t // BLK - 1), 0)), _ANY],
        out_specs=_ANY,
        out_shape=S(dproj.shape, dproj.dtype),
        input_output_aliases={3: 0},
        scratch_shapes=[pltpu.VMEM((2, tb, 512), BF16), pltpu.SemaphoreType.DMA((2,))],
        compiler_params=_cp(("arbitrary",), 32),
    )(dcur, dprev, dprev, dproj)


def _bias_grad(dsacc, bucket):
    def body(ds_ref, bk_ref, o_ref, row_scr):
        bk = bk_ref[...]

        def group(h, carry):
            for g in range(GRP):
                dsv = ds_ref[h, :, g * BLK:(g + 1) * BLK]
                for b in range(N_BUCKETS):
                    row_scr[GRP * h + g, b:b + 1, :] = jnp.sum(jnp.where(bk == b, dsv, 0.0), axis=0, keepdims=True)
            return carry

        lax.fori_loop(0, NKV, group, 0)
        for hd in range(NQ):
            o_ref[hd] = jnp.sum(row_scr[hd], axis=-1, keepdims=True)

    return pl.pallas_call(
        body, name="bias_grad", out_shape=S((NQ, N_BUCKETS, 1), F32),
        in_specs=[_VMEM, _VMEM], out_specs=_VMEM,
        scratch_shapes=[pltpu.VMEM((NQ, N_BUCKETS, BLK), F32)],
        compiler_params=_cp(None, 32),
    )(dsacc, bucket)


def _dh_gradx(dproj, wg, gx0, x, scale1p, p_in, p_out, p_pw):
    t = x.shape[0]
    tm = min(512, t)
    ni = t // tm
    bn = D // 2
    nn = D // bn

    def body(dp_ref, w_ref, gx_ref, x_ref, sc_ref, pi, po, pp, out_ref, st_ref, li, lo, lp, send_sems, recv_sems):
        n, i = pl.program_id(0), pl.program_id(1)

        @pl.when((n == 0) & (i == 0))
        def _():
            for cp in _chip_exchange_copies(pi, po, pp, li, lo, lp, send_sems, recv_sems):
                cp.start()

        @pl.when(i == 0)
        def _():
            st_ref[...] = jnp.zeros_like(st_ref)

        dh = None
        for j in range(NCHIP):
            part = lax.dot_general(dp_ref[:, j * NB:(j + 1) * NB], w_ref[j], (((1,), (1,)), ((), ())),
                                   preferred_element_type=F32)
            dh = part if dh is None else dh + part
        out_ref[...] = gx_ref[...] + dh * sc_ref[...]
        st_ref[0:1, :] += jnp.sum(dh, axis=0, keepdims=True)
        st_ref[1:2, :] += jnp.sum(dh * x_ref[...], axis=0, keepdims=True)

        @pl.when((n == nn - 1) & (i == ni - 1))
        def _():
            cps = _chip_exchange_copies(pi, po, pp, li, lo, lp, send_sems, recv_sems)
            for cp in cps:
                cp.wait_recv()
            for cp in cps:
                cp.wait_send()

    tile = pl.BlockSpec((tm, bn), lambda n, i: (i, n))
    return pl.pallas_call(
        body, name="dh_gradx", grid=(nn, ni),
        in_specs=[pl.BlockSpec((tm, D_IN), lambda n, i: (i, 0)), pl.BlockSpec((NCHIP, bn, NB), lambda n, i: (0, n, 0)),
                  tile, tile, pl.BlockSpec((1, bn), lambda n, i: (0, n)), _ANY, _ANY, _ANY],
        out_specs=[tile, pl.BlockSpec((8, bn), lambda n, i: (0, n)), _ANY, _ANY, _ANY],
        out_shape=[S((t, D), F32), S((8, D), F32), S((3, p_in.shape[0], NB), p_in.dtype),
                   S((3,) + p_out.shape[1:], p_out.dtype), S((3,) + p_pw.shape[1:], p_pw.dtype)],
        scratch_shapes=[pltpu.SemaphoreType.DMA((9,)), pltpu.SemaphoreType.DMA((9,))],
        compiler_params=_cp(("arbitrary", "arbitrary"), 60),
    )(dproj, wg, gx0, x, scale1p, p_in, p_out, p_pw)


def _atb(a_parts, b, bn, name):
    bm = DA
    t = b.shape[0]
    n = b.shape[1]
    tk = min(2048, t)
    nk = t // tk
    na = len(a_parts)

    def body(*refs):
        a_refs, b_ref, o_ref = refs[:na], refs[na], refs[na + 1]
        mi, k = pl.program_id(0), pl.program_id(2)
        for q in range(na):
            @pl.when(mi == q)
            def _(q=q):
                part = lax.dot_general(a_refs[q][...], b_ref[...], (((0,), (0,)), ((), ())), preferred_element_type=F32)

                @pl.when(k == 0)
                def _():
                    o_ref[...] = part

                @pl.when(k > 0)
                def _():
                    o_ref[...] += part

    a_spec = lambda q, cidx: pl.BlockSpec((tk, bm), lambda mi, j, k: (jnp.where(mi == q, k, 0), cidx))
    return pl.pallas_call(
        body, name=name, grid=(na, n // bn, nk),
        in_specs=[a_spec(q, cidx) for q, (_, cidx) in enumerate(a_parts)] + [pl.BlockSpec((tk, bn), lambda mi, j, k: (k, j))],
        out_specs=pl.BlockSpec((bm, bn), lambda mi, j, k: (mi, j)),
        out_shape=S((na * bm, n), F32),
        compiler_params=_cp(("parallel", "parallel", "arbitrary"), 56),
    )(*[arr for arr, _ in a_parts], b)


def _sum8(parts):
    _, r, n = parts.shape

    def body(p_ref, o_ref):
        v = p_ref[0]
        for d in range(1, 8):
            v = v + p_ref[d]
        o_ref[...] = v

    return pl.pallas_call(body, name="sum8", out_shape=S((r, n), F32), in_specs=[_VMEM], out_specs=_VMEM,
                          compiler_params=_cp(None, 32))(parts)


def _adam_math(w, g, m, v):
    m = B1 * m + (1.0 - B1) * g
    v = B2 * v + (1.0 - B2) * (g * g)
    m_hat = m / (1.0 - B1 ** STEP)
    v_hat = v / (1.0 - B2 ** STEP)
    delta = -LR * (m_hat / (jnp.sqrt(v_hat) + EPS) + WD * w)
    return delta, m, v


def _adamw(w, g, m, v, name):
    r, n = w.shape
    tr = min(256, r)

    def body(w_ref, g_ref, m_ref, v_ref, d_ref, nm_ref, nv_ref):
        d_ref[...], nm_ref[...], nv_ref[...] = _adam_math(w_ref[...], g_ref[...], m_ref[...], v_ref[...])

    spec = pl.BlockSpec((tr, n), lambda i: (i, 0))
    return pl.pallas_call(
        body, name=name, grid=(r // tr,), in_specs=[spec] * 4, out_specs=[spec] * 3,
        out_shape=[S((r, n), F32)] * 3, compiler_params=_cp(("parallel",), 48),
    )(w, g, m, v)


def _adamw_ada(ct, dmod_p, w, m, v):
    r, n = w.shape
    tr = min(256, r)

    def body(ct_ref, dm_ref, w_ref, m_ref, v_ref, g_ref, d_ref, nm_ref, nv_ref):
        cv = ct_ref[...]
        g = jnp.dot(cv * _sig(cv), dm_ref[...], preferred_element_type=F32)
        g_ref[...] = g
        d_ref[...], nm_ref[...], nv_ref[...] = _adam_math(w_ref[...], g, m_ref[...], v_ref[...])

    spec = pl.BlockSpec((tr, n), lambda i: (i, 0))
    return pl.pallas_call(
        body, name="adamw_ada", grid=(r // tr,),
        in_specs=[pl.BlockSpec((tr, 8), lambda i: (i, 0)), pl.BlockSpec((8, n), lambda i: (0, 0)), spec, spec, spec],
        out_specs=[spec] * 4, out_shape=[S((r, n), F32)] * 4, compiler_params=_cp(("parallel",), 48),
    )(ct, dmod_p, w, m, v)


def _adamw_small(ws, gs, ms, vs):
    k = len(ws)

    def body(*refs):
        ins, outs = refs[:4 * k], refs[4 * k:]
        for i in range(k):
            d, nm, nv = _adam_math(ins[i][...], ins[k + i][...], ins[2 * k + i][...], ins[3 * k + i][...])
            outs[i][...] = d
            outs[k + i][...] = nm
            outs[2 * k + i][...] = nv

    shapes = [S(w.shape, F32) for w in ws]
    return pl.pallas_call(body, name="adamw_small", out_shape=shapes * 3, in_specs=[_VMEM] * (4 * k),
                          out_specs=[_VMEM] * (3 * k), compiler_params=_cp(None, 32))(*ws, *gs, *ms, *vs)


def _bucket_map():
    qi = jnp.arange(BLK, dtype=jnp.int32)[None, :]
    kj = jnp.arange(2 * BLK, dtype=jnp.int32)[:, None]
    dist = qi + BLK - kj
    dd = jnp.maximum(dist, 0)
    max_exact = N_BUCKETS // 2
    dfl = jnp.maximum(dd, 1).astype(F32)
    large = max_exact + (jnp.log(dfl / max_exact) / math.log(MAX_DIST / max_exact) * (N_BUCKETS - max_exact)).astype(jnp.int32)
    large = jnp.minimum(large, N_BUCKETS - 1)
    bucket = jnp.where(dd < max_exact, dd, large)
    return jnp.where((dist >= 0) & (dist < BLK), bucket, -1).astype(jnp.int32)


def _pad_rows(a, rows):
    return jnp.pad(a, ((0, rows - a.shape[0]), (0, 0)))


def kernel(x, c, w_ada, b_ada, w_in, rel_bias, sinks, conv_w, conv_b, conv_ln_g, conv_ln_b, w_pw, b_pw, w_out, ln_g, ln_b, loss_target, m_w_ada, m_b_ada, m_w_in, m_rel_bias, m_sinks, m_conv_w, m_conv_b, m_conv_ln_g, m_conv_ln_b, m_w_pw, m_b_pw, m_w_out, m_ln_g, m_ln_b, v_w_ada, v_b_ada, v_w_in, v_rel_bias, v_sinks, v_conv_w, v_conv_b, v_conv_ln_g, v_conv_ln_b, v_w_pw, v_b_pw, v_w_out, v_ln_g, v_ln_b):
    mx, my, mc = _me()
    chip = 2 * mx + my
    dev = 2 * chip + mc
    t = x.shape[1]
    x2 = x.reshape(t, D)
    tgt = loss_target.reshape(t, D)
    n_ada = w_ada.shape[2]
    cw_cols = conv_w.shape[2]

    pack0 = jnp.concatenate([_pad_rows(c, 8), _pad_rows(_pad_rows(conv_w[0], 32).reshape(-1, D), 8)], axis=0)
    g0 = _allgather8(pack0, "gather_c_convw").reshape(8, 16, D)
    c_all = g0[:, 0, :]
    cw_rows = 32 * cw_cols // D
    cw_full = jnp.concatenate([g0[2 * q, 8:8 + cw_rows, :].reshape(32, cw_cols) for q in range(NCHIP)], axis=1)[:CW]
    cwb = jnp.broadcast_to(cw_full[:, None, :], (CW, 8, DA))

    b_ada_p = lax.dynamic_slice(b_ada, (0, chip * n_ada), (1, n_ada))
    mod_part = _ada_fwd(c_all, w_ada[0], b_ada_p)
    mod_all = _allgather8(mod_part, "gather_mod").reshape(8, 8, n_ada)
    mod = jnp.concatenate([lax.dynamic_slice(mod_all[2 * q], (dev, 0), (1, n_ada)) for q in range(NCHIP)], axis=1)
    shift, scale, gate = mod[:, 0:D], mod[:, D:2 * D], mod[:, 2 * D:3 * D]
    scale1p = 1.0 + scale

    bucket = _bucket_map()
    sel_chip = jnp.reshape(chip, (1,)).astype(jnp.int32)
    sel = jnp.reshape(mc, (1,)).astype(jnp.int32)
    proj, h, wg_in, wg_out, wg_pw = _inproj_gather(
        x2, scale1p, shift, w_in[0].astype(BF16), w_out[0].astype(BF16), w_pw[0].astype(BF16), sel_chip)
    w_out_f = wg_out.reshape(D, D)
    w_pw_f = wg_pw.reshape(DA, DA)
    ya, attn_o, lse = _attn_fwd(proj, rel_bias, sinks, bucket)
    uc, pw, yc = _conv_fwd(proj, cwb, conv_b, conv_ln_g, conv_ln_b, w_pw_f, b_pw)
    gx0, dy, st_out = _outproj_loss(ya, yc, w_out_f, x2, tgt, gate, ln_g, ln_b)

    dproj = lax.empty((t, D_IN), BF16)
    dao, dpw, dproj = _dycat_gates(dy, w_out_f, attn_o, pw, proj, dproj)
    duc, sw, st_conv = _conv_bwd_ln(dpw, w_pw_f, uc, conv_ln_g, conv_ln_b)
    gw_conv, dproj = _conv_bwd_dw(duc, proj, cwb, dproj)
    dkv_cur, dkv_prev, dsacc, dsk, dproj = _attn_bwd(proj, dao, lse, rel_bias, sinks, bucket, dproj)
    dproj = _dkv_fix(dkv_cur, dkv_prev, dproj)
    g_rel = _bias_grad(dsacc, bucket).reshape(NQ, N_BUCKETS).T
    gp_in = _atb([(h, 0), (h, 1)], dproj, NB, "grad_w_in")
    gp_out = _atb([(ya, 0), (yc, 0)], dy, 1024, "grad_w_out")
    gp_pw = _atb([(sw, 0)], dpw, 1024, "grad_w_pw")

    r_out, r_pw = D // NCHIP // 2, DA // NCHIP // 2
    gi3, go3, gp3 = gp_in.reshape(2, D // 2, D_IN), gp_out.reshape(8, r_out, D), gp_pw.reshape(8, r_pw, DA)
    l_in, l_out, l_pw = _pair_exchange(gi3, go3, gp3)
    pi32, pi16 = _pair_sum(gi3, l_in.reshape(1, D // 2, D_IN), sel, "pair_sum_in")
    po32, po16 = _pair_sum(go3, l_out, sel, "pair_sum_out")
    pp32, pp16 = _pair_sum(gp3, l_pw, sel, "pair_sum_pw")
    grad_x, st_in, rc_in, rc_out, rc_pw = _dh_gradx(dproj, wg_in, gx0, x2, scale1p, pi16[0], po16, pp16)
    own_in = lax.dynamic_slice(pi32[0], (0, chip * NB), (D // 2, NB))
    own_out = lax.dynamic_index_in_dim(po32, chip, 0, keepdims=False)
    own_pw = lax.dynamic_index_in_dim(pp32, chip, 0, keepdims=False)
    h_in = _chip_sum(own_in, rc_in, sel, "chip_sum_in")
    h_out = _chip_sum(own_out, rc_out, sel, "chip_sum_out")
    h_pw = _chip_sum(own_pw, rc_pw, sel, "chip_sum_pw")
    f_in, f_out, f_pw = _pair_share(h_in, h_out, h_pw)
    g_w_in = f_in.reshape(D, NB)
    g_w_out = f_out.reshape(D // NCHIP, D)
    g_w_pw = f_pw.reshape(DA // NCHIP, DA)

    dmod = jnp.concatenate([st_in[0:1], st_in[1:2], st_out[2:3]], axis=1)
    loss_row = st_out[3:4, 0:1]
    small = jnp.concatenate([
        dmod, st_out[0:1], st_out[1:2],
        st_conv[0:1], st_conv[1:2], st_conv[2:3], st_conv[3:4],
        g_rel.reshape(1, N_BUCKETS * NQ), dsk.reshape(NKV, GRP, BLK)[:, :, 0].reshape(1, NQ), loss_row,
        gw_conv[:CW].reshape(1, CW * DA)], axis=1)
    n_small = small.shape[1]
    rows_small = -(-n_small // (8 * D)) * 8
    small = jnp.pad(small, ((0, 0), (0, rows_small * D - n_small))).reshape(rows_small, D)
    parts = _allgather8(small, "gather_small").reshape(8, rows_small, D)
    tot = _sum8(parts).reshape(1, rows_small * D)
    dmod_all = parts.reshape(8, rows_small * D)[:, 0:3 * D]

    o = 3 * D
    def take(nn):
        nonlocal o
        v = tot[:, o:o + nn]
        o += nn
        return v
    g_b_ada = tot[:, 0:3 * D]
    g_ln_g, g_ln_b = take(D), take(D)
    g_cln_g, g_cln_b, g_conv_b, g_b_pw = take(DA), take(DA), take(DA), take(DA)
    g_rel_bias = take(N_BUCKETS * NQ).reshape(N_BUCKETS, NQ)
    g_sinks = take(NQ)
    loss = take(1).reshape(())
    g_conv_w_full = take(CW * DA).reshape(CW, DA)
    g_conv_w = lax.dynamic_slice(g_conv_w_full, (0, chip * cw_cols), (CW, cw_cols))

    dmod_p = lax.dynamic_slice(dmod_all, (0, chip * n_ada), (8, n_ada))
    g_w_ada, d_w_ada, nm_w_ada, nv_w_ada = _adamw_ada(c_all.T, dmod_p, w_ada[0], m_w_ada[0], v_w_ada[0])
    d_w_in, nm_w_in, nv_w_in = _adamw(w_in[0], g_w_in, m_w_in[0], v_w_in[0], "adamw_in")
    d_w_out, nm_w_out, nv_w_out = _adamw(w_out[0], g_w_out, m_w_out[0], v_w_out[0], "adamw_out")
    d_w_pw, nm_w_pw, nv_w_pw = _adamw(w_pw[0], g_w_pw, m_w_pw[0], v_w_pw[0], "adamw_pw")
    small_w = [b_ada, rel_bias, sinks, conv_w[0], conv_b, conv_ln_g, conv_ln_b, b_pw, ln_g, ln_b]
    small_g = [g_b_ada, g_rel_bias, g_sinks, g_conv_w, g_conv_b, g_cln_g, g_cln_b, g_b_pw, g_ln_g, g_ln_b]
    small_m = [m_b_ada, m_rel_bias, m_sinks, m_conv_w[0], m_conv_b, m_conv_ln_g, m_conv_ln_b, m_b_pw, m_ln_g, m_ln_b]
    small_v = [v_b_ada, v_rel_bias, v_sinks, v_conv_w[0], v_conv_b, v_conv_ln_g, v_conv_ln_b, v_b_pw, v_ln_g, v_ln_b]
    res = _adamw_small(small_w, small_g, small_m, small_v)
    ns = len(small_w)
    d_s, nm_s, nv_s = res[:ns], res[ns:2 * ns], res[2 * ns:]

    def ordered(w_ada_, w_in_, w_pw_, w_out_, sm):
        b_ada_, rel_, sinks_, conv_w_, conv_b_, cln_g_, cln_b_, b_pw_, ln_g_, ln_b_ = sm
        return (w_ada_[None], b_ada_, w_in_[None], rel_, sinks_, conv_w_[None], conv_b_, cln_g_, cln_b_,
                w_pw_[None], b_pw_, w_out_[None], ln_g_, ln_b_)

    grads = ordered(g_w_ada, g_w_in, g_w_pw, g_w_out, small_g)
    deltas = ordered(d_w_ada, d_w_in, d_w_pw, d_w_out, d_s)
    new_m = ordered(nm_w_ada, nm_w_in, nm_w_pw, nm_w_out, nm_s)
    new_v = ordered(nv_w_ada, nv_w_in, nv_w_pw, nv_w_out, nv_s)
    return (loss, grad_x.reshape(1, t, D), *grads, *deltas, *new_m, *new_v)
```

```python
import functools
import math

import jax
import jax.numpy as jnp
from jax import lax
from jax.experimental import pallas as pl
from jax.experimental.pallas import tpu as pltpu

F32, BF16 = jnp.float32, jnp.bfloat16
S = jax.ShapeDtypeStruct
MESH = pl.DeviceIdType.MESH

D = 2048
DA = 1024
HD = 64
NQ, NKV, GRP = 16, 4, 4
BLK = 128
CW = 31
HALO = 32
D_IN = 5632
NCHIP = 4
NB = D_IN // NCHIP
N_BUCKETS, MAX_DIST = 32, 128
LN_EPS = 1e-5
ALPHA = 2.0 ** 0.25
SCALE = HD ** -0.5
NEG = -1e30
LR, B1, B2, EPS, WD, STEP = 0.001, 0.9, 0.999, 1e-08, 0.01, 10

_VMEM = pl.BlockSpec(memory_space=pltpu.VMEM)
_SMEM = pl.BlockSpec(memory_space=pltpu.SMEM)
_ANY = pl.BlockSpec(memory_space=pl.ANY)


def _cp(sem=None, vmem_mb=None):
    kw = {}
    if sem is not None:
        kw["dimension_semantics"] = sem
    if vmem_mb is not None:
        kw["vmem_limit_bytes"] = vmem_mb * 1024 * 1024
    return pltpu.CompilerParams(**kw)


def _sig(v):
    return jax.nn.sigmoid(v)


def _dsilu(g, sg):
    return sg * (1.0 + g * (1.0 - sg))


def _me():
    return lax.axis_index("x"), lax.axis_index("y"), lax.axis_index("c")


def _allgather8(x_shard, name):
    m_per, n = x_shard.shape

    def body(x_ref, out_ref, send_sems, recv_sems, local_sem):
        x, y, c = _me()
        me, sibling = (x, y, c), (x, y, 1 - c)
        chips = [(1 - x, y), (x, 1 - y), (1 - x, 1 - y)]

        def rows(px, py, pc):
            return out_ref.at[pl.ds((4 * px + 2 * py + pc) * m_per, m_per), :]

        def copy(k, block, to, src=None):
            return pltpu.make_async_remote_copy(
                src_ref=rows(*block) if src is None else src, dst_ref=rows(*block),
                send_sem=send_sems.at[k], recv_sem=recv_sems.at[k], device_id=to, device_id_type=MESH)

        mine = pltpu.make_async_copy(x_ref, rows(*me), local_sem)
        mine.start()
        first = [copy(0, me, sibling, src=x_ref)]
        first += [copy(1 + j, me, (*chip, c), src=x_ref) for j, chip in enumerate(chips)]
        for cp in first:
            cp.start()
        passed = [copy(4 + j, (*chip, c), sibling) for j, chip in enumerate(chips)]
        for j, chip in enumerate(chips):
            copy(1 + j, (*chip, c), me).wait_recv()
            passed[j].start()
        copy(0, sibling, me).wait_recv()
        for j, chip in enumerate(chips):
            copy(4 + j, (*chip, 1 - c), me).wait_recv()
        for cp in first + passed:
            cp.wait_send()
        mine.wait()

    return pl.pallas_call(
        body, name=name, out_shape=S((8 * m_per, n), x_shard.dtype),
        in_specs=[_VMEM], out_specs=_VMEM,
        scratch_shapes=[pltpu.SemaphoreType.DMA((7,)), pltpu.SemaphoreType.DMA((7,)), pltpu.SemaphoreType.DMA],
    )(x_shard)


def _inproj_gather(x, scale1p, shift, wi, wo, wp, chip_id):
    t = x.shape[0]
    tm = min(512, t)
    ni = t // tm
    shards = (wi, wo, wp)
    halves = [s.shape[0] // 2 for s in shards]

    def body(cid_ref, x_ref, sc_ref, sh_ref, wi_ref, wo_ref, wp_ref, proj_ref, h_ref, gi_ref, go_ref, gp_ref,
             wbuf, obuf, pbuf, hb_scr, send_sems, recv_sems, ld_sems, st_sems):
        jj, i = pl.program_id(0), pl.program_id(1)
        mx, my, c = _me()
        p = 2 * mx + my
        sibling = (mx, my, 1 - c)
        srcs, dsts, bufs = (wi_ref, wo_ref, wp_ref), (gi_ref, go_ref, gp_ref), (wbuf, obuf, pbuf)
        chips = {1: (mx, 1 - my), 2: (1 - mx, my), 3: (1 - mx, 1 - my)}

        def half(ref, w, hc):
            return ref.at[pl.ds(hc * halves[w], halves[w])]

        def copy(k, src, dst, to):
            return pltpu.make_async_remote_copy(src_ref=src, dst_ref=dst, send_sem=send_sems.at[k],
                                                recv_sem=recv_sems.at[k], device_id=to, device_id_type=MESH)

        def sent(w, m):
            return copy(3 * w + m - 1, half(srcs[w], w, c), half(dsts[w].at[p], w, c), (*chips[m], c))

        def landed(w, m):
            blk = half(dsts[w].at[jnp.bitwise_xor(p, m)], w, c)
            return copy(3 * w + m - 1, blk, blk, (*chips[m], c))

        def passed(w, m, hc):
            blk = half(dsts[w].at[jnp.bitwise_xor(p, m)], w, hc)
            return copy(9 + 3 * w + m - 1, blk, blk, sibling)

        def vm(w, slot):
            return bufs[w].at[slot] if w == 0 else bufs[w]

        def load(w, src, slot=0):
            return pltpu.make_async_copy(src, vm(w, slot), ld_sems.at[w])

        def store(w):
            return pltpu.make_async_copy(vm(w, 0), dsts[w].at[p], st_sems.at[w])

        def block_load(m):
            return load(0, gi_ref.at[jnp.bitwise_xor(p, m)], m % 2)

        @pl.when((jj == 0) & (i == 0))
        def _():
            sent(0, 1).start()
            sent(0, 2).start()
            for w in range(3):
                load(w, srcs[w]).start()
            for w in range(3):
                load(w, srcs[w]).wait()
                store(w).start()

        for m in (1, 2, 3):
            @pl.when((jj == m) & (i == 0))
            def _(m=m):
                if m == 1:
                    sent(0, 3).start()
                    store(0).wait()
                if m == 2:
                    for w in (1, 2):
                        for mm in (1, 2, 3):
                            sent(w, mm).start()
                block_load(m).wait()

        hb_scr[...] = (x_ref[...] * sc_ref[...] + sh_ref[...]).astype(BF16)

        @pl.when(jj == 0)
        def _():
            h_ref[...] = hb_scr[...]

        proj_ref[...] = jnp.dot(hb_scr[...], wbuf[jj % 2], preferred_element_type=F32).astype(BF16)

        for m in (1, 2, 3):
            @pl.when((jj == m - 1) & (i == ni - 1))
            def _(m=m):
                landed(0, m).wait_recv()
                passed(0, m, c).start()
                passed(0, m, 1 - c).wait_recv()
                block_load(m).start()

        @pl.when((jj == NCHIP - 1) & (i == ni - 1))
        def _():
            for w in (1, 2):
                for m in (1, 2, 3):
                    landed(w, m).wait_recv()
                    passed(w, m, c).start()
            for w in (1, 2):
                for m in (1, 2, 3):
                    passed(w, m, 1 - c).wait_recv()
            for w in range(3):
                for m in (1, 2, 3):
                    sent(w, m).wait_send()
                    passed(w, m, c).wait_send()
            store(1).wait()
            store(2).wait()

    row = pl.BlockSpec((1, D), lambda jj, i, s: (0, 0))
    return pl.pallas_call(
        body, name="inproj_gather",
        grid_spec=pltpu.PrefetchScalarGridSpec(
            num_scalar_prefetch=1, grid=(NCHIP, ni),
            in_specs=[pl.BlockSpec((tm, D), lambda jj, i, s: (i, 0)), row, row, _ANY, _ANY, _ANY],
            out_specs=[pl.BlockSpec((tm, NB), lambda jj, i, s: (i, jnp.bitwise_xor(s[0], jj))),
                       pl.BlockSpec((tm, D), lambda jj, i, s: (jnp.where(jj == 0, i, ni - 1), 0)),
                       _ANY, _ANY, _ANY],
            scratch_shapes=[pltpu.VMEM((2,) + wi.shape, BF16), pltpu.VMEM(wo.shape, BF16), pltpu.VMEM(wp.shape, BF16),
                            pltpu.VMEM((tm, D), BF16),
                            pltpu.SemaphoreType.DMA((18,)), pltpu.SemaphoreType.DMA((18,)),
                            pltpu.SemaphoreType.DMA((3,)), pltpu.SemaphoreType.DMA((3,))]),
        out_shape=[S((t, D_IN), BF16), S((t, D), BF16)] + [S((NCHIP,) + s.shape, s.dtype) for s in shards],
        compiler_params=_cp(("arbitrary", "arbitrary"), 48),
    )(chip_id, x, scale1p, shift, wi, wo, wp)


def _pair_exchange(g_out, g_pw):
    def body(go, gp, lo, lp, send_sems, recv_sems):
        x, y, c = _me()
        sibling = (x, y, 1 - c)
        copies = []
        for j in range(NCHIP):
            copies.append((go.at[2 * j + 1 - c], lo.at[j]))
        for j in range(NCHIP):
            copies.append((gp.at[2 * j + 1 - c], lp.at[j]))
        cps = [pltpu.make_async_remote_copy(src_ref=s_, dst_ref=d_, send_sem=send_sems.at[k], recv_sem=recv_sems.at[k],
                                            device_id=sibling, device_id_type=MESH) for k, (s_, d_) in enumerate(copies)]
        for cp in cps:
            cp.start()
        for cp in cps:
            cp.wait_recv()
        for cp in cps:
            cp.wait_send()

    n = 2 * NCHIP
    return pl.pallas_call(
        body, name="grad_pair_exchange",
        out_shape=[S((NCHIP,) + g_out.shape[1:], g_out.dtype), S((NCHIP,) + g_pw.shape[1:], g_pw.dtype)],
        in_specs=[_ANY] * 2, out_specs=[_ANY] * 2,
        scratch_shapes=[pltpu.SemaphoreType.DMA((n,)), pltpu.SemaphoreType.DMA((n,))],
    )(g_out, g_pw)


def _chip_exchange_copies(pi, po, pp, li, lo, lp, send_sems, recv_sems):
    x, y, c = _me()
    chips = [(1 - x, y), (x, 1 - y), (1 - x, 1 - y)]
    cps = []
    for j, chip in enumerate(chips):
        q = 2 * chip[0] + chip[1]
        pairs = ((pi.at[:, pl.ds(q * NB, NB)], li.at[j]), (po.at[q], lo.at[j]), (pp.at[q], lp.at[j]))
        for w, (src, dst) in enumerate(pairs):
            k = 3 * j + w
            cps.append(pltpu.make_async_remote_copy(src_ref=src, dst_ref=dst, send_sem=send_sems.at[k],
                                                    recv_sem=recv_sems.at[k], device_id=(*chip, c), device_id_type=MESH))
    return cps


def _pair_share(f_in, f_out, f_pw):
    bufs = (f_in, f_out, f_pw)

    def body(ai, ao, ap, fi, fo, fp, send_sems, recv_sems):
        x, y, c = _me()
        sibling = (x, y, 1 - c)

        def copy(w, ref, hc):
            return pltpu.make_async_remote_copy(src_ref=ref.at[hc], dst_ref=ref.at[hc], send_sem=send_sems.at[w],
                                                recv_sem=recv_sems.at[w], device_id=sibling, device_id_type=MESH)

        cps = [copy(w, ref, c) for w, ref in enumerate((fi, fo, fp))]
        for cp in cps:
            cp.start()
        for w, ref in enumerate((fi, fo, fp)):
            copy(w, ref, 1 - c).wait_recv()
        for cp in cps:
            cp.wait_send()

    return pl.pallas_call(
        body, name="grad_pair_share",
        out_shape=[S(b.shape, b.dtype) for b in bufs],
        in_specs=[_ANY] * 3, out_specs=[_ANY] * 3, input_output_aliases={0: 0, 1: 1, 2: 2},
        scratch_shapes=[pltpu.SemaphoreType.DMA((3,)), pltpu.SemaphoreType.DMA((3,))],
    )(*bufs)


def _pair_sum(g, l, sel, name):
    n, r, ccols = l.shape
    tr = min(256 if ccols <= D else 128, r)

    def body(sel_ref, g_ref, l_ref, o32_ref, o16_ref):
        v = g_ref[...] + l_ref[...]
        o32_ref[...] = v
        o16_ref[...] = v.astype(BF16)

    spec_l = pl.BlockSpec((None, tr, ccols), lambda j, i, s: (j, i, 0))
    return pl.pallas_call(
        body, name=name,
        grid_spec=pltpu.PrefetchScalarGridSpec(
            num_scalar_prefetch=1, grid=(n, r // tr),
            in_specs=[pl.BlockSpec((None, tr, ccols), lambda j, i, s: (2 * j + s[0], i, 0)), spec_l],
            out_specs=[spec_l, spec_l]),
        out_shape=[S(l.shape, F32), S(l.shape, BF16)],
        compiler_params=_cp(("parallel", "parallel"), 48),
    )(sel, g, l)


def _chip_sum(own, recv, sel, name):
    r, ccols = own.shape
    tr = min(256, r)

    def body(sel_ref, o_ref, r_ref, out_ref):
        v = o_ref[...]
        for j in range(3):
            v = v + r_ref[j].astype(F32)
        out_ref[...] = v

    return pl.pallas_call(
        body, name=name,
        grid_spec=pltpu.PrefetchScalarGridSpec(
            num_scalar_prefetch=1, grid=(r // tr,),
            in_specs=[pl.BlockSpec((tr, ccols), lambda i, s: (i, 0)), pl.BlockSpec((3, tr, ccols), lambda i, s: (0, i, 0))],
            out_specs=pl.BlockSpec((None, tr, ccols), lambda i, s: (s[0], i, 0))),
        out_shape=S((2, r, ccols), F32),
        compiler_params=_cp(("parallel",), 48),
    )(sel, own, recv)


def _ada_fwd(c_all, w_ada, b_ada_p):
    n = w_ada.shape[1]
    tn = 512

    def body(c_ref, w_ref, b_ref, o_ref):
        cv = c_ref[...]
        ca = cv * _sig(cv)
        o_ref[...] = jnp.dot(ca, w_ref[...], preferred_element_type=F32) + b_ref[...]

    return pl.pallas_call(
        body, name="ada_fwd", grid=(n // tn,),
        in_specs=[pl.BlockSpec((8, D), lambda j: (0, 0)), pl.BlockSpec((D, tn), lambda j: (0, j)),
                  pl.BlockSpec((1, tn), lambda j: (0, j))],
        out_specs=pl.BlockSpec((8, tn), lambda j: (0, j)),
        out_shape=S((8, n), F32),
        compiler_params=_cp(("parallel",), 32),
    )(c_all, w_ada, b_ada_p)


def _build_bias(rel_ref, bk_ref, bias_ref):
    bk = bk_ref[...]
    kj = lax.broadcasted_iota(jnp.int32, (2 * BLK, BLK), 0)
    for hd in range(NQ):
        acc = jnp.full((2 * BLK, BLK), NEG, F32)
        for b in range(N_BUCKETS):
            acc = jnp.where(bk == b, rel_ref[b, hd], acc)
        lanes = slice((hd % GRP) * BLK, (hd % GRP + 1) * BLK)
        bias_ref[1, hd // GRP, :, lanes] = acc
        bias_ref[0, hd // GRP, :, lanes] = jnp.where(kj < BLK, NEG, acc)


def _group_rows(ref, h):
    return jnp.concatenate([ref[:, (GRP * h + g) * HD:(GRP * h + g + 1) * HD] for g in range(GRP)], axis=0)


def _sink_row(sink_ref, h):
    return jnp.concatenate([jnp.full((1, BLK), sink_ref[0, GRP * h + g], F32) for g in range(GRP)], axis=1)


def _attn_fwd(proj, rel_bias, sinks, bucket):
    t = proj.shape[0]

    def body(rel_ref, sink_ref, bk_ref, q_ref, kvc_ref, kvp_ref, glo_ref, ghi_ref, ya_ref, o_ref, lse_ref, bias_ref):
        n = pl.program_id(0)

        @pl.when(n == 0)
        def _():
            _build_bias(rel_ref, bk_ref, bias_ref)

        tbl = jnp.where(n == 0, 0, 1)
        kv = jnp.concatenate([kvp_ref[...], kvc_ref[...]], axis=0)
        for h in range(NKV):
            k_h = kv[:, h * HD:(h + 1) * HD]
            v_h = kv[:, NKV * HD + h * HD:NKV * HD + (h + 1) * HD]
            q4 = _group_rows(q_ref, h) * SCALE
            s = lax.dot_general(k_h, q4, (((1,), (1,)), ((), ())), preferred_element_type=F32) + bias_ref[tbl, h]
            sink = _sink_row(sink_ref, h)
            m = jnp.maximum(jnp.max(s, axis=0, keepdims=True), sink)
            p = jnp.exp(s - m)
            l = jnp.sum(p, axis=0, keepdims=True) + jnp.exp(sink - m)
            pn = (p * (1.0 / l)).astype(BF16)
            o4 = lax.dot_general(pn, v_h, (((0,), (0,)), ((), ())), preferred_element_type=F32)
            lse_ref[h:h + 1, :] = m + jnp.log(l)
            for g in range(GRP):
                hd = GRP * h + g
                cols = slice(hd * HD, (hd + 1) * HD)
                o = o4[g * BLK:(g + 1) * BLK]
                g_ref = glo_ref if hd < NQ // 2 else ghi_ref
                gc = slice((hd % (NQ // 2)) * HD, (hd % (NQ // 2) + 1) * HD)
                gt = g_ref[:, gc].astype(F32)
                o_ref[:, cols] = o.astype(BF16)
                ya_ref[:, cols] = (o * (gt * _sig(gt))).astype(BF16)

    blk = lambda w, cidx: pl.BlockSpec((BLK, w), lambda n: (n, cidx))
    return pl.pallas_call(
        body, name="attn_fwd", grid=(t // BLK,),
        in_specs=[_SMEM, _SMEM, pl.BlockSpec((2 * BLK, BLK), lambda n: (0, 0)),
                  blk(DA, 0), blk(512, 2), pl.BlockSpec((BLK, 512), lambda n: (jnp.maximum(n - 1, 0), 2)),
                  blk(512, 3), blk(512, 4)],
        out_specs=[blk(DA, 0), blk(DA, 0), pl.BlockSpec((None, NKV, GRP * BLK), lambda n: (n, 0, 0))],
        out_shape=[S((t, DA), BF16), S((t, DA), BF16), S((t // BLK, NKV, GRP * BLK), F32)],
        scratch_shapes=[pltpu.VMEM((2, NKV, 2 * BLK, GRP * BLK), F32)],
        compiler_params=_cp(("arbitrary",), 32),
    )(rel_bias, sinks, bucket, proj, proj, proj, proj, proj)


def _conv_rows(t):
    return min(256, t)


def _glu_into(u_scr, i, a_refs, b_refs, ah_refs, bh_refs, tc):
    for cc in range(DA // 128):
        half, lc = cc // 4, slice((cc % 4) * 128, (cc % 4 + 1) * 128)
        uh = ah_refs[half][:, lc].astype(F32) * _sig(bh_refs[half][:, lc].astype(F32))
        u_scr[cc, 0:HALO, :] = jnp.where(i == 0, 0.0, uh)
        u_scr[cc, HALO:HALO + tc, :] = a_refs[half][:, lc].astype(F32) * _sig(b_refs[half][:, lc].astype(F32))


def _conv_fwd(proj, cwb, conv_b, ln_g, ln_b, wpw, b_pw):
    t = proj.shape[0]
    tc = _conv_rows(t)
    rc = min(128, tc)

    def body(alo, ahi, blo, bhi, alo_h, ahi_h, blo_h, bhi_h, glo, ghi, cw_ref, cb_ref, lg_ref, lb_ref, wpw_ref, bpw_ref,
             uc_ref, pw_ref, yc_ref, u_scr):
        i = pl.program_id(0)
        _glu_into(u_scr, i, (alo, ahi), (blo, bhi), (alo_h, ahi_h), (blo_h, bhi_h), tc)

        def rows(r, carry):
            r0 = pl.multiple_of(r * rc, rc)
            for cc in range(DA // 128):
                lanes = pl.ds(cc * 128, 128)
                acc = jnp.zeros((rc // 8, 8, 128), F32)
                for k in range(CW):
                    u = u_scr[cc, pl.ds(r0 + (HALO - CW + 1) + k, rc), :].reshape(rc // 8, 8, 128)
                    acc = acc + u * cw_ref[k, :, lanes]
                uc_ref[pl.ds(r0, rc), lanes] = acc.reshape(rc, 128) + cb_ref[:, lanes]
            return carry

        lax.fori_loop(0, tc // rc, rows, 0)

        uc = uc_ref[...]
        mu = jnp.mean(uc, axis=-1, keepdims=True)
        xc = uc - mu
        rstd = lax.rsqrt(jnp.mean(xc * xc, axis=-1, keepdims=True) + LN_EPS)
        ln = xc * rstd * lg_ref[...] + lb_ref[...]
        sw = (ln * _sig(ln)).astype(BF16)
        pw = jnp.dot(sw, wpw_ref[...], preferred_element_type=F32) + bpw_ref[...]
        pw_ref[...] = pw.astype(BF16)
        gt = jnp.concatenate([glo[...], ghi[...]], axis=1).astype(F32)
        yc_ref[...] = (pw * (gt * _sig(gt))).astype(BF16)

    nh = tc // HALO
    cur = lambda cidx: pl.BlockSpec((tc, 512), lambda i: (i, cidx))
    halo = lambda cidx: pl.BlockSpec((HALO, 512), lambda i: (jnp.maximum(i * nh - 1, 0), cidx))
    row = pl.BlockSpec((1, DA), lambda i: (0, 0))
    full = pl.BlockSpec((tc, DA), lambda i: (i, 0))
    return pl.pallas_call(
        body, name="conv_fwd", grid=(t // tc,),
        in_specs=[cur(5), cur(6), cur(7), cur(8), halo(5), halo(6), halo(7), halo(8), cur(9), cur(10),
                  pl.BlockSpec((CW, 8, DA), lambda i: (0, 0, 0)), row, row, row,
                  pl.BlockSpec((DA, DA), lambda i: (0, 0)), row],
        out_specs=[full, full, full],
        out_shape=[S((t, DA), F32), S((t, DA), BF16), S((t, DA), BF16)],
        scratch_shapes=[pltpu.VMEM((DA // 128, HALO + tc, 128), F32)],
        compiler_params=_cp(("arbitrary",), 48),
    )(*([proj] * 10), cwb, conv_b, ln_g, ln_b, wpw, b_pw)


def _outproj_loss(ya, yc, w_out, x, target, gate, ln_g, ln_b):
    t = x.shape[0]
    tm = min(256, t)

    def body(ya_ref, yc_ref, w_ref, x_ref, t_ref, g_ref, lg_ref, lb_ref, gx_ref, dy_ref, st_ref):
        @pl.when(pl.program_id(0) == 0)
        def _():
            st_ref[...] = jnp.zeros_like(st_ref)

        y = jnp.dot(ya_ref[...], w_ref[0:DA, :], preferred_element_type=F32)
        y = y + jnp.dot(yc_ref[...], w_ref[DA:2 * DA, :], preferred_element_type=F32)
        gate_v = g_ref[...]
        z = ALPHA * x_ref[...] + gate_v * y
        mu = jnp.mean(z, axis=-1, keepdims=True)
        zc = z - mu
        rstd = lax.rsqrt(jnp.mean(zc * zc, axis=-1, keepdims=True) + LN_EPS)
        zh = zc * rstd
        diff = zh * lg_ref[...] + lb_ref[...] - t_ref[...]
        dout = diff * (1.0 / D)
        dzh = dout * lg_ref[...]
        m1 = jnp.mean(dzh, axis=-1, keepdims=True)
        m2 = jnp.mean(dzh * zh, axis=-1, keepdims=True)
        dz = rstd * (dzh - m1 - zh * m2)
        gx_ref[...] = ALPHA * dz
        dy_ref[...] = (dz * gate_v).astype(BF16)
        st_ref[0:1, :] += jnp.sum(dout * zh, axis=0, keepdims=True)
        st_ref[1:2, :] += jnp.sum(dout, axis=0, keepdims=True)
        st_ref[2:3, :] += jnp.sum(dz * y, axis=0, keepdims=True)
        st_ref[3:4, :] += jnp.sum(diff * diff, axis=0, keepdims=True) * (0.5 / D)

        @pl.when(pl.program_id(0) == t // tm - 1)
        def _():
            st_ref[3:4, :] = jnp.broadcast_to(jnp.sum(st_ref[3:4, :], axis=-1, keepdims=True), (1, D))

    row = pl.BlockSpec((1, D), lambda i: (0, 0))
    half = pl.BlockSpec((tm, DA), lambda i: (i, 0))
    full = pl.BlockSpec((tm, D), lambda i: (i, 0))
    return pl.pallas_call(
        body, name="outproj_loss", grid=(t // tm,),
        in_specs=[half, half, pl.BlockSpec((D, D), lambda i: (0, 0)), full, full, row, row, row],
        out_specs=[full, full, pl.BlockSpec((8, D), lambda i: (0, 0))],
        out_shape=[S((t, D), F32), S((t, D), BF16), S((8, D), F32)],
        compiler_params=_cp(("arbitrary",), 56),
    )(ya, yc, w_out, x, target, gate, ln_g, ln_b)


COL_DQ, COL_DKV, COL_DGA, COL_DGLU_A, COL_DGLU_B, COL_DGC = 0, 1024, 1536, 2560, 3584, 4608


def _tile_copy(dst, stage, sems, slot, row0, col0):
    rows, width = stage.shape[1:]
    return pltpu.make_async_copy(stage.at[slot], dst.at[pl.ds(row0, rows), pl.ds(col0, width)], sems.at[slot])


def _stage_slot(dst, stage, sems, step):
    slot = step % 2

    @pl.when(step >= 2)
    def _():
        _tile_copy(dst, stage, sems, slot, 0, 0).wait()

    return slot


def _tile_send(dst, stage, sems, step, nsteps, col0):
    rows = stage.shape[1]
    slot = step % 2
    _tile_copy(dst, stage, sems, slot, pl.multiple_of(step * rows, rows), col0).start()

    @pl.when(step == nsteps - 1)
    def _():
        _tile_copy(dst, stage, sems, slot, 0, col0).wait()
        if nsteps > 1:
            _tile_copy(dst, stage, sems, 1 - slot, 0, col0).wait()


def _dycat_gates(dy, w_out, attn_o, pw, proj, dproj):
    t = dy.shape[0]
    tm = min(256, t)
    nt = t // tm

    def body(dy_ref, w_ref, o_ref, pw_ref, galo, gahi, gclo, gchi, dp_in, dao_ref, dpw_ref, dp_ref,
             st_a, st_c, sem_a, sem_c):
        i = pl.program_id(0)
        sl_a = _stage_slot(dp_ref, st_a, sem_a, i)
        sl_c = _stage_slot(dp_ref, st_c, sem_c, i)
        dyc = lax.dot_general(dy_ref[...], w_ref[...], (((1,), (1,)), ((), ())), preferred_element_type=F32)
        da, dc = dyc[:, 0:DA], dyc[:, DA:2 * DA]
        ga = jnp.concatenate([galo[...], gahi[...]], axis=1).astype(F32)
        sa = _sig(ga)
        dao_ref[...] = (da * (ga * sa)).astype(BF16)
        st_a[sl_a] = (da * o_ref[...].astype(F32) * _dsilu(ga, sa)).astype(BF16)
        _tile_send(dp_ref, st_a, sem_a, i, nt, COL_DGA)
        gc = jnp.concatenate([gclo[...], gchi[...]], axis=1).astype(F32)
        sc = _sig(gc)
        dpw_ref[...] = (dc * (gc * sc)).astype(BF16)
        st_c[sl_c] = (dc * pw_ref[...].astype(F32) * _dsilu(gc, sc)).astype(BF16)
        _tile_send(dp_ref, st_c, sem_c, i, nt, COL_DGC)

    half = pl.BlockSpec((tm, DA), lambda i: (i, 0))
    cur = lambda cidx: pl.BlockSpec((tm, 512), lambda i: (i, cidx))
    return pl.pallas_call(
        body, name="dycat_gates", grid=(nt,),
        in_specs=[pl.BlockSpec((tm, D), lambda i: (i, 0)), pl.BlockSpec((D, D), lambda i: (0, 0)), half, half,
                  cur(3), cur(4), cur(9), cur(10), _ANY],
        out_specs=[half, half, _ANY],
        out_shape=[S((t, DA), BF16), S((t, DA), BF16), S(dproj.shape, dproj.dtype)],
        input_output_aliases={8: 2},
        scratch_shapes=[pltpu.VMEM((2, tm, DA), BF16), pltpu.VMEM((2, tm, DA), BF16),
                        pltpu.SemaphoreType.DMA((2,)), pltpu.SemaphoreType.DMA((2,))],
        compiler_params=_cp(("arbitrary",), 48),
    )(dy, w_out, attn_o, pw, proj, proj, proj, proj, dproj)


def _conv_bwd_ln(dpw, wpw, uc, ln_g, ln_b):
    t = dpw.shape[0]
    tc = min(256, t)

    def body(dpw_ref, w_ref, uc_ref, lg_ref, lb_ref, duc_ref, sw_ref, st_ref):
        @pl.when(pl.program_id(0) == 0)
        def _():
            st_ref[...] = jnp.zeros_like(st_ref)

        dpw_v = dpw_ref[...]
        ds = lax.dot_general(dpw_v, w_ref[...], (((1,), (1,)), ((), ())), preferred_element_type=F32)
        uc = uc_ref[...]
        mu = jnp.mean(uc, axis=-1, keepdims=True)
        xc = uc - mu
        rstd = lax.rsqrt(jnp.mean(xc * xc, axis=-1, keepdims=True) + LN_EPS)
        uh = xc * rstd
        ln = uh * lg_ref[...] + lb_ref[...]
        sg = _sig(ln)
        sw_ref[...] = (ln * sg).astype(BF16)
        dln = ds * _dsilu(ln, sg)
        dxh = dln * lg_ref[...]
        m1 = jnp.mean(dxh, axis=-1, keepdims=True)
        m2 = jnp.mean(dxh * uh, axis=-1, keepdims=True)
        duc = rstd * (dxh - m1 - uh * m2)
        duc_ref[...] = duc
        st_ref[0:1, :] += jnp.sum(dln * uh, axis=0, keepdims=True)
        st_ref[1:2, :] += jnp.sum(dln, axis=0, keepdims=True)
        st_ref[2:3, :] += jnp.sum(duc, axis=0, keepdims=True)
        st_ref[3:4, :] += jnp.sum(dpw_v.astype(F32), axis=0, keepdims=True)

    row = pl.BlockSpec((1, DA), lambda i: (0, 0))
    full = pl.BlockSpec((tc, DA), lambda i: (i, 0))
    return pl.pallas_call(
        body, name="conv_bwd_ln", grid=(t // tc,),
        in_specs=[full, pl.BlockSpec((DA, DA), lambda i: (0, 0)), full, row, row],
        out_specs=[full, full, pl.BlockSpec((8, DA), lambda i: (0, 0))],
        out_shape=[S((t, DA), F32), S((t, DA), BF16), S((8, DA), F32)],
        compiler_params=_cp(("arbitrary",), 48),
    )(dpw, wpw, uc, ln_g, ln_b)


def _conv_bwd_dw(duc, proj, cwb, dproj):
    t = duc.shape[0]
    tc = _conv_rows(t)
    rc = min(128, tc)
    nt = t // tc
    off = HALO - CW + 1

    def body(dcur, dnext, alo, ahi, blo, bhi, alo_h, ahi_h, blo_h, bhi_h, cw_ref, dp_in, gw_ref, dp_ref,
             u_scr, d_scr, g_scr, st_ab, sem_ab):
        i = pl.program_id(0)
        slot = _stage_slot(dp_ref, st_ab, sem_ab, i)

        @pl.when(i == 0)
        def _():
            g_scr[...] = jnp.zeros_like(g_scr)

        _glu_into(u_scr, i, (alo, ahi), (blo, bhi), (alo_h, ahi_h), (blo_h, bhi_h), tc)
        for cc in range(DA // 128):
            d_scr[cc, 0:tc, :] = dcur[:, cc * 128:(cc + 1) * 128]
            d_scr[cc, tc:tc + HALO, :] = jnp.where(i == nt - 1, 0.0, dnext[:, cc * 128:(cc + 1) * 128])

        def rows(r, carry):
            r0 = pl.multiple_of(r * rc, rc)
            for cc in range(DA // 128):
                lanes = pl.ds(cc * 128, 128)
                acc = jnp.zeros((rc // 8, 8, 128), F32)
                for j in range(CW):
                    dv = d_scr[cc, pl.ds(r0 + j, rc), :].reshape(rc // 8, 8, 128)
                    acc = acc + dv * cw_ref[CW - 1 - j, :, lanes]
                du = acc.reshape(rc, 128)
                d0 = d_scr[cc, pl.ds(r0, rc), :].reshape(rc // 8, 8, 128)
                for k in range(CW):
                    u = u_scr[cc, pl.ds(r0 + off + k, rc), :].reshape(rc // 8, 8, 128)
                    g_scr[k, :, lanes] += jnp.sum(d0 * u, axis=0)
                a_ref, b_ref = (alo, blo) if cc < 4 else (ahi, bhi)
                lc = pl.ds((cc % 4) * 128, 128)
                av = a_ref[pl.ds(r0, rc), lc].astype(F32)
                sb = _sig(b_ref[pl.ds(r0, rc), lc].astype(F32))
                st_ab[slot, pl.ds(r0, rc), lanes] = (du * sb).astype(BF16)
                st_ab[slot, pl.ds(r0, rc), pl.ds(DA + cc * 128, 128)] = (du * av * sb * (1.0 - sb)).astype(BF16)
            return carry

        lax.fori_loop(0, tc // rc, rows, 0)
        _tile_send(dp_ref, st_ab, sem_ab, i, nt, COL_DGLU_A)

        @pl.when(i == nt - 1)
        def _():
            gw_ref[0:CW, :] = jnp.sum(g_scr[...], axis=1)
            gw_ref[CW:32, :] = jnp.zeros((32 - CW, DA), F32)

    nh = tc // HALO
    nhb = t // HALO
    cur = lambda cidx: pl.BlockSpec((tc, 512), lambda i: (i, cidx))
    halo = lambda cidx: pl.BlockSpec((HALO, 512), lambda i: (jnp.maximum(i * nh - 1, 0), cidx))
    full = pl.BlockSpec((tc, DA), lambda i: (i, 0))
    return pl.pallas_call(
        body, name="conv_bwd_dw", grid=(nt,),
        in_specs=[full, pl.BlockSpec((HALO, DA), lambda i: (jnp.minimum((i + 1) * nh, nhb - 1), 0)),
                  cur(5), cur(6), cur(7), cur(8), halo(5), halo(6), halo(7), halo(8),
                  pl.BlockSpec((CW, 8, DA), lambda i: (0, 0, 0)), _ANY],
        out_specs=[pl.BlockSpec((32, DA), lambda i: (0, 0)), _ANY],
        out_shape=[S((32, DA), F32), S(dproj.shape, dproj.dtype)],
        input_output_aliases={11: 1},
        scratch_shapes=[pltpu.VMEM((DA // 128, HALO + tc, 128), F32), pltpu.VMEM((DA // 128, tc + HALO, 128), F32),
                        pltpu.VMEM((CW, 8, DA), F32), pltpu.VMEM((2, tc, 2 * DA), BF16), pltpu.SemaphoreType.DMA((2,))],
        compiler_params=_cp(("arbitrary",), 48),
    )(duc, duc, *([proj] * 8), cwb, dproj)


def _attn_bwd(proj, dao, lse, rel_bias, sinks, bucket, dproj):
    t = proj.shape[0]

    def body(rel_ref, sink_ref, bk_ref, q_ref, kvc_ref, kvp_ref, do_ref, lse_ref, dp_in,
             dcur_ref, dprev_ref, dsacc_ref, dsk_ref, dp_ref, bias_ref, st_q, sem_q):
        n = pl.program_id(0)
        slot = _stage_slot(dp_ref, st_q, sem_q, n)

        @pl.when(n == 0)
        def _():
            _build_bias(rel_ref, bk_ref, bias_ref)
            dsacc_ref[...] = jnp.zeros_like(dsacc_ref)
            dsk_ref[...] = jnp.zeros_like(dsk_ref)

        tbl = jnp.where(n == 0, 0, 1)
        kv = jnp.concatenate([kvp_ref[...], kvc_ref[...]], axis=0)
        for h in range(NKV):
            k_h = kv[:, h * HD:(h + 1) * HD]
            v_h = kv[:, NKV * HD + h * HD:NKV * HD + (h + 1) * HD]
            q4 = _group_rows(q_ref, h) * SCALE
            do4 = _group_rows(do_ref, h)
            lse_h = lse_ref[h:h + 1, :]
            s = lax.dot_general(k_h, q4, (((1,), (1,)), ((), ())), preferred_element_type=F32)
            p = jnp.exp(s + bias_ref[tbl, h] - lse_h)
            dp = lax.dot_general(v_h, do4, (((1,), (1,)), ((), ())), preferred_element_type=F32)
            delta = jnp.sum(p * dp, axis=0, keepdims=True)
            ds = p * (dp - delta)
            dsacc_ref[h] += ds
            dsk_ref[h:h + 1, :] += -jnp.exp(_sink_row(sink_ref, h) - lse_h) * delta
            dsb = ds.astype(BF16)
            dq4 = lax.dot_general(dsb, k_h, (((0,), (0,)), ((), ())), preferred_element_type=F32) * SCALE
            for g in range(GRP):
                hd = GRP * h + g
                st_q[slot, :, hd * HD:(hd + 1) * HD] = dq4[g * BLK:(g + 1) * BLK].astype(BF16)
            dk = jnp.dot(dsb, q4, preferred_element_type=F32)
            dv = jnp.dot(p.astype(BF16), do4, preferred_element_type=F32)
            kc = slice(h * HD, (h + 1) * HD)
            vc = slice(NKV * HD + h * HD, NKV * HD + (h + 1) * HD)
            dprev_ref[:, kc] = dk[0:BLK]
            dcur_ref[:, kc] = dk[BLK:2 * BLK]
            dprev_ref[:, vc] = dv[0:BLK]
            dcur_ref[:, vc] = dv[BLK:2 * BLK]

        _tile_send(dp_ref, st_q, sem_q, n, t // BLK, COL_DQ)

        @pl.when(n == t // BLK - 1)
        def _():
            for g in range(GRP):
                lanes = slice(g * BLK, (g + 1) * BLK)
                dsk_ref[:, lanes] = jnp.broadcast_to(jnp.sum(dsk_ref[:, lanes], axis=-1, keepdims=True), (NKV, BLK))

    blk = lambda w, cidx: pl.BlockSpec((BLK, w), lambda n: (n, cidx))
    return pl.pallas_call(
        body, name="attn_bwd", grid=(t // BLK,),
        in_specs=[_SMEM, _SMEM, pl.BlockSpec((2 * BLK, BLK), lambda n: (0, 0)),
                  blk(DA, 0), blk(512, 2), pl.BlockSpec((BLK, 512), lambda n: (jnp.maximum(n - 1, 0), 2)),
                  blk(DA, 0), pl.BlockSpec((None, NKV, GRP * BLK), lambda n: (n, 0, 0)), _ANY],
        out_specs=[blk(512, 0), blk(512, 0),
                   pl.BlockSpec((NKV, 2 * BLK, GRP * BLK), lambda n: (0, 0, 0)), pl.BlockSpec((NKV, GRP * BLK), lambda n: (0, 0)),
                   _ANY],
        out_shape=[S((t, 512), F32), S((t, 512), F32), S((NKV, 2 * BLK, GRP * BLK), F32),
                   S((NKV, GRP * BLK), F32), S(dproj.shape, dproj.dtype)],
        input_output_aliases={8: 4},
        scratch_shapes=[pltpu.VMEM((2, NKV, 2 * BLK, GRP * BLK), F32), pltpu.VMEM((2, BLK, DA), BF16),
                        pltpu.SemaphoreType.DMA((2,))],
        compiler_params=_cp(("arbitrary",), 40),
    )(rel_bias, sinks, bucket, proj, proj, proj, dao, lse, dproj)


def _dkv_fix(dcur, dprev, dproj):
    t = dcur.shape[0]
    tb = min(1024, t)
    nt = t // tb
    per = tb // BLK

    def body(c_ref, p_ref, pn_ref, dp_in, dp_ref, st_kv, sem_kv):
        i = pl.program_id(0)
        slot = _stage_slot(dp_ref, st_kv, sem_kv, i)
        if tb > BLK:
            st_kv[slot, 0:tb - BLK, :] = (c_ref[0:tb - BLK, :] + p_ref[BLK:tb, :]).astype(BF16)
        nxt = jnp.where(i == nt - 1, 0.0, pn_ref[...])
        st_kv[slot, tb - BLK:tb, :] = (c_ref[tb - BLK:tb, :] + nxt).astype(BF16)
        _tile_send(dp_ref, st_kv, sem_kv, i, nt, COL_DKV)

    tile = pl.BlockSpec((tb, 512), lambda i: (i, 0))
    return pl.pallas_call(
        body, name="dkv_fix", grid=(nt,),
        in_specs=[tile, tile, pl.BlockSpec((BLK, 512), lambda i: (jnp.minimum((i + 1) * per, t // BLK - 1), 0)), _ANY],
        out_specs=_ANY,
        out_shape=S(dproj.shape, dproj.dtype),
        input_output_aliases={3: 0},
        scratch_shapes=[pltpu.VMEM((2, tb, 512), BF16), pltpu.SemaphoreType.DMA((2,))],
        compiler_params=_cp(("arbitrary",), 32),
    )(dcur, dprev, dprev, dproj)


def _bias_grad(dsacc, bucket):
    def body(ds_ref, bk_ref, o_ref, row_scr):
        bk = bk_ref[...]

        def group(h, carry):
            for g in range(GRP):
                dsv = ds_ref[h, :, g * BLK:(g + 1) * BLK]
                for b in range(N_BUCKETS):
                    row_scr[GRP * h + g, b:b + 1, :] = jnp.sum(jnp.where(bk == b, dsv, 0.0), axis=0, keepdims=True)
            return carry

        lax.fori_loop(0, NKV, group, 0)
        for hd in range(NQ):
            o_ref[hd] = jnp.sum(row_scr[hd], axis=-1, keepdims=True)

    return pl.pallas_call(
        body, name="bias_grad", out_shape=S((NQ, N_BUCKETS, 1), F32),
        in_specs=[_VMEM, _VMEM], out_specs=_VMEM,
        scratch_shapes=[pltpu.VMEM((NQ, N_BUCKETS, BLK), F32)],
        compiler_params=_cp(None, 32),
    )(dsacc, bucket)


def _dh_gradx(dproj, wg, gx0, x, scale1p, p_in, p_out, p_pw):
    t = x.shape[0]
    tm = min(512, t)
    ni = t // tm
    bn = D // 2
    nn = D // bn

    def body(dp_ref, w_ref, gx_ref, x_ref, sc_ref, pi, po, pp, out_ref, st_ref, li, lo, lp, send_sems, recv_sems):
        n, i = pl.program_id(0), pl.program_id(1)

        @pl.when((n == 0) & (i == 0))
        def _():
            for cp in _chip_exchange_copies(pi, po, pp, li, lo, lp, send_sems, recv_sems):
                cp.start()

        @pl.when(i == 0)
        def _():
            st_ref[...] = jnp.zeros_like(st_ref)

        dh = None
        for j in range(NCHIP):
            part = lax.dot_general(dp_ref[:, j * NB:(j + 1) * NB], w_ref[j], (((1,), (1,)), ((), ())),
                                   preferred_element_type=F32)
            dh = part if dh is None else dh + part
        out_ref[...] = gx_ref[...] + dh * sc_ref[...]
        st_ref[0:1, :] += jnp.sum(dh, axis=0, keepdims=True)
        st_ref[1:2, :] += jnp.sum(dh * x_ref[...], axis=0, keepdims=True)

        @pl.when((n == nn - 1) & (i == ni - 1))
        def _():
            cps = _chip_exchange_copies(pi, po, pp, li, lo, lp, send_sems, recv_sems)
            for cp in cps:
                cp.wait_recv()
            for cp in cps:
                cp.wait_send()

    tile = pl.BlockSpec((tm, bn), lambda n, i: (i, n))
    return pl.pallas_call(
        body, name="dh_gradx", grid=(nn, ni),
        in_specs=[pl.BlockSpec((tm, D_IN), lambda n, i: (i, 0)), pl.BlockSpec((NCHIP, bn, NB), lambda n, i: (0, n, 0)),
                  tile, tile, pl.BlockSpec((1, bn), lambda n, i: (0, n)), _ANY, _ANY, _ANY],
        out_specs=[tile, pl.BlockSpec((8, bn), lambda n, i: (0, n)), _ANY, _ANY, _ANY],
        out_shape=[S((t, D), F32), S((8, D), F32), S((3, p_in.shape[0], NB), p_in.dtype),
                   S((3,) + p_out.shape[1:], p_out.dtype), S((3,) + p_pw.shape[1:], p_pw.dtype)],
        scratch_shapes=[pltpu.SemaphoreType.DMA((9,)), pltpu.SemaphoreType.DMA((9,))],
        compiler_params=_cp(("arbitrary", "arbitrary"), 60),
    )(dproj, wg, gx0, x, scale1p, p_in, p_out, p_pw)


def _atb(a_parts, b, bn, name, halves=None):
    bm = DA
    t = b.shape[0]
    n = b.shape[1]
    tk = min(2048, t)
    nk = t // tk
    na = len(a_parts)
    nj = n // bn

    def body(*refs):
        a_refs, b_ref, o_ref = refs[:na], refs[na], refs[na + 1 if halves is None else na + 2]
        mi, k = pl.program_id(0), pl.program_id(2)
        if halves is not None:
            g_ref, l_ref, send_sem, recv_sem = refs[na + 1], refs[na + 3], refs[na + 4], refs[na + 5]
            mx, my, c = _me()

            def exchange(hc):
                return pltpu.make_async_remote_copy(src_ref=g_ref.at[hc], dst_ref=l_ref, send_sem=send_sem, recv_sem=recv_sem,
                                                    device_id=(mx, my, 1 - c), device_id_type=MESH)

            first = (mi == 0) & (pl.program_id(1) == 0) & (k == 0)
            last = (mi == na - 1) & (pl.program_id(1) == nj - 1) & (k == nk - 1)

            @pl.when(first)
            def _():
                exchange(1 - c).start()

        for q in range(na):
            @pl.when(mi == q)
            def _(q=q):
                part = lax.dot_general(a_refs[q][...], b_ref[...], (((0,), (0,)), ((), ())), preferred_element_type=F32)

                @pl.when(k == 0)
                def _():
                    o_ref[...] = part

                @pl.when(k > 0)
                def _():
                    o_ref[...] += part

        if halves is not None:
            @pl.when(last)
            def _():
                exchange(c).wait_recv()
                exchange(1 - c).wait_send()

    a_spec = lambda q, cidx: pl.BlockSpec((tk, bm), lambda mi, j, k: (jnp.where(mi == q, k, 0), cidx))
    in_specs = [a_spec(q, cidx) for q, (_, cidx) in enumerate(a_parts)] + [pl.BlockSpec((tk, bn), lambda mi, j, k: (k, j))]
    out_spec = pl.BlockSpec((bm, bn), lambda mi, j, k: (mi, j))
    out_shape = S((na * bm, n), F32)
    args = [arr for arr, _ in a_parts] + [b]
    if halves is None:
        return pl.pallas_call(
            body, name=name, grid=(na, nj, nk), in_specs=in_specs, out_specs=out_spec, out_shape=out_shape,
            compiler_params=_cp(("parallel", "parallel", "arbitrary"), 56),
        )(*args)
    return pl.pallas_call(
        body, name=name, grid=(na, nj, nk), in_specs=in_specs + [_ANY], out_specs=[out_spec, _ANY],
        out_shape=[out_shape, S(halves.shape[1:], halves.dtype)],
        scratch_shapes=[pltpu.SemaphoreType.DMA, pltpu.SemaphoreType.DMA],
        compiler_params=_cp(("arbitrary", "arbitrary", "arbitrary"), 56),
    )(*args, halves)


def _sum8(parts):
    _, r, n = parts.shape

    def body(p_ref, o_ref):
        v = p_ref[0]
        for d in range(1, 8):
            v = v + p_ref[d]
        o_ref[...] = v

    return pl.pallas_call(body, name="sum8", out_shape=S((r, n), F32), in_specs=[_VMEM], out_specs=_VMEM,
                          compiler_params=_cp(None, 32))(parts)


def _adam_math(w, g, m, v):
    m = B1 * m + (1.0 - B1) * g
    v = B2 * v + (1.0 - B2) * (g * g)
    m_hat = m / (1.0 - B1 ** STEP)
    v_hat = v / (1.0 - B2 ** STEP)
    delta = -LR * (m_hat / (jnp.sqrt(v_hat) + EPS) + WD * w)
    return delta, m, v


def _adamw(w, g, m, v, name):
    r, n = w.shape
    tr = min(256, r)

    def body(w_ref, g_ref, m_ref, v_ref, d_ref, nm_ref, nv_ref):
        d_ref[...], nm_ref[...], nv_ref[...] = _adam_math(w_ref[...], g_ref[...], m_ref[...], v_ref[...])

    spec = pl.BlockSpec((tr, n), lambda i: (i, 0))
    return pl.pallas_call(
        body, name=name, grid=(r // tr,), in_specs=[spec] * 4, out_specs=[spec] * 3,
        out_shape=[S((r, n), F32)] * 3, compiler_params=_cp(("parallel",), 48),
    )(w, g, m, v)


def _adamw_ada(ct, dmod_p, w, m, v):
    r, n = w.shape
    tr = min(256, r)

    def body(ct_ref, dm_ref, w_ref, m_ref, v_ref, g_ref, d_ref, nm_ref, nv_ref):
        cv = ct_ref[...]
        g = jnp.dot(cv * _sig(cv), dm_ref[...], preferred_element_type=F32)
        g_ref[...] = g
        d_ref[...], nm_ref[...], nv_ref[...] = _adam_math(w_ref[...], g, m_ref[...], v_ref[...])

    spec = pl.BlockSpec((tr, n), lambda i: (i, 0))
    return pl.pallas_call(
        body, name="adamw_ada", grid=(r // tr,),
        in_specs=[pl.BlockSpec((tr, 8), lambda i: (i, 0)), pl.BlockSpec((8, n), lambda i: (0, 0)), spec, spec, spec],
        out_specs=[spec] * 4, out_shape=[S((r, n), F32)] * 4, compiler_params=_cp(("parallel",), 48),
    )(ct, dmod_p, w, m, v)


def _adamw_small(ws, gs, ms, vs):
    k = len(ws)

    def body(*refs):
        ins, outs = refs[:4 * k], refs[4 * k:]
        for i in range(k):
            d, nm, nv = _adam_math(ins[i][...], ins[k + i][...], ins[2 * k + i][...], ins[3 * k + i][...])
            outs[i][...] = d
            outs[k + i][...] = nm
            outs[2 * k + i][...] = nv

    shapes = [S(w.shape, F32) for w in ws]
    return pl.pallas_call(body, name="adamw_small", out_shape=shapes * 3, in_specs=[_VMEM] * (4 * k),
                          out_specs=[_VMEM] * (3 * k), compiler_params=_cp(None, 32))(*ws, *gs, *ms, *vs)


def _bucket_map():
    qi = jnp.arange(BLK, dtype=jnp.int32)[None, :]
    kj = jnp.arange(2 * BLK, dtype=jnp.int32)[:, None]
    dist = qi + BLK - kj
    dd = jnp.maximum(dist, 0)
    max_exact = N_BUCKETS // 2
    dfl = jnp.maximum(dd, 1).astype(F32)
    large = max_exact + (jnp.log(dfl / max_exact) / math.log(MAX_DIST / max_exact) * (N_BUCKETS - max_exact)).astype(jnp.int32)
    large = jnp.minimum(large, N_BUCKETS - 1)
    bucket = jnp.where(dd < max_exact, dd, large)
    return jnp.where((dist >= 0) & (dist < BLK), bucket, -1).astype(jnp.int32)


def _pad_rows(a, rows):
    return jnp.pad(a, ((0, rows - a.shape[0]), (0, 0)))


def kernel(x, c, w_ada, b_ada, w_in, rel_bias, sinks, conv_w, conv_b, conv_ln_g, conv_ln_b, w_pw, b_pw, w_out, ln_g, ln_b, loss_target, m_w_ada, m_b_ada, m_w_in, m_rel_bias, m_sinks, m_conv_w, m_conv_b, m_conv_ln_g, m_conv_ln_b, m_w_pw, m_b_pw, m_w_out, m_ln_g, m_ln_b, v_w_ada, v_b_ada, v_w_in, v_rel_bias, v_sinks, v_conv_w, v_conv_b, v_conv_ln_g, v_conv_ln_b, v_w_pw, v_b_pw, v_w_out, v_ln_g, v_ln_b):
    mx, my, mc = _me()
    chip = 2 * mx + my
    dev = 2 * chip + mc
    t = x.shape[1]
    x2 = x.reshape(t, D)
    tgt = loss_target.reshape(t, D)
    n_ada = w_ada.shape[2]
    cw_cols = conv_w.shape[2]

    pack0 = jnp.concatenate([_pad_rows(c, 8), _pad_rows(_pad_rows(conv_w[0], 32).reshape(-1, D), 8)], axis=0)
    g0 = _allgather8(pack0, "gather_c_convw").reshape(8, 16, D)
    c_all = g0[:, 0, :]
    cw_rows = 32 * cw_cols // D
    cw_full = jnp.concatenate([g0[2 * q, 8:8 + cw_rows, :].reshape(32, cw_cols) for q in range(NCHIP)], axis=1)[:CW]
    cwb = jnp.broadcast_to(cw_full[:, None, :], (CW, 8, DA))

    b_ada_p = lax.dynamic_slice(b_ada, (0, chip * n_ada), (1, n_ada))
    mod_part = _ada_fwd(c_all, w_ada[0], b_ada_p)
    mod_all = _allgather8(mod_part, "gather_mod").reshape(8, 8, n_ada)
    mod = jnp.concatenate([lax.dynamic_slice(mod_all[2 * q], (dev, 0), (1, n_ada)) for q in range(NCHIP)], axis=1)
    shift, scale, gate = mod[:, 0:D], mod[:, D:2 * D], mod[:, 2 * D:3 * D]
    scale1p = 1.0 + scale

    bucket = _bucket_map()
    sel_chip = jnp.reshape(chip, (1,)).astype(jnp.int32)
    sel = jnp.reshape(mc, (1,)).astype(jnp.int32)
    proj, h, wg_in, wg_out, wg_pw = _inproj_gather(
        x2, scale1p, shift, w_in[0].astype(BF16), w_out[0].astype(BF16), w_pw[0].astype(BF16), sel_chip)
    w_out_f = wg_out.reshape(D, D)
    w_pw_f = wg_pw.reshape(DA, DA)
    ya, attn_o, lse = _attn_fwd(proj, rel_bias, sinks, bucket)
    uc, pw, yc = _conv_fwd(proj, cwb, conv_b, conv_ln_g, conv_ln_b, w_pw_f, b_pw)
    gx0, dy, st_out = _outproj_loss(ya, yc, w_out_f, x2, tgt, gate, ln_g, ln_b)

    dproj = lax.empty((t, D_IN), BF16)
    dao, dpw, dproj = _dycat_gates(dy, w_out_f, attn_o, pw, proj, dproj)
    duc, sw, st_conv = _conv_bwd_ln(dpw, w_pw_f, uc, conv_ln_g, conv_ln_b)
    gw_conv, dproj = _conv_bwd_dw(duc, proj, cwb, dproj)
    dkv_cur, dkv_prev, dsacc, dsk, dproj = _attn_bwd(proj, dao, lse, rel_bias, sinks, bucket, dproj)
    dproj = _dkv_fix(dkv_cur, dkv_prev, dproj)
    g_rel = _bias_grad(dsacc, bucket).reshape(NQ, N_BUCKETS).T
    r_out, r_pw = D // NCHIP // 2, DA // NCHIP // 2
    gp_in = _atb([(h, 0), (h, 1)], dproj, NB, "grad_w_in")
    gi3 = gp_in.reshape(2, D // 2, D_IN)
    gp_out, l_in = _atb([(ya, 0), (yc, 0)], dy, 1024, "grad_w_out", halves=gi3)
    gp_pw = _atb([(sw, 0)], dpw, 1024, "grad_w_pw")
    go3, gp3 = gp_out.reshape(8, r_out, D), gp_pw.reshape(8, r_pw, DA)
    l_out, l_pw = _pair_exchange(go3, gp3)
    pi32, pi16 = _pair_sum(gi3, l_in.reshape(1, D // 2, D_IN), sel, "pair_sum_in")
    po32, po16 = _pair_sum(go3, l_out, sel, "pair_sum_out")
    pp32, pp16 = _pair_sum(gp3, l_pw, sel, "pair_sum_pw")
    grad_x, st_in, rc_in, rc_out, rc_pw = _dh_gradx(dproj, wg_in, gx0, x2, scale1p, pi16[0], po16, pp16)
    own_in = lax.dynamic_slice(pi32[0], (0, chip * NB), (D // 2, NB))
    own_out = lax.dynamic_index_in_dim(po32, chip, 0, keepdims=False)
    own_pw = lax.dynamic_index_in_dim(pp32, chip, 0, keepdims=False)
    h_in = _chip_sum(own_in, rc_in, sel, "chip_sum_in")
    h_out = _chip_sum(own_out, rc_out, sel, "chip_sum_out")
    h_pw = _chip_sum(own_pw, rc_pw, sel, "chip_sum_pw")
    f_in, f_out, f_pw = _pair_share(h_in, h_out, h_pw)
    g_w_in = f_in.reshape(D, NB)
    g_w_out = f_out.reshape(D // NCHIP, D)
    g_w_pw = f_pw.reshape(DA // NCHIP, DA)

    dmod = jnp.concatenate([st_in[0:1], st_in[1:2], st_out[2:3]], axis=1)
    loss_row = st_out[3:4, 0:1]
    small = jnp.concatenate([
        dmod, st_out[0:1], st_out[1:2],
        st_conv[0:1], st_conv[1:2], st_conv[2:3], st_conv[3:4],
        g_rel.reshape(1, N_BUCKETS * NQ), dsk.reshape(NKV, GRP, BLK)[:, :, 0].reshape(1, NQ), loss_row,
        gw_conv[:CW].reshape(1, CW * DA)], axis=1)
    n_small = small.shape[1]
    rows_small = -(-n_small // (8 * D)) * 8
    small = jnp.pad(small, ((0, 0), (0, rows_small * D - n_small))).reshape(rows_small, D)
    parts = _allgather8(small, "gather_small").reshape(8, rows_small, D)
    tot = _sum8(parts).reshape(1, rows_small * D)
    dmod_all = parts.reshape(8, rows_small * D)[:, 0:3 * D]

    o = 3 * D
    def take(nn):
        nonlocal o
        v = tot[:, o:o + nn]
        o += nn
        return v
    g_b_ada = tot[:, 0:3 * D]
    g_ln_g, g_ln_b = take(D), take(D)
    g_cln_g, g_cln_b, g_conv_b, g_b_pw = take(DA), take(DA), take(DA), take(DA)
    g_rel_bias = take(N_BUCKETS * NQ).reshape(N_BUCKETS, NQ)
    g_sinks = take(NQ)
    loss = take(1).reshape(())
    g_conv_w_full = take(CW * DA).reshape(CW, DA)
    g_conv_w = lax.dynamic_slice(g_conv_w_full, (0, chip * cw_cols), (CW, cw_cols))

    dmod_p = lax.dynamic_slice(dmod_all, (0, chip * n_ada), (8, n_ada))
    g_w_ada, d_w_ada, nm_w_ada, nv_w_ada = _adamw_ada(c_all.T, dmod_p, w_ada[0], m_w_ada[0], v_w_ada[0])
    d_w_in, nm_w_in, nv_w_in = _adamw(w_in[0], g_w_in, m_w_in[0], v_w_in[0], "adamw_in")
    d_w_out, nm_w_out, nv_w_out = _adamw(w_out[0], g_w_out, m_w_out[0], v_w_out[0], "adamw_out")
    d_w_pw, nm_w_pw, nv_w_pw = _adamw(w_pw[0], g_w_pw, m_w_pw[0], v_w_pw[0], "adamw_pw")
    small_w = [b_ada, rel_bias, sinks, conv_w[0], conv_b, conv_ln_g, conv_ln_b, b_pw, ln_g, ln_b]
    small_g = [g_b_ada, g_rel_bias, g_sinks, g_conv_w, g_conv_b, g_cln_g, g_cln_b, g_b_pw, g_ln_g, g_ln_b]
    small_m = [m_b_ada, m_rel_bias, m_sinks, m_conv_w[0], m_conv_b, m_conv_ln_g, m_conv_ln_b, m_b_pw, m_ln_g, m_ln_b]
    small_v = [v_b_ada, v_rel_bias, v_sinks, v_conv_w[0], v_conv_b, v_conv_ln_g, v_conv_ln_b, v_b_pw, v_ln_g, v_ln_b]
    res = _adamw_small(small_w, small_g, small_m, small_v)
    ns = len(small_w)
    d_s, nm_s, nv_s = res[:ns], res[ns:2 * ns], res[2 * ns:]

    def ordered(w_ada_, w_in_, w_pw_, w_out_, sm):
        b_ada_, rel_, sinks_, conv_w_, conv_b_, cln_g_, cln_b_, b_pw_, ln_g_, ln_b_ = sm
        return (w_ada_[None], b_ada_, w_in_[None], rel_, sinks_, conv_w_[None], conv_b_, cln_g_, cln_b_,
                w_pw_[None], b_pw_, w_out_[None], ln_g_, ln_b_)

    grads = ordered(g_w_ada, g_w_in, g_w_pw, g_w_out, small_g)
    deltas = ordered(d_w_ada, d_w_in, d_w_pw, d_w_out, d_s)
    new_m = ordered(nm_w_ada, nm_w_in, nm_w_pw, nm_w_out, nm_s)
    new_v = ordered(nv_w_ada, nv_w_in, nv_w_pw, nv_w_out, nv_s)
    return (loss, grad_x.reshape(1, t, D), *grads, *deltas, *new_m, *new_v)
```

```python
import functools
import math

import jax
import jax.numpy as jnp
from jax import lax
from jax.experimental import pallas as pl
from jax.experimental.pallas import tpu as pltpu

F32, BF16 = jnp.float32, jnp.bfloat16
S = jax.ShapeDtypeStruct
MESH = pl.DeviceIdType.MESH

D = 2048
DA = 1024
HD = 64
NQ, NKV, GRP = 16, 4, 4
BLK = 128
CW = 31
HALO = 32
D_IN = 5632
NCHIP = 4
NB = D_IN // NCHIP
N_BUCKETS, MAX_DIST = 32, 128
LN_EPS = 1e-5
ALPHA = 2.0 ** 0.25
SCALE = HD ** -0.5
NEG = -1e30
LR, B1, B2, EPS, WD, STEP = 0.001, 0.9, 0.999, 1e-08, 0.01, 10

_VMEM = pl.BlockSpec(memory_space=pltpu.VMEM)
_SMEM = pl.BlockSpec(memory_space=pltpu.SMEM)
_ANY = pl.BlockSpec(memory_space=pl.ANY)


def _cp(sem=None, vmem_mb=None):
    kw = {}
    if sem is not None:
        kw["dimension_semantics"] = sem
    if vmem_mb is not None:
        kw["vmem_limit_bytes"] = vmem_mb * 1024 * 1024
    return pltpu.CompilerParams(**kw)


def _sig(v):
    return jax.nn.sigmoid(v)


def _dsilu(g, sg):
    return sg * (1.0 + g * (1.0 - sg))


def _me():
    return lax.axis_index("x"), lax.axis_index("y"), lax.axis_index("c")


def _allgather8(x_shard, name):
    m_per, n = x_shard.shape

    def body(x_ref, out_ref, send_sems, recv_sems, local_sem):
        x, y, c = _me()
        me, sibling = (x, y, c), (x, y, 1 - c)
        chips = [(1 - x, y), (x, 1 - y), (1 - x, 1 - y)]

        def rows(px, py, pc):
            return out_ref.at[pl.ds((4 * px + 2 * py + pc) * m_per, m_per), :]

        def copy(k, block, to, src=None):
            return pltpu.make_async_remote_copy(
                src_ref=rows(*block) if src is None else src, dst_ref=rows(*block),
                send_sem=send_sems.at[k], recv_sem=recv_sems.at[k], device_id=to, device_id_type=MESH)

        mine = pltpu.make_async_copy(x_ref, rows(*me), local_sem)
        mine.start()
        first = [copy(0, me, sibling, src=x_ref)]
        first += [copy(1 + j, me, (*chip, c), src=x_ref) for j, chip in enumerate(chips)]
        for cp in first:
            cp.start()
        passed = [copy(4 + j, (*chip, c), sibling) for j, chip in enumerate(chips)]
        for j, chip in enumerate(chips):
            copy(1 + j, (*chip, c), me).wait_recv()
            passed[j].start()
        copy(0, sibling, me).wait_recv()
        for j, chip in enumerate(chips):
            copy(4 + j, (*chip, 1 - c), me).wait_recv()
        for cp in first + passed:
            cp.wait_send()
        mine.wait()

    return pl.pallas_call(
        body, name=name, out_shape=S((8 * m_per, n), x_shard.dtype),
        in_specs=[_VMEM], out_specs=_VMEM,
        scratch_shapes=[pltpu.SemaphoreType.DMA((7,)), pltpu.SemaphoreType.DMA((7,)), pltpu.SemaphoreType.DMA],
    )(x_shard)


def _inproj_gather(x, scale1p, shift, wi, wo, wp, chip_id):
    t = x.shape[0]
    tm = min(512, t)
    ni = t // tm
    shards = (wi, wo, wp)
    halves = [s.shape[0] // 2 for s in shards]

    def body(cid_ref, x_ref, sc_ref, sh_ref, wi_ref, wo_ref, wp_ref, proj_ref, h_ref, gi_ref, go_ref, gp_ref,
             wbuf, obuf, pbuf, hb_scr, send_sems, recv_sems, ld_sems, st_sems):
        jj, i = pl.program_id(0), pl.program_id(1)
        mx, my, c = _me()
        p = 2 * mx + my
        sibling = (mx, my, 1 - c)
        srcs, dsts, bufs = (wi_ref, wo_ref, wp_ref), (gi_ref, go_ref, gp_ref), (wbuf, obuf, pbuf)
        chips = {1: (mx, 1 - my), 2: (1 - mx, my), 3: (1 - mx, 1 - my)}

        def half(ref, w, hc):
            return ref.at[pl.ds(hc * halves[w], halves[w])]

        def copy(k, src, dst, to):
            return pltpu.make_async_remote_copy(src_ref=src, dst_ref=dst, send_sem=send_sems.at[k],
                                                recv_sem=recv_sems.at[k], device_id=to, device_id_type=MESH)

        def sent(w, m):
            return copy(3 * w + m - 1, half(srcs[w], w, c), half(dsts[w].at[p], w, c), (*chips[m], c))

        def landed(w, m):
            blk = half(dsts[w].at[jnp.bitwise_xor(p, m)], w, c)
            return copy(3 * w + m - 1, blk, blk, (*chips[m], c))

        def passed(w, m, hc):
            blk = half(dsts[w].at[jnp.bitwise_xor(p, m)], w, hc)
            return copy(9 + 3 * w + m - 1, blk, blk, sibling)

        def vm(w, slot):
            return bufs[w].at[slot] if w == 0 else bufs[w]

        def load(w, src, slot=0):
            return pltpu.make_async_copy(src, vm(w, slot), ld_sems.at[w])

        def store(w):
            return pltpu.make_async_copy(vm(w, 0), dsts[w].at[p], st_sems.at[w])

        def block_load(m):
            return load(0, gi_ref.at[jnp.bitwise_xor(p, m)], m % 2)

        @pl.when((jj == 0) & (i == 0))
        def _():
            sent(0, 1).start()
            sent(0, 2).start()
            for w in range(3):
                load(w, srcs[w]).start()
            for w in range(3):
                load(w, srcs[w]).wait()
                store(w).start()

        for m in (1, 2, 3):
            @pl.when((jj == m) & (i == 0))
            def _(m=m):
                if m == 1:
                    sent(0, 3).start()
                    store(0).wait()
                if m == 2:
                    for w in (1, 2):
                        for mm in (1, 2, 3):
                            sent(w, mm).start()
                block_load(m).wait()

        hb_scr[...] = (x_ref[...] * sc_ref[...] + sh_ref[...]).astype(BF16)

        @pl.when(jj == 0)
        def _():
            h_ref[...] = hb_scr[...]

        proj_ref[...] = jnp.dot(hb_scr[...], wbuf[jj % 2], preferred_element_type=F32).astype(BF16)

        mid = ni // 2

        @pl.when((jj == 0) & (i == ni - 1))
        def _():
            for m in (1, 2):
                landed(0, m).wait_recv()
                passed(0, m, c).start()

        @pl.when((jj == 2) & (i == mid))
        def _():
            landed(0, 3).wait_recv()
            passed(0, 3, c).start()

        @pl.when((jj == NCHIP - 1) & (i == mid))
        def _():
            for w in (1, 2):
                for m in (1, 2, 3):
                    landed(w, m).wait_recv()
                    passed(w, m, c).start()

        for m in (1, 2, 3):
            @pl.when((jj == m - 1) & (i == ni - 1))
            def _(m=m):
                passed(0, m, 1 - c).wait_recv()
                block_load(m).start()

        @pl.when((jj == NCHIP - 1) & (i == ni - 1))
        def _():
            for w in (1, 2):
                for m in (1, 2, 3):
                    passed(w, m, 1 - c).wait_recv()
            for w in range(3):
                for m in (1, 2, 3):
                    sent(w, m).wait_send()
                    passed(w, m, c).wait_send()
            store(1).wait()
            store(2).wait()

    row = pl.BlockSpec((1, D), lambda jj, i, s: (0, 0))
    return pl.pallas_call(
        body, name="inproj_gather",
        grid_spec=pltpu.PrefetchScalarGridSpec(
            num_scalar_prefetch=1, grid=(NCHIP, ni),
            in_specs=[pl.BlockSpec((tm, D), lambda jj, i, s: (i, 0)), row, row, _ANY, _ANY, _ANY],
            out_specs=[pl.BlockSpec((tm, NB), lambda jj, i, s: (i, jnp.bitwise_xor(s[0], jj))),
                       pl.BlockSpec((tm, D), lambda jj, i, s: (jnp.where(jj == 0, i, ni - 1), 0)),
                       _ANY, _ANY, _ANY],
            scratch_shapes=[pltpu.VMEM((2,) + wi.shape, BF16), pltpu.VMEM(wo.shape, BF16), pltpu.VMEM(wp.shape, BF16),
                            pltpu.VMEM((tm, D), BF16),
                            pltpu.SemaphoreType.DMA((18,)), pltpu.SemaphoreType.DMA((18,)),
                            pltpu.SemaphoreType.DMA((3,)), pltpu.SemaphoreType.DMA((3,))]),
        out_shape=[S((t, D_IN), BF16), S((t, D), BF16)] + [S((NCHIP,) + s.shape, s.dtype) for s in shards],
        compiler_params=_cp(("arbitrary", "arbitrary"), 48),
    )(chip_id, x, scale1p, shift, wi, wo, wp)


def _pair_exchange(g_out, g_pw):
    def body(go, gp, lo, lp, send_sems, recv_sems):
        x, y, c = _me()
        sibling = (x, y, 1 - c)
        copies = []
        for j in range(NCHIP):
            copies.append((go.at[2 * j + 1 - c], lo.at[j]))
        for j in range(NCHIP):
            copies.append((gp.at[2 * j + 1 - c], lp.at[j]))
        cps = [pltpu.make_async_remote_copy(src_ref=s_, dst_ref=d_, send_sem=send_sems.at[k], recv_sem=recv_sems.at[k],
                                            device_id=sibling, device_id_type=MESH) for k, (s_, d_) in enumerate(copies)]
        for cp in cps:
            cp.start()
        for cp in cps:
            cp.wait_recv()
        for cp in cps:
            cp.wait_send()

    n = 2 * NCHIP
    return pl.pallas_call(
        body, name="grad_pair_exchange",
        out_shape=[S((NCHIP,) + g_out.shape[1:], g_out.dtype), S((NCHIP,) + g_pw.shape[1:], g_pw.dtype)],
        in_specs=[_ANY] * 2, out_specs=[_ANY] * 2,
        scratch_shapes=[pltpu.SemaphoreType.DMA((n,)), pltpu.SemaphoreType.DMA((n,))],
    )(g_out, g_pw)


def _chip_exchange_copies(pi, po, pp, li, lo, lp, send_sems, recv_sems):
    x, y, c = _me()
    chips = [(1 - x, y), (x, 1 - y), (1 - x, 1 - y)]
    cps = []
    for j, chip in enumerate(chips):
        q = 2 * chip[0] + chip[1]
        pairs = ((pi.at[:, pl.ds(q * NB, NB)], li.at[j]), (po.at[q], lo.at[j]), (pp.at[q], lp.at[j]))
        for w, (src, dst) in enumerate(pairs):
            k = 3 * j + w
            cps.append(pltpu.make_async_remote_copy(src_ref=src, dst_ref=dst, send_sem=send_sems.at[k],
                                                    recv_sem=recv_sems.at[k], device_id=(*chip, c), device_id_type=MESH))
    return cps


def _pair_share(f_in, f_out, f_pw):
    bufs = (f_in, f_out, f_pw)

    def body(ai, ao, ap, fi, fo, fp, send_sems, recv_sems):
        x, y, c = _me()
        sibling = (x, y, 1 - c)

        def copy(w, ref, hc):
            return pltpu.make_async_remote_copy(src_ref=ref.at[hc], dst_ref=ref.at[hc], send_sem=send_sems.at[w],
                                                recv_sem=recv_sems.at[w], device_id=sibling, device_id_type=MESH)

        cps = [copy(w, ref, c) for w, ref in enumerate((fi, fo, fp))]
        for cp in cps:
            cp.start()
        for w, ref in enumerate((fi, fo, fp)):
            copy(w, ref, 1 - c).wait_recv()
        for cp in cps:
            cp.wait_send()

    return pl.pallas_call(
        body, name="grad_pair_share",
        out_shape=[S(b.shape, b.dtype) for b in bufs],
        in_specs=[_ANY] * 3, out_specs=[_ANY] * 3, input_output_aliases={0: 0, 1: 1, 2: 2},
        scratch_shapes=[pltpu.SemaphoreType.DMA((3,)), pltpu.SemaphoreType.DMA((3,))],
    )(*bufs)


def _pair_sum(g, l, sel, name):
    n, r, ccols = l.shape
    tr = min(256 if ccols <= D else 128, r)

    def body(sel_ref, g_ref, l_ref, o32_ref, o16_ref):
        v = g_ref[...] + l_ref[...]
        o32_ref[...] = v
        o16_ref[...] = v.astype(BF16)

    spec_l = pl.BlockSpec((None, tr, ccols), lambda j, i, s: (j, i, 0))
    return pl.pallas_call(
        body, name=name,
        grid_spec=pltpu.PrefetchScalarGridSpec(
            num_scalar_prefetch=1, grid=(n, r // tr),
            in_specs=[pl.BlockSpec((None, tr, ccols), lambda j, i, s: (2 * j + s[0], i, 0)), spec_l],
            out_specs=[spec_l, spec_l]),
        out_shape=[S(l.shape, F32), S(l.shape, BF16)],
        compiler_params=_cp(("parallel", "parallel"), 48),
    )(sel, g, l)


def _chip_sum(own, recv, sel, name):
    r, ccols = own.shape
    tr = min(256, r)

    def body(sel_ref, o_ref, r_ref, out_ref):
        v = o_ref[...]
        for j in range(3):
            v = v + r_ref[j].astype(F32)
        out_ref[...] = v

    return pl.pallas_call(
        body, name=name,
        grid_spec=pltpu.PrefetchScalarGridSpec(
            num_scalar_prefetch=1, grid=(r // tr,),
            in_specs=[pl.BlockSpec((tr, ccols), lambda i, s: (i, 0)), pl.BlockSpec((3, tr, ccols), lambda i, s: (0, i, 0))],
            out_specs=pl.BlockSpec((None, tr, ccols), lambda i, s: (s[0], i, 0))),
        out_shape=S((2, r, ccols), F32),
        compiler_params=_cp(("parallel",), 48),
    )(sel, own, recv)


def _ada_fwd(c_all, w_ada, b_ada_p):
    n = w_ada.shape[1]
    tn = 512

    def body(c_ref, w_ref, b_ref, o_ref):
        cv = c_ref[...]
        ca = cv * _sig(cv)
        o_ref[...] = jnp.dot(ca, w_ref[...], preferred_element_type=F32) + b_ref[...]

    return pl.pallas_call(
        body, name="ada_fwd", grid=(n // tn,),
        in_specs=[pl.BlockSpec((8, D), lambda j: (0, 0)), pl.BlockSpec((D, tn), lambda j: (0, j)),
                  pl.BlockSpec((1, tn), lambda j: (0, j))],
        out_specs=pl.BlockSpec((8, tn), lambda j: (0, j)),
        out_shape=S((8, n), F32),
        compiler_params=_cp(("parallel",), 32),
    )(c_all, w_ada, b_ada_p)


def _build_bias(rel_ref, bk_ref, bias_ref):
    bk = bk_ref[...]
    kj = lax.broadcasted_iota(jnp.int32, (2 * BLK, BLK), 0)
    for hd in range(NQ):
        acc = jnp.full((2 * BLK, BLK), NEG, F32)
        for b in range(N_BUCKETS):
            acc = jnp.where(bk == b, rel_ref[b, hd], acc)
        lanes = slice((hd % GRP) * BLK, (hd % GRP + 1) * BLK)
        bias_ref[1, hd // GRP, :, lanes] = acc
        bias_ref[0, hd // GRP, :, lanes] = jnp.where(kj < BLK, NEG, acc)


def _group_rows(ref, h):
    return jnp.concatenate([ref[:, (GRP * h + g) * HD:(GRP * h + g + 1) * HD] for g in range(GRP)], axis=0)


def _sink_row(sink_ref, h):
    return jnp.concatenate([jnp.full((1, BLK), sink_ref[0, GRP * h + g], F32) for g in range(GRP)], axis=1)


def _attn_fwd(proj, rel_bias, sinks, bucket):
    t = proj.shape[0]

    def body(rel_ref, sink_ref, bk_ref, q_ref, kvc_ref, kvp_ref, glo_ref, ghi_ref, ya_ref, o_ref, lse_ref, bias_ref):
        n = pl.program_id(0)

        @pl.when(n == 0)
        def _():
            _build_bias(rel_ref, bk_ref, bias_ref)

        tbl = jnp.where(n == 0, 0, 1)
        kv = jnp.concatenate([kvp_ref[...], kvc_ref[...]], axis=0)
        for h in range(NKV):
            k_h = kv[:, h * HD:(h + 1) * HD]
            v_h = kv[:, NKV * HD + h * HD:NKV * HD + (h + 1) * HD]
            q4 = _group_rows(q_ref, h) * SCALE
            s = lax.dot_general(k_h, q4, (((1,), (1,)), ((), ())), preferred_element_type=F32) + bias_ref[tbl, h]
            sink = _sink_row(sink_ref, h)
            m = jnp.maximum(jnp.max(s, axis=0, keepdims=True), sink)
            p = jnp.exp(s - m)
            l = jnp.sum(p, axis=0, keepdims=True) + jnp.exp(sink - m)
            pn = (p * (1.0 / l)).astype(BF16)
            o4 = lax.dot_general(pn, v_h, (((0,), (0,)), ((), ())), preferred_element_type=F32)
            lse_ref[h:h + 1, :] = m + jnp.log(l)
            for g in range(GRP):
                hd = GRP * h + g
                cols = slice(hd * HD, (hd + 1) * HD)
                o = o4[g * BLK:(g + 1) * BLK]
                g_ref = glo_ref if hd < NQ // 2 else ghi_ref
                gc = slice((hd % (NQ // 2)) * HD, (hd % (NQ // 2) + 1) * HD)
                gt = g_ref[:, gc].astype(F32)
                o_ref[:, cols] = o.astype(BF16)
                ya_ref[:, cols] = (o * (gt * _sig(gt))).astype(BF16)

    blk = lambda w, cidx: pl.BlockSpec((BLK, w), lambda n: (n, cidx))
    return pl.pallas_call(
        body, name="attn_fwd", grid=(t // BLK,),
        in_specs=[_SMEM, _SMEM, pl.BlockSpec((2 * BLK, BLK), lambda n: (0, 0)),
                  blk(DA, 0), blk(512, 2), pl.BlockSpec((BLK, 512), lambda n: (jnp.maximum(n - 1, 0), 2)),
                  blk(512, 3), blk(512, 4)],
        out_specs=[blk(DA, 0), blk(DA, 0), pl.BlockSpec((None, NKV, GRP * BLK), lambda n: (n, 0, 0))],
        out_shape=[S((t, DA), BF16), S((t, DA), BF16), S((t // BLK, NKV, GRP * BLK), F32)],
        scratch_shapes=[pltpu.VMEM((2, NKV, 2 * BLK, GRP * BLK), F32)],
        compiler_params=_cp(("arbitrary",), 32),
    )(rel_bias, sinks, bucket, proj, proj, proj, proj, proj)


def _conv_rows(t):
    return min(256, t)


def _glu_into(u_scr, i, a_refs, b_refs, ah_refs, bh_refs, tc):
    for cc in range(DA // 128):
        half, lc = cc // 4, slice((cc % 4) * 128, (cc % 4 + 1) * 128)
        uh = ah_refs[half][:, lc].astype(F32) * _sig(bh_refs[half][:, lc].astype(F32))
        u_scr[cc, 0:HALO, :] = jnp.where(i == 0, 0.0, uh)
        u_scr[cc, HALO:HALO + tc, :] = a_refs[half][:, lc].astype(F32) * _sig(b_refs[half][:, lc].astype(F32))


def _conv_fwd(proj, cwb, conv_b, ln_g, ln_b, wpw, b_pw):
    t = proj.shape[0]
    tc = _conv_rows(t)
    rc = min(128, tc)

    def body(alo, ahi, blo, bhi, alo_h, ahi_h, blo_h, bhi_h, glo, ghi, cw_ref, cb_ref, lg_ref, lb_ref, wpw_ref, bpw_ref,
             uc_ref, pw_ref, yc_ref, u_scr):
        i = pl.program_id(0)
        _glu_into(u_scr, i, (alo, ahi), (blo, bhi), (alo_h, ahi_h), (blo_h, bhi_h), tc)

        def rows(r, carry):
            r0 = pl.multiple_of(r * rc, rc)
            for cc in range(DA // 128):
                lanes = pl.ds(cc * 128, 128)
                acc = jnp.zeros((rc // 8, 8, 128), F32)
                for k in range(CW):
                    u = u_scr[cc, pl.ds(r0 + (HALO - CW + 1) + k, rc), :].reshape(rc // 8, 8, 128)
                    acc = acc + u * cw_ref[k, :, lanes]
                uc_ref[pl.ds(r0, rc), lanes] = acc.reshape(rc, 128) + cb_ref[:, lanes]
            return carry

        lax.fori_loop(0, tc // rc, rows, 0)

        uc = uc_ref[...]
        mu = jnp.mean(uc, axis=-1, keepdims=True)
        xc = uc - mu
        rstd = lax.rsqrt(jnp.mean(xc * xc, axis=-1, keepdims=True) + LN_EPS)
        ln = xc * rstd * lg_ref[...] + lb_ref[...]
        sw = (ln * _sig(ln)).astype(BF16)
        pw = jnp.dot(sw, wpw_ref[...], preferred_element_type=F32) + bpw_ref[...]
        pw_ref[...] = pw.astype(BF16)
        gt = jnp.concatenate([glo[...], ghi[...]], axis=1).astype(F32)
        yc_ref[...] = (pw * (gt * _sig(gt))).astype(BF16)

    nh = tc // HALO
    cur = lambda cidx: pl.BlockSpec((tc, 512), lambda i: (i, cidx))
    halo = lambda cidx: pl.BlockSpec((HALO, 512), lambda i: (jnp.maximum(i * nh - 1, 0), cidx))
    row = pl.BlockSpec((1, DA), lambda i: (0, 0))
    full = pl.BlockSpec((tc, DA), lambda i: (i, 0))
    return pl.pallas_call(
        body, name="conv_fwd", grid=(t // tc,),
        in_specs=[cur(5), cur(6), cur(7), cur(8), halo(5), halo(6), halo(7), halo(8), cur(9), cur(10),
                  pl.BlockSpec((CW, 8, DA), lambda i: (0, 0, 0)), row, row, row,
                  pl.BlockSpec((DA, DA), lambda i: (0, 0)), row],
        out_specs=[full, full, full],
        out_shape=[S((t, DA), F32), S((t, DA), BF16), S((t, DA), BF16)],
        scratch_shapes=[pltpu.VMEM((DA // 128, HALO + tc, 128), F32)],
        compiler_params=_cp(("arbitrary",), 48),
    )(*([proj] * 10), cwb, conv_b, ln_g, ln_b, wpw, b_pw)


def _outproj_loss(ya, yc, w_out, x, target, gate, ln_g, ln_b):
    t = x.shape[0]
    tm = min(256, t)

    def body(ya_ref, yc_ref, w_ref, x_ref, t_ref, g_ref, lg_ref, lb_ref, gx_ref, dy_ref, st_ref):
        @pl.when(pl.program_id(0) == 0)
        def _():
            st_ref[...] = jnp.zeros_like(st_ref)

        y = jnp.dot(ya_ref[...], w_ref[0:DA, :], preferred_element_type=F32)
        y = y + jnp.dot(yc_ref[...], w_ref[DA:2 * DA, :], preferred_element_type=F32)
        gate_v = g_ref[...]
        z = ALPHA * x_ref[...] + gate_v * y
        mu = jnp.mean(z, axis=-1, keepdims=True)
        zc = z - mu
        rstd = lax.rsqrt(jnp.mean(zc * zc, axis=-1, keepdims=True) + LN_EPS)
        zh = zc * rstd
        diff = zh * lg_ref[...] + lb_ref[...] - t_ref[...]
        dout = diff * (1.0 / D)
        dzh = dout * lg_ref[...]
        m1 = jnp.mean(dzh, axis=-1, keepdims=True)
        m2 = jnp.mean(dzh * zh, axis=-1, keepdims=True)
        dz = rstd * (dzh - m1 - zh * m2)
        gx_ref[...] = ALPHA * dz
        dy_ref[...] = (dz * gate_v).astype(BF16)
        st_ref[0:1, :] += jnp.sum(dout * zh, axis=0, keepdims=True)
        st_ref[1:2, :] += jnp.sum(dout, axis=0, keepdims=True)
        st_ref[2:3, :] += jnp.sum(dz * y, axis=0, keepdims=True)
        st_ref[3:4, :] += jnp.sum(diff * diff, axis=0, keepdims=True) * (0.5 / D)

        @pl.when(pl.program_id(0) == t // tm - 1)
        def _():
            st_ref[3:4, :] = jnp.broadcast_to(jnp.sum(st_ref[3:4, :], axis=-1, keepdims=True), (1, D))

    row = pl.BlockSpec((1, D), lambda i: (0, 0))
    half = pl.BlockSpec((tm, DA), lambda i: (i, 0))
    full = pl.BlockSpec((tm, D), lambda i: (i, 0))
    return pl.pallas_call(
        body, name="outproj_loss", grid=(t // tm,),
        in_specs=[half, half, pl.BlockSpec((D, D), lambda i: (0, 0)), full, full, row, row, row],
        out_specs=[full, full, pl.BlockSpec((8, D), lambda i: (0, 0))],
        out_shape=[S((t, D), F32), S((t, D), BF16), S((8, D), F32)],
        compiler_params=_cp(("arbitrary",), 56),
    )(ya, yc, w_out, x, target, gate, ln_g, ln_b)


COL_DQ, COL_DKV, COL_DGA, COL_DGLU_A, COL_DGLU_B, COL_DGC = 0, 1024, 1536, 2560, 3584, 4608


def _tile_copy(dst, stage, sems, slot, row0, col0):
    rows, width = stage.shape[1:]
    return pltpu.make_async_copy(stage.at[slot], dst.at[pl.ds(row0, rows), pl.ds(col0, width)], sems.at[slot])


def _stage_slot(dst, stage, sems, step):
    slot = step % 2

    @pl.when(step >= 2)
    def _():
        _tile_copy(dst, stage, sems, slot, 0, 0).wait()

    return slot


def _tile_send(dst, stage, sems, step, nsteps, col0):
    rows = stage.shape[1]
    slot = step % 2
    _tile_copy(dst, stage, sems, slot, pl.multiple_of(step * rows, rows), col0).start()

    @pl.when(step == nsteps - 1)
    def _():
        _tile_copy(dst, stage, sems, slot, 0, col0).wait()
        if nsteps > 1:
            _tile_copy(dst, stage, sems, 1 - slot, 0, col0).wait()


def _dycat_gates(dy, w_out, attn_o, pw, proj, dproj):
    t = dy.shape[0]
    tm = min(256, t)
    nt = t // tm

    def body(dy_ref, w_ref, o_ref, pw_ref, galo, gahi, gclo, gchi, dp_in, dao_ref, dpw_ref, dp_ref,
             st_a, st_c, sem_a, sem_c):
        i = pl.program_id(0)
        sl_a = _stage_slot(dp_ref, st_a, sem_a, i)
        sl_c = _stage_slot(dp_ref, st_c, sem_c, i)
        dyc = lax.dot_general(dy_ref[...], w_ref[...], (((1,), (1,)), ((), ())), preferred_element_type=F32)
        da, dc = dyc[:, 0:DA], dyc[:, DA:2 * DA]
        ga = jnp.concatenate([galo[...], gahi[...]], axis=1).astype(F32)
        sa = _sig(ga)
        dao_ref[...] = (da * (ga * sa)).astype(BF16)
        st_a[sl_a] = (da * o_ref[...].astype(F32) * _dsilu(ga, sa)).astype(BF16)
        _tile_send(dp_ref, st_a, sem_a, i, nt, COL_DGA)
        gc = jnp.concatenate([gclo[...], gchi[...]], axis=1).astype(F32)
        sc = _sig(gc)
        dpw_ref[...] = (dc * (gc * sc)).astype(BF16)
        st_c[sl_c] = (dc * pw_ref[...].astype(F32) * _dsilu(gc, sc)).astype(BF16)
        _tile_send(dp_ref, st_c, sem_c, i, nt, COL_DGC)

    half = pl.BlockSpec((tm, DA), lambda i: (i, 0))
    cur = lambda cidx: pl.BlockSpec((tm, 512), lambda i: (i, cidx))
    return pl.pallas_call(
        body, name="dycat_gates", grid=(nt,),
        in_specs=[pl.BlockSpec((tm, D), lambda i: (i, 0)), pl.BlockSpec((D, D), lambda i: (0, 0)), half, half,
                  cur(3), cur(4), cur(9), cur(10), _ANY],
        out_specs=[half, half, _ANY],
        out_shape=[S((t, DA), BF16), S((t, DA), BF16), S(dproj.shape, dproj.dtype)],
        input_output_aliases={8: 2},
        scratch_shapes=[pltpu.VMEM((2, tm, DA), BF16), pltpu.VMEM((2, tm, DA), BF16),
                        pltpu.SemaphoreType.DMA((2,)), pltpu.SemaphoreType.DMA((2,))],
        compiler_params=_cp(("arbitrary",), 48),
    )(dy, w_out, attn_o, pw, proj, proj, proj, proj, dproj)


def _conv_bwd_ln(dpw, wpw, uc, ln_g, ln_b):
    t = dpw.shape[0]
    tc = min(256, t)

    def body(dpw_ref, w_ref, uc_ref, lg_ref, lb_ref, duc_ref, sw_ref, st_ref):
        @pl.when(pl.program_id(0) == 0)
        def _():
            st_ref[...] = jnp.zeros_like(st_ref)

        dpw_v = dpw_ref[...]
        ds = lax.dot_general(dpw_v, w_ref[...], (((1,), (1,)), ((), ())), preferred_element_type=F32)
        uc = uc_ref[...]
        mu = jnp.mean(uc, axis=-1, keepdims=True)
        xc = uc - mu
        rstd = lax.rsqrt(jnp.mean(xc * xc, axis=-1, keepdims=True) + LN_EPS)
        uh = xc * rstd
        ln = uh * lg_ref[...] + lb_ref[...]
        sg = _sig(ln)
        sw_ref[...] = (ln * sg).astype(BF16)
        dln = ds * _dsilu(ln, sg)
        dxh = dln * lg_ref[...]
        m1 = jnp.mean(dxh, axis=-1, keepdims=True)
        m2 = jnp.mean(dxh * uh, axis=-1, keepdims=True)
        duc = rstd * (dxh - m1 - uh * m2)
        duc_ref[...] = duc
        st_ref[0:1, :] += jnp.sum(dln * uh, axis=0, keepdims=True)
        st_ref[1:2, :] += jnp.sum(dln, axis=0, keepdims=True)
        st_ref[2:3, :] += jnp.sum(duc, axis=0, keepdims=True)
        st_ref[3:4, :] += jnp.sum(dpw_v.astype(F32), axis=0, keepdims=True)

    row = pl.BlockSpec((1, DA), lambda i: (0, 0))
    full = pl.BlockSpec((tc, DA), lambda i: (i, 0))
    return pl.pallas_call(
        body, name="conv_bwd_ln", grid=(t // tc,),
        in_specs=[full, pl.BlockSpec((DA, DA), lambda i: (0, 0)), full, row, row],
        out_specs=[full, full, pl.BlockSpec((8, DA), lambda i: (0, 0))],
        out_shape=[S((t, DA), F32), S((t, DA), BF16), S((8, DA), F32)],
        compiler_params=_cp(("arbitrary",), 48),
    )(dpw, wpw, uc, ln_g, ln_b)


def _conv_bwd_dw(duc, proj, cwb, dproj):
    t = duc.shape[0]
    tc = _conv_rows(t)
    rc = min(128, tc)
    nt = t // tc
    off = HALO - CW + 1

    def body(dcur, dnext, alo, ahi, blo, bhi, alo_h, ahi_h, blo_h, bhi_h, cw_ref, dp_in, gw_ref, dp_ref,
             u_scr, d_scr, g_scr, st_ab, sem_ab):
        i = pl.program_id(0)
        slot = _stage_slot(dp_ref, st_ab, sem_ab, i)

        @pl.when(i == 0)
        def _():
            g_scr[...] = jnp.zeros_like(g_scr)

        _glu_into(u_scr, i, (alo, ahi), (blo, bhi), (alo_h, ahi_h), (blo_h, bhi_h), tc)
        for cc in range(DA // 128):
            d_scr[cc, 0:tc, :] = dcur[:, cc * 128:(cc + 1) * 128]
            d_scr[cc, tc:tc + HALO, :] = jnp.where(i == nt - 1, 0.0, dnext[:, cc * 128:(cc + 1) * 128])

        def rows(r, carry):
            r0 = pl.multiple_of(r * rc, rc)
            for cc in range(DA // 128):
                lanes = pl.ds(cc * 128, 128)
                acc = jnp.zeros((rc // 8, 8, 128), F32)
                for j in range(CW):
                    dv = d_scr[cc, pl.ds(r0 + j, rc), :].reshape(rc // 8, 8, 128)
                    acc = acc + dv * cw_ref[CW - 1 - j, :, lanes]
                du = acc.reshape(rc, 128)
                d0 = d_scr[cc, pl.ds(r0, rc), :].reshape(rc // 8, 8, 128)
                for k in range(CW):
                    u = u_scr[cc, pl.ds(r0 + off + k, rc), :].reshape(rc // 8, 8, 128)
                    g_scr[k, :, lanes] += jnp.sum(d0 * u, axis=0)
                a_ref, b_ref = (alo, blo) if cc < 4 else (ahi, bhi)
                lc = pl.ds((cc % 4) * 128, 128)
                av = a_ref[pl.ds(r0, rc), lc].astype(F32)
                sb = _sig(b_ref[pl.ds(r0, rc), lc].astype(F32))
                st_ab[slot, pl.ds(r0, rc), lanes] = (du * sb).astype(BF16)
                st_ab[slot, pl.ds(r0, rc), pl.ds(DA + cc * 128, 128)] = (du * av * sb * (1.0 - sb)).astype(BF16)
            return carry

        lax.fori_loop(0, tc // rc, rows, 0)
        _tile_send(dp_ref, st_ab, sem_ab, i, nt, COL_DGLU_A)

        @pl.when(i == nt - 1)
        def _():
            gw_ref[0:CW, :] = jnp.sum(g_scr[...], axis=1)
            gw_ref[CW:32, :] = jnp.zeros((32 - CW, DA), F32)

    nh = tc // HALO
    nhb = t // HALO
    cur = lambda cidx: pl.BlockSpec((tc, 512), lambda i: (i, cidx))
    halo = lambda cidx: pl.BlockSpec((HALO, 512), lambda i: (jnp.maximum(i * nh - 1, 0), cidx))
    full = pl.BlockSpec((tc, DA), lambda i: (i, 0))
    return pl.pallas_call(
        body, name="conv_bwd_dw", grid=(nt,),
        in_specs=[full, pl.BlockSpec((HALO, DA), lambda i: (jnp.minimum((i + 1) * nh, nhb - 1), 0)),
                  cur(5), cur(6), cur(7), cur(8), halo(5), halo(6), halo(7), halo(8),
                  pl.BlockSpec((CW, 8, DA), lambda i: (0, 0, 0)), _ANY],
        out_specs=[pl.BlockSpec((32, DA), lambda i: (0, 0)), _ANY],
        out_shape=[S((32, DA), F32), S(dproj.shape, dproj.dtype)],
        input_output_aliases={11: 1},
        scratch_shapes=[pltpu.VMEM((DA // 128, HALO + tc, 128), F32), pltpu.VMEM((DA // 128, tc + HALO, 128), F32),
                        pltpu.VMEM((CW, 8, DA), F32), pltpu.VMEM((2, tc, 2 * DA), BF16), pltpu.SemaphoreType.DMA((2,))],
        compiler_params=_cp(("arbitrary",), 48),
    )(duc, duc, *([proj] * 8), cwb, dproj)


def _attn_bwd(proj, dao, lse, rel_bias, sinks, bucket, dproj):
    t = proj.shape[0]

    def body(rel_ref, sink_ref, bk_ref, q_ref, kvc_ref, kvp_ref, do_ref, lse_ref, dp_in,
             dcur_ref, dprev_ref, dsacc_ref, dsk_ref, dp_ref, bias_ref, st_q, sem_q):
        n = pl.program_id(0)
        slot = _stage_slot(dp_ref, st_q, sem_q, n)

        @pl.when(n == 0)
        def _():
            _build_bias(rel_ref, bk_ref, bias_ref)
            dsacc_ref[...] = jnp.zeros_like(dsacc_ref)
            dsk_ref[...] = jnp.zeros_like(dsk_ref)

        tbl = jnp.where(n == 0, 0, 1)
        kv = jnp.concatenate([kvp_ref[...], kvc_ref[...]], axis=0)
        for h in range(NKV):
            k_h = kv[:, h * HD:(h + 1) * HD]
            v_h = kv[:, NKV * HD + h * HD:NKV * HD + (h + 1) * HD]
            q4 = _group_rows(q_ref, h) * SCALE
            do4 = _group_rows(do_ref, h)
            lse_h = lse_ref[h:h + 1, :]
            s = lax.dot_general(k_h, q4, (((1,), (1,)), ((), ())), preferred_element_type=F32)
            p = jnp.exp(s + bias_ref[tbl, h] - lse_h)
            dp = lax.dot_general(v_h, do4, (((1,), (1,)), ((), ())), preferred_element_type=F32)
            delta = jnp.sum(p * dp, axis=0, keepdims=True)
            ds = p * (dp - delta)
            dsacc_ref[h] += ds
            dsk_ref[h:h + 1, :] += -jnp.exp(_sink_row(sink_ref, h) - lse_h) * delta
            dsb = ds.astype(BF16)
            dq4 = lax.dot_general(dsb, k_h, (((0,), (0,)), ((), ())), preferred_element_type=F32) * SCALE
            for g in range(GRP):
                hd = GRP * h + g
                st_q[slot, :, hd * HD:(hd + 1) * HD] = dq4[g * BLK:(g + 1) * BLK].astype(BF16)
            dk = jnp.dot(dsb, q4, preferred_element_type=F32)
            dv = jnp.dot(p.astype(BF16), do4, preferred_element_type=F32)
            kc = slice(h * HD, (h + 1) * HD)
            vc = slice(NKV * HD + h * HD, NKV * HD + (h + 1) * HD)
            dprev_ref[:, kc] = dk[0:BLK]
            dcur_ref[:, kc] = dk[BLK:2 * BLK]
            dprev_ref[:, vc] = dv[0:BLK]
            dcur_ref[:, vc] = dv[BLK:2 * BLK]

        _tile_send(dp_ref, st_q, sem_q, n, t // BLK, COL_DQ)

        @pl.when(n == t // BLK - 1)
        def _():
            for g in range(GRP):
                lanes = slice(g * BLK, (g + 1) * BLK)
                dsk_ref[:, lanes] = jnp.broadcast_to(jnp.sum(dsk_ref[:, lanes], axis=-1, keepdims=True), (NKV, BLK))

    blk = lambda w, cidx: pl.BlockSpec((BLK, w), lambda n: (n, cidx))
    return pl.pallas_call(
        body, name="attn_bwd", grid=(t // BLK,),
        in_specs=[_SMEM, _SMEM, pl.BlockSpec((2 * BLK, BLK), lambda n: (0, 0)),
                  blk(DA, 0), blk(512, 2), pl.BlockSpec((BLK, 512), lambda n: (jnp.maximum(n - 1, 0), 2)),
                  blk(DA, 0), pl.BlockSpec((None, NKV, GRP * BLK), lambda n: (n, 0, 0)), _ANY],
        out_specs=[blk(512, 0), blk(512, 0),
                   pl.BlockSpec((NKV, 2 * BLK, GRP * BLK), lambda n: (0, 0, 0)), pl.BlockSpec((NKV, GRP * BLK), lambda n: (0, 0)),
                   _ANY],
        out_shape=[S((t, 512), F32), S((t, 512), F32), S((NKV, 2 * BLK, GRP * BLK), F32),
                   S((NKV, GRP * BLK), F32), S(dproj.shape, dproj.dtype)],
        input_output_aliases={8: 4},
        scratch_shapes=[pltpu.VMEM((2, NKV, 2 * BLK, GRP * BLK), F32), pltpu.VMEM((2, BLK, DA), BF16),
                        pltpu.SemaphoreType.DMA((2,))],
        compiler_params=_cp(("arbitrary",), 40),
    )(rel_bias, sinks, bucket, proj, proj, proj, dao, lse, dproj)


def _dkv_fix(dcur, dprev, dproj):
    t = dcur.shape[0]
    tb = min(1024, t)
    nt = t // tb
    per = tb // BLK

    def body(c_ref, p_ref, pn_ref, dp_in, dp_ref, st_kv, sem_kv):
        i = pl.program_id(0)
        slot = _stage_slot(dp_ref, st_kv, sem_kv, i)
        if tb > BLK:
            st_kv[slot, 0:tb - BLK, :] = (c_ref[0:tb - BLK, :] + p_ref[BLK:tb, :]).astype(BF16)
        nxt = jnp.where(i == nt - 1, 0.0, pn_ref[...])
        st_kv[slot, tb - BLK:tb, :] = (c_ref[tb - BLK:tb, :] + nxt).astype(BF16)
        _tile_send(dp_ref, st_kv, sem_kv, i, nt, COL_DKV)

    tile = pl.BlockSpec((tb, 512), lambda i: (i, 0))
    return pl.pallas_call(
        body, name="dkv_fix", grid=(nt,),
        in_specs=[tile, tile, pl.BlockSpec((BLK, 512), lambda i: (jnp.minimum((i + 1) * per, t // BLK - 1), 0)), _ANY],
        out_specs=_ANY,
        out_shape=S(dproj.shape, dproj.dtype),
        input_output_aliases={3: 0},
        scratch_shapes=[pltpu.VMEM((2, tb, 512), BF16), pltpu.SemaphoreType.DMA((2,))],
        compiler_params=_cp(("arbitrary",), 32),
    )(dcur, dprev, dprev, dproj)


def _bias_grad(dsacc, bucket):
    def body(ds_ref, bk_ref, o_ref, row_scr):
        bk = bk_ref[...]

        def group(h, carry):
            for g in range(GRP):
                dsv = ds_ref[h, :, g * BLK:(g + 1) * BLK]
                for b in range(N_BUCKETS):
                    row_scr[GRP * h + g, b:b + 1, :] = jnp.sum(jnp.where(bk == b, dsv, 0.0), axis=0, keepdims=True)
            return carry

        lax.fori_loop(0, NKV, group, 0)
        for hd in range(NQ):
            o_ref[hd] = jnp.sum(row_scr[hd], axis=-1, keepdims=True)

    return pl.pallas_call(
        body, name="bias_grad", out_shape=S((NQ, N_BUCKETS, 1), F32),
        in_specs=[_VMEM, _VMEM], out_specs=_VMEM,
        scratch_shapes=[pltpu.VMEM((NQ, N_BUCKETS, BLK), F32)],
        compiler_params=_cp(None, 32),
    )(dsacc, bucket)


def _dh_gradx(dproj, wg, gx0, x, scale1p, p_in, p_out, p_pw):
    t = x.shape[0]
    tm = min(512, t)
    ni = t // tm
    bn = D // 2
    nn = D // bn

    def body(dp_ref, w_ref, gx_ref, x_ref, sc_ref, pi, po, pp, out_ref, st_ref, li, lo, lp, send_sems, recv_sems):
        n, i = pl.program_id(0), pl.program_id(1)

        @pl.when((n == 0) & (i == 0))
        def _():
            for cp in _chip_exchange_copies(pi, po, pp, li, lo, lp, send_sems, recv_sems):
                cp.start()

        @pl.when(i == 0)
        def _():
            st_ref[...] = jnp.zeros_like(st_ref)

        dh = None
        for j in range(NCHIP):
            part = lax.dot_general(dp_ref[:, j * NB:(j + 1) * NB], w_ref[j], (((1,), (1,)), ((), ())),
                                   preferred_element_type=F32)
            dh = part if dh is None else dh + part
        out_ref[...] = gx_ref[...] + dh * sc_ref[...]
        st_ref[0:1, :] += jnp.sum(dh, axis=0, keepdims=True)
        st_ref[1:2, :] += jnp.sum(dh * x_ref[...], axis=0, keepdims=True)

        @pl.when((n == nn - 1) & (i == ni - 1))
        def _():
            cps = _chip_exchange_copies(pi, po, pp, li, lo, lp, send_sems, recv_sems)
            for cp in cps:
                cp.wait_recv()
            for cp in cps:
                cp.wait_send()

    tile = pl.BlockSpec((tm, bn), lambda n, i: (i, n))
    return pl.pallas_call(
        body, name="dh_gradx", grid=(nn, ni),
        in_specs=[pl.BlockSpec((tm, D_IN), lambda n, i: (i, 0)), pl.BlockSpec((NCHIP, bn, NB), lambda n, i: (0, n, 0)),
                  tile, tile, pl.BlockSpec((1, bn), lambda n, i: (0, n)), _ANY, _ANY, _ANY],
        out_specs=[tile, pl.BlockSpec((8, bn), lambda n, i: (0, n)), _ANY, _ANY, _ANY],
        out_shape=[S((t, D), F32), S((8, D), F32), S((3, p_in.shape[0], NB), p_in.dtype),
                   S((3,) + p_out.shape[1:], p_out.dtype), S((3,) + p_pw.shape[1:], p_pw.dtype)],
        scratch_shapes=[pltpu.SemaphoreType.DMA((9,)), pltpu.SemaphoreType.DMA((9,))],
        compiler_params=_cp(("arbitrary", "arbitrary"), 60),
    )(dproj, wg, gx0, x, scale1p, p_in, p_out, p_pw)


def _atb(a_parts, b, bn, name, halves=None):
    bm = DA
    t = b.shape[0]
    n = b.shape[1]
    tk = min(2048, t)
    nk = t // tk
    na = len(a_parts)
    nj = n // bn

    def body(*refs):
        a_refs, b_ref, o_ref = refs[:na], refs[na], refs[na + 1 if halves is None else na + 2]
        mi, k = pl.program_id(0), pl.program_id(2)
        if halves is not None:
            g_ref, l_ref, send_sem, recv_sem = refs[na + 1], refs[na + 3], refs[na + 4], refs[na + 5]
            mx, my, c = _me()

            def exchange(hc):
                return pltpu.make_async_remote_copy(src_ref=g_ref.at[hc], dst_ref=l_ref, send_sem=send_sem, recv_sem=recv_sem,
                                                    device_id=(mx, my, 1 - c), device_id_type=MESH)

            first = (mi == 0) & (pl.program_id(1) == 0) & (k == 0)
            last = (mi == na - 1) & (pl.program_id(1) == nj - 1) & (k == nk - 1)

            @pl.when(first)
            def _():
                exchange(1 - c).start()

        for q in range(na):
            @pl.when(mi == q)
            def _(q=q):
                part = lax.dot_general(a_refs[q][...], b_ref[...], (((0,), (0,)), ((), ())), preferred_element_type=F32)

                @pl.when(k == 0)
                def _():
                    o_ref[...] = part

                @pl.when(k > 0)
                def _():
                    o_ref[...] += part

        if halves is not None:
            @pl.when(last)
            def _():
                exchange(c).wait_recv()
                exchange(1 - c).wait_send()

    a_spec = lambda q, cidx: pl.BlockSpec((tk, bm), lambda mi, j, k: (jnp.where(mi == q, k, 0), cidx))
    in_specs = [a_spec(q, cidx) for q, (_, cidx) in enumerate(a_parts)] + [pl.BlockSpec((tk, bn), lambda mi, j, k: (k, j))]
    out_spec = pl.BlockSpec((bm, bn), lambda mi, j, k: (mi, j))
    out_shape = S((na * bm, n), F32)
    args = [arr for arr, _ in a_parts] + [b]
    if halves is None:
        return pl.pallas_call(
            body, name=name, grid=(na, nj, nk), in_specs=in_specs, out_specs=out_spec, out_shape=out_shape,
            compiler_params=_cp(("parallel", "parallel", "arbitrary"), 56),
        )(*args)
    return pl.pallas_call(
        body, name=name, grid=(na, nj, nk), in_specs=in_specs + [_ANY], out_specs=[out_spec, _ANY],
        out_shape=[out_shape, S(halves.shape[1:], halves.dtype)],
        scratch_shapes=[pltpu.SemaphoreType.DMA, pltpu.SemaphoreType.DMA],
        compiler_params=_cp(("arbitrary", "arbitrary", "arbitrary"), 56),
    )(*args, halves)


def _sum8(parts):
    _, r, n = parts.shape

    def body(p_ref, o_ref):
        v = p_ref[0]
        for d in range(1, 8):
            v = v + p_ref[d]
        o_ref[...] = v

    return pl.pallas_call(body, name="sum8", out_shape=S((r, n), F32), in_specs=[_VMEM], out_specs=_VMEM,
                          compiler_params=_cp(None, 32))(parts)


def _adam_math(w, g, m, v):
    m = B1 * m + (1.0 - B1) * g
    v = B2 * v + (1.0 - B2) * (g * g)
    m_hat = m / (1.0 - B1 ** STEP)
    v_hat = v / (1.0 - B2 ** STEP)
    delta = -LR * (m_hat / (jnp.sqrt(v_hat) + EPS) + WD * w)
    return delta, m, v


def _adamw(w, g, m, v, name):
    r, n = w.shape
    tr = min(256, r)

    def body(w_ref, g_ref, m_ref, v_ref, d_ref, nm_ref, nv_ref):
        d_ref[...], nm_ref[...], nv_ref[...] = _adam_math(w_ref[...], g_ref[...], m_ref[...], v_ref[...])

    spec = pl.BlockSpec((tr, n), lambda i: (i, 0))
    return pl.pallas_call(
        body, name=name, grid=(r // tr,), in_specs=[spec] * 4, out_specs=[spec] * 3,
        out_shape=[S((r, n), F32)] * 3, compiler_params=_cp(("parallel",), 48),
    )(w, g, m, v)


def _adamw_ada(ct, dmod_p, w, m, v):
    r, n = w.shape
    tr = min(256, r)

    def body(ct_ref, dm_ref, w_ref, m_ref, v_ref, g_ref, d_ref, nm_ref, nv_ref):
        cv = ct_ref[...]
        g = jnp.dot(cv * _sig(cv), dm_ref[...], preferred_element_type=F32)
        g_ref[...] = g
        d_ref[...], nm_ref[...], nv_ref[...] = _adam_math(w_ref[...], g, m_ref[...], v_ref[...])

    spec = pl.BlockSpec((tr, n), lambda i: (i, 0))
    return pl.pallas_call(
        body, name="adamw_ada", grid=(r // tr,),
        in_specs=[pl.BlockSpec((tr, 8), lambda i: (i, 0)), pl.BlockSpec((8, n), lambda i: (0, 0)), spec, spec, spec],
        out_specs=[spec] * 4, out_shape=[S((r, n), F32)] * 4, compiler_params=_cp(("parallel",), 48),
    )(ct, dmod_p, w, m, v)


def _adamw_small(ws, gs, ms, vs):
    k = len(ws)

    def body(*refs):
        ins, outs = refs[:4 * k], refs[4 * k:]
        for i in range(k):
            d, nm, nv = _adam_math(ins[i][...], ins[k + i][...], ins[2 * k + i][...], ins[3 * k + i][...])
            outs[i][...] = d
            outs[k + i][...] = nm
            outs[2 * k + i][...] = nv

    shapes = [S(w.shape, F32) for w in ws]
    return pl.pallas_call(body, name="adamw_small", out_shape=shapes * 3, in_specs=[_VMEM] * (4 * k),
                          out_specs=[_VMEM] * (3 * k), compiler_params=_cp(None, 32))(*ws, *gs, *ms, *vs)


def _bucket_map():
    qi = jnp.arange(BLK, dtype=jnp.int32)[None, :]
    kj = jnp.arange(2 * BLK, dtype=jnp.int32)[:, None]
    dist = qi + BLK - kj
    dd = jnp.maximum(dist, 0)
    max_exact = N_BUCKETS // 2
    dfl = jnp.maximum(dd, 1).astype(F32)
    large = max_exact + (jnp.log(dfl / max_exact) / math.log(MAX_DIST / max_exact) * (N_BUCKETS - max_exact)).astype(jnp.int32)
    large = jnp.minimum(large, N_BUCKETS - 1)
    bucket = jnp.where(dd < max_exact, dd, large)
    return jnp.where((dist >= 0) & (dist < BLK), bucket, -1).astype(jnp.int32)


def _pad_rows(a, rows):
    return jnp.pad(a, ((0, rows - a.shape[0]), (0, 0)))


def kernel(x, c, w_ada, b_ada, w_in, rel_bias, sinks, conv_w, conv_b, conv_ln_g, conv_ln_b, w_pw, b_pw, w_out, ln_g, ln_b, loss_target, m_w_ada, m_b_ada, m_w_in, m_rel_bias, m_sinks, m_conv_w, m_conv_b, m_conv_ln_g, m_conv_ln_b, m_w_pw, m_b_pw, m_w_out, m_ln_g, m_ln_b, v_w_ada, v_b_ada, v_w_in, v_rel_bias, v_sinks, v_conv_w, v_conv_b, v_conv_ln_g, v_conv_ln_b, v_w_pw, v_b_pw, v_w_out, v_ln_g, v_ln_b):
    mx, my, mc = _me()
    chip = 2 * mx + my
    dev = 2 * chip + mc
    t = x.shape[1]
    x2 = x.reshape(t, D)
    tgt = loss_target.reshape(t, D)
    n_ada = w_ada.shape[2]
    cw_cols = conv_w.shape[2]

    pack0 = jnp.concatenate([_pad_rows(c, 8), _pad_rows(_pad_rows(conv_w[0], 32).reshape(-1, D), 8)], axis=0)
    g0 = _allgather8(pack0, "gather_c_convw").reshape(8, 16, D)
    c_all = g0[:, 0, :]
    cw_rows = 32 * cw_cols // D
    cw_full = jnp.concatenate([g0[2 * q, 8:8 + cw_rows, :].reshape(32, cw_cols) for q in range(NCHIP)], axis=1)[:CW]
    cwb = jnp.broadcast_to(cw_full[:, None, :], (CW, 8, DA))

    b_ada_p = lax.dynamic_slice(b_ada, (0, chip * n_ada), (1, n_ada))
    mod_part = _ada_fwd(c_all, w_ada[0], b_ada_p)
    mod_all = _allgather8(mod_part, "gather_mod").reshape(8, 8, n_ada)
    mod = jnp.concatenate([lax.dynamic_slice(mod_all[2 * q], (dev, 0), (1, n_ada)) for q in range(NCHIP)], axis=1)
    shift, scale, gate = mod[:, 0:D], mod[:, D:2 * D], mod[:, 2 * D:3 * D]
    scale1p = 1.0 + scale

    bucket = _bucket_map()
    sel_chip = jnp.reshape(chip, (1,)).astype(jnp.int32)
    sel = jnp.reshape(mc, (1,)).astype(jnp.int32)
    proj, h, wg_in, wg_out, wg_pw = _inproj_gather(
        x2, scale1p, shift, w_in[0].astype(BF16), w_out[0].astype(BF16), w_pw[0].astype(BF16), sel_chip)
    w_out_f = wg_out.reshape(D, D)
    w_pw_f = wg_pw.reshape(DA, DA)
    ya, attn_o, lse = _attn_fwd(proj, rel_bias, sinks, bucket)
    uc, pw, yc = _conv_fwd(proj, cwb, conv_b, conv_ln_g, conv_ln_b, w_pw_f, b_pw)
    gx0, dy, st_out = _outproj_loss(ya, yc, w_out_f, x2, tgt, gate, ln_g, ln_b)

    dproj = lax.empty((t, D_IN), BF16)
    dao, dpw, dproj = _dycat_gates(dy, w_out_f, attn_o, pw, proj, dproj)
    duc, sw, st_conv = _conv_bwd_ln(dpw, w_pw_f, uc, conv_ln_g, conv_ln_b)
    gw_conv, dproj = _conv_bwd_dw(duc, proj, cwb, dproj)
    dkv_cur, dkv_prev, dsacc, dsk, dproj = _attn_bwd(proj, dao, lse, rel_bias, sinks, bucket, dproj)
    dproj = _dkv_fix(dkv_cur, dkv_prev, dproj)
    g_rel = _bias_grad(dsacc, bucket).reshape(NQ, N_BUCKETS).T
    r_out, r_pw = D // NCHIP // 2, DA // NCHIP // 2
    gp_in = _atb([(h, 0), (h, 1)], dproj, NB, "grad_w_in")
    gi3 = gp_in.reshape(2, D // 2, D_IN)
    gp_out, l_in = _atb([(ya, 0), (yc, 0)], dy, 1024, "grad_w_out", halves=gi3)
    gp_pw = _atb([(sw, 0)], dpw, 1024, "grad_w_pw")
    go3, gp3 = gp_out.reshape(8, r_out, D), gp_pw.reshape(8, r_pw, DA)
    l_out, l_pw = _pair_exchange(go3, gp3)
    pi32, pi16 = _pair_sum(gi3, l_in.reshape(1, D // 2, D_IN), sel, "pair_sum_in")
    po32, po16 = _pair_sum(go3, l_out, sel, "pair_sum_out")
    pp32, pp16 = _pair_sum(gp3, l_pw, sel, "pair_sum_pw")
    grad_x, st_in, rc_in, rc_out, rc_pw = _dh_gradx(dproj, wg_in, gx0, x2, scale1p, pi16[0], po16, pp16)
    own_in = lax.dynamic_slice(pi32[0], (0, chip * NB), (D // 2, NB))
    own_out = lax.dynamic_index_in_dim(po32, chip, 0, keepdims=False)
    own_pw = lax.dynamic_index_in_dim(pp32, chip, 0, keepdims=False)
    h_in = _chip_sum(own_in, rc_in, sel, "chip_sum_in")
    h_out = _chip_sum(own_out, rc_out, sel, "chip_sum_out")
    h_pw = _chip_sum(own_pw, rc_pw, sel, "chip_sum_pw")
    f_in, f_out, f_pw = _pair_share(h_in, h_out, h_pw)
    g_w_in = f_in.reshape(D, NB)
    g_w_out = f_out.reshape(D // NCHIP, D)
    g_w_pw = f_pw.reshape(DA // NCHIP, DA)

    dmod = jnp.concatenate([st_in[0:1], st_in[1:2], st_out[2:3]], axis=1)
    loss_row = st_out[3:4, 0:1]
    small = jnp.concatenate([
        dmod, st_out[0:1], st_out[1:2],
        st_conv[0:1], st_conv[1:2], st_conv[2:3], st_conv[3:4],
        g_rel.reshape(1, N_BUCKETS * NQ), dsk.reshape(NKV, GRP, BLK)[:, :, 0].reshape(1, NQ), loss_row,
        gw_conv[:CW].reshape(1, CW * DA)], axis=1)
    n_small = small.shape[1]
    rows_small = -(-n_small // (8 * D)) * 8
    small = jnp.pad(small, ((0, 0), (0, rows_small * D - n_small))).reshape(rows_small, D)
    parts = _allgather8(small, "gather_small").reshape(8, rows_small, D)
    tot = _sum8(parts).reshape(1, rows_small * D)
    dmod_all = parts.reshape(8, rows_small * D)[:, 0:3 * D]

    o = 3 * D
    def take(nn):
        nonlocal o
        v = tot[:, o:o + nn]
        o += nn
        return v
    g_b_ada = tot[:, 0:3 * D]
    g_ln_g, g_ln_b = take(D), take(D)
    g_cln_g, g_cln_b, g_conv_b, g_b_pw = take(DA), take(DA), take(DA), take(DA)
    g_rel_bias = take(N_BUCKETS * NQ).reshape(N_BUCKETS, NQ)
    g_sinks = take(NQ)
    loss = take(1).reshape(())
    g_conv_w_full = take(CW * DA).reshape(CW, DA)
    g_conv_w = lax.dynamic_slice(g_conv_w_full, (0, chip * cw_cols), (CW, cw_cols))

    dmod_p = lax.dynamic_slice(dmod_all, (0, chip * n_ada), (8, n_ada))
    g_w_ada, d_w_ada, nm_w_ada, nv_w_ada = _adamw_ada(c_all.T, dmod_p, w_ada[0], m_w_ada[0], v_w_ada[0])
    d_w_in, nm_w_in, nv_w_in = _adamw(w_in[0], g_w_in, m_w_in[0], v_w_in[0], "adamw_in")
    d_w_out, nm_w_out, nv_w_out = _adamw(w_out[0], g_w_out, m_w_out[0], v_w_out[0], "adamw_out")
    d_w_pw, nm_w_pw, nv_w_pw = _adamw(w_pw[0], g_w_pw, m_w_pw[0], v_w_pw[0], "adamw_pw")
    small_w = [b_ada, rel_bias, sinks, conv_w[0], conv_b, conv_ln_g, conv_ln_b, b_pw, ln_g, ln_b]
    small_g = [g_b_ada, g_rel_bias, g_sinks, g_conv_w, g_conv_b, g_cln_g, g_cln_b, g_b_pw, g_ln_g, g_ln_b]
    small_m = [m_b_ada, m_rel_bias, m_sinks, m_conv_w[0], m_conv_b, m_conv_ln_g, m_conv_ln_b, m_b_pw, m_ln_g, m_ln_b]
    small_v = [v_b_ada, v_rel_bias, v_sinks, v_conv_w[0], v_conv_b, v_conv_ln_g, v_conv_ln_b, v_b_pw, v_ln_g, v_ln_b]
    res = _adamw_small(small_w, small_g, small_m, small_v)
    ns = len(small_w)
    d_s, nm_s, nv_s = res[:ns], res[ns:2 * ns], res[2 * ns:]

    def ordered(w_ada_, w_in_, w_pw_, w_out_, sm):
        b_ada_, rel_, sinks_, conv_w_, conv_b_, cln_g_, cln_b_, b_pw_, ln_g_, ln_b_ = sm
        return (w_ada_[None], b_ada_, w_in_[None], rel_, sinks_, conv_w_[None], conv_b_, cln_g_, cln_b_,
                w_pw_[None], b_pw_, w_out_[None], ln_g_, ln_b_)

    grads = ordered(g_w_ada, g_w_in, g_w_pw, g_w_out, small_g)
    deltas = ordered(d_w_ada, d_w_in, d_w_pw, d_w_out, d_s)
    new_m = ordered(nm_w_ada, nm_w_in, nm_w_pw, nm_w_out, nm_s)
    new_v = ordered(nv_w_ada, nv_w_in, nv_w_pw, nv_w_out, nv_s)
    return (loss, grad_x.reshape(1, t, D), *grads, *deltas, *new_m, *new_v)
```

```python
import functools
import math

import jax
import jax.numpy as jnp
from jax import lax
from jax.experimental import pallas as pl
from jax.experimental.pallas import tpu as pltpu

F32, BF16 = jnp.float32, jnp.bfloat16
S = jax.ShapeDtypeStruct
MESH = pl.DeviceIdType.MESH

D = 2048
DA = 1024
HD = 64
NQ, NKV, GRP = 16, 4, 4
BLK = 128
CW = 31
HALO = 32
D_IN = 5632
NCHIP = 4
NB = D_IN // NCHIP
N_BUCKETS, MAX_DIST = 32, 128
LN_EPS = 1e-5
ALPHA = 2.0 ** 0.25
SCALE = HD ** -0.5
NEG = -1e30
LR, B1, B2, EPS, WD, STEP = 0.001, 0.9, 0.999, 1e-08, 0.01, 10

_VMEM = pl.BlockSpec(memory_space=pltpu.VMEM)
_SMEM = pl.BlockSpec(memory_space=pltpu.SMEM)
_ANY = pl.BlockSpec(memory_space=pl.ANY)


def _cp(sem=None, vmem_mb=None):
    kw = {}
    if sem is not None:
        kw["dimension_semantics"] = sem
    if vmem_mb is not None:
        kw["vmem_limit_bytes"] = vmem_mb * 1024 * 1024
    return pltpu.CompilerParams(**kw)


def _sig(v):
    return jax.nn.sigmoid(v)


def _dsilu(g, sg):
    return sg * (1.0 + g * (1.0 - sg))


def _me():
    return lax.axis_index("x"), lax.axis_index("y"), lax.axis_index("c")


def _allgather8(x_shard, name):
    m_per, n = x_shard.shape

    def body(x_ref, out_ref, send_sems, recv_sems, local_sem):
        x, y, c = _me()
        me, sibling = (x, y, c), (x, y, 1 - c)
        chips = [(1 - x, y), (x, 1 - y), (1 - x, 1 - y)]

        def rows(px, py, pc):
            return out_ref.at[pl.ds((4 * px + 2 * py + pc) * m_per, m_per), :]

        def copy(k, block, to, src=None):
            return pltpu.make_async_remote_copy(
                src_ref=rows(*block) if src is None else src, dst_ref=rows(*block),
                send_sem=send_sems.at[k], recv_sem=recv_sems.at[k], device_id=to, device_id_type=MESH)

        mine = pltpu.make_async_copy(x_ref, rows(*me), local_sem)
        mine.start()
        first = [copy(0, me, sibling, src=x_ref)]
        first += [copy(1 + j, me, (*chip, c), src=x_ref) for j, chip in enumerate(chips)]
        for cp in first:
            cp.start()
        passed = [copy(4 + j, (*chip, c), sibling) for j, chip in enumerate(chips)]
        for j, chip in enumerate(chips):
            copy(1 + j, (*chip, c), me).wait_recv()
            passed[j].start()
        copy(0, sibling, me).wait_recv()
        for j, chip in enumerate(chips):
            copy(4 + j, (*chip, 1 - c), me).wait_recv()
        for cp in first + passed:
            cp.wait_send()
        mine.wait()

    return pl.pallas_call(
        body, name=name, out_shape=S((8 * m_per, n), x_shard.dtype),
        in_specs=[_VMEM], out_specs=_VMEM,
        scratch_shapes=[pltpu.SemaphoreType.DMA((7,)), pltpu.SemaphoreType.DMA((7,)), pltpu.SemaphoreType.DMA],
    )(x_shard)


def _inproj_gather(x, scale1p, shift, wi, wo, wp, chip_id):
    t = x.shape[0]
    tm = min(512, t)
    ni = t // tm
    shards = (wi, wo, wp)
    halves = [s.shape[0] // 2 for s in shards]

    def body(cid_ref, x_ref, sc_ref, sh_ref, wi_ref, wo_ref, wp_ref, proj_ref, h_ref, gi_ref, go_ref, gp_ref,
             wbuf, obuf, pbuf, hb_scr, send_sems, recv_sems, ld_sems, st_sems):
        jj, i = pl.program_id(0), pl.program_id(1)
        mx, my, c = _me()
        p = 2 * mx + my
        sibling = (mx, my, 1 - c)
        srcs, dsts, bufs = (wi_ref, wo_ref, wp_ref), (gi_ref, go_ref, gp_ref), (wbuf, obuf, pbuf)
        chips = {1: (mx, 1 - my), 2: (1 - mx, my), 3: (1 - mx, 1 - my)}

        def half(ref, w, hc):
            return ref.at[pl.ds(hc * halves[w], halves[w])]

        def copy(k, src, dst, to):
            return pltpu.make_async_remote_copy(src_ref=src, dst_ref=dst, send_sem=send_sems.at[k],
                                                recv_sem=recv_sems.at[k], device_id=to, device_id_type=MESH)

        def sent(w, m):
            return copy(3 * w + m - 1, half(srcs[w], w, c), half(dsts[w].at[p], w, c), (*chips[m], c))

        def landed(w, m):
            blk = half(dsts[w].at[jnp.bitwise_xor(p, m)], w, c)
            return copy(3 * w + m - 1, blk, blk, (*chips[m], c))

        def passed(w, m, hc):
            blk = half(dsts[w].at[jnp.bitwise_xor(p, m)], w, hc)
            return copy(9 + 3 * w + m - 1, blk, blk, sibling)

        def vm(w, slot):
            return bufs[w].at[slot] if w == 0 else bufs[w]

        def load(w, src, slot=0):
            return pltpu.make_async_copy(src, vm(w, slot), ld_sems.at[w])

        def store(w):
            return pltpu.make_async_copy(vm(w, 0), dsts[w].at[p], st_sems.at[w])

        def block_load(m):
            return load(0, gi_ref.at[jnp.bitwise_xor(p, m)], m % 2)

        @pl.when((jj == 0) & (i == 0))
        def _():
            sent(0, 1).start()
            sent(0, 2).start()
            for w in range(3):
                load(w, srcs[w]).start()
            for w in range(3):
                load(w, srcs[w]).wait()
                store(w).start()

        for m in (1, 2, 3):
            @pl.when((jj == m) & (i == 0))
            def _(m=m):
                if m == 1:
                    sent(0, 3).start()
                    store(0).wait()
                if m == 2:
                    for w in (1, 2):
                        for mm in (1, 2, 3):
                            sent(w, mm).start()
                block_load(m).wait()

        hb_scr[...] = (x_ref[...] * sc_ref[...] + sh_ref[...]).astype(BF16)

        @pl.when(jj == 0)
        def _():
            h_ref[...] = hb_scr[...]

        proj_ref[...] = jnp.dot(hb_scr[...], wbuf[jj % 2], preferred_element_type=F32).astype(BF16)

        mid = ni // 2

        @pl.when((jj == 0) & (i == ni - 1))
        def _():
            for m in (1, 2):
                landed(0, m).wait_recv()
                passed(0, m, c).start()

        @pl.when((jj == 2) & (i == mid))
        def _():
            landed(0, 3).wait_recv()
            passed(0, 3, c).start()

        @pl.when((jj == NCHIP - 1) & (i == mid))
        def _():
            for w in (1, 2):
                for m in (1, 2, 3):
                    landed(w, m).wait_recv()
                    passed(w, m, c).start()

        for m in (1, 2, 3):
            @pl.when((jj == m - 1) & (i == ni - 1))
            def _(m=m):
                passed(0, m, 1 - c).wait_recv()
                block_load(m).start()

        @pl.when((jj == NCHIP - 1) & (i == ni - 1))
        def _():
            for w in (1, 2):
                for m in (1, 2, 3):
                    passed(w, m, 1 - c).wait_recv()
            for w in range(3):
                for m in (1, 2, 3):
                    sent(w, m).wait_send()
                    passed(w, m, c).wait_send()
            store(1).wait()
            store(2).wait()

    row = pl.BlockSpec((1, D), lambda jj, i, s: (0, 0))
    return pl.pallas_call(
        body, name="inproj_gather",
        grid_spec=pltpu.PrefetchScalarGridSpec(
            num_scalar_prefetch=1, grid=(NCHIP, ni),
            in_specs=[pl.BlockSpec((tm, D), lambda jj, i, s: (i, 0)), row, row, _ANY, _ANY, _ANY],
            out_specs=[pl.BlockSpec((tm, NB), lambda jj, i, s: (i, jnp.bitwise_xor(s[0], jj))),
                       pl.BlockSpec((tm, D), lambda jj, i, s: (jnp.where(jj == 0, i, ni - 1), 0)),
                       _ANY, _ANY, _ANY],
            scratch_shapes=[pltpu.VMEM((2,) + wi.shape, BF16), pltpu.VMEM(wo.shape, BF16), pltpu.VMEM(wp.shape, BF16),
                            pltpu.VMEM((tm, D), BF16),
                            pltpu.SemaphoreType.DMA((18,)), pltpu.SemaphoreType.DMA((18,)),
                            pltpu.SemaphoreType.DMA((3,)), pltpu.SemaphoreType.DMA((3,))]),
        out_shape=[S((t, D_IN), BF16), S((t, D), BF16)] + [S((NCHIP,) + s.shape, s.dtype) for s in shards],
        compiler_params=_cp(("arbitrary", "arbitrary"), 48),
    )(chip_id, x, scale1p, shift, wi, wo, wp)


def _pair_exchange(g_out, g_pw):
    def body(go, gp, lo, lp, send_sems, recv_sems):
        x, y, c = _me()
        sibling = (x, y, 1 - c)
        copies = []
        for j in range(NCHIP):
            copies.append((go.at[2 * j + 1 - c], lo.at[j]))
        for j in range(NCHIP):
            copies.append((gp.at[2 * j + 1 - c], lp.at[j]))
        cps = [pltpu.make_async_remote_copy(src_ref=s_, dst_ref=d_, send_sem=send_sems.at[k], recv_sem=recv_sems.at[k],
                                            device_id=sibling, device_id_type=MESH) for k, (s_, d_) in enumerate(copies)]
        for cp in cps:
            cp.start()
        for cp in cps:
            cp.wait_recv()
        for cp in cps:
            cp.wait_send()

    n = 2 * NCHIP
    return pl.pallas_call(
        body, name="grad_pair_exchange",
        out_shape=[S((NCHIP,) + g_out.shape[1:], g_out.dtype), S((NCHIP,) + g_pw.shape[1:], g_pw.dtype)],
        in_specs=[_ANY] * 2, out_specs=[_ANY] * 2,
        scratch_shapes=[pltpu.SemaphoreType.DMA((n,)), pltpu.SemaphoreType.DMA((n,))],
    )(g_out, g_pw)


def _chip_exchange_copies(pi, po, pp, li, lo, lp, send_sems, recv_sems):
    x, y, c = _me()
    chips = [(1 - x, y), (x, 1 - y), (1 - x, 1 - y)]
    cps = []
    for j, chip in enumerate(chips):
        q = 2 * chip[0] + chip[1]
        pairs = ((pi.at[:, pl.ds(q * NB, NB)], li.at[j]), (po.at[q], lo.at[j]), (pp.at[q], lp.at[j]))
        for w, (src, dst) in enumerate(pairs):
            k = 3 * j + w
            cps.append(pltpu.make_async_remote_copy(src_ref=src, dst_ref=dst, send_sem=send_sems.at[k],
                                                    recv_sem=recv_sems.at[k], device_id=(*chip, c), device_id_type=MESH))
    return cps


def _pair_share(f_in, f_out, f_pw):
    bufs = (f_in, f_out, f_pw)

    def body(ai, ao, ap, fi, fo, fp, send_sems, recv_sems):
        x, y, c = _me()
        sibling = (x, y, 1 - c)

        def copy(w, ref, hc):
            return pltpu.make_async_remote_copy(src_ref=ref.at[hc], dst_ref=ref.at[hc], send_sem=send_sems.at[w],
                                                recv_sem=recv_sems.at[w], device_id=sibling, device_id_type=MESH)

        cps = [copy(w, ref, c) for w, ref in enumerate((fi, fo, fp))]
        for cp in cps:
            cp.start()
        for w, ref in enumerate((fi, fo, fp)):
            copy(w, ref, 1 - c).wait_recv()
        for cp in cps:
            cp.wait_send()

    return pl.pallas_call(
        body, name="grad_pair_share",
        out_shape=[S(b.shape, b.dtype) for b in bufs],
        in_specs=[_ANY] * 3, out_specs=[_ANY] * 3, input_output_aliases={0: 0, 1: 1, 2: 2},
        scratch_shapes=[pltpu.SemaphoreType.DMA((3,)), pltpu.SemaphoreType.DMA((3,))],
    )(*bufs)


def _pair_sum(g, l, sel, name):
    n, r, ccols = l.shape
    tr = min(256 if ccols <= D else 128, r)

    def body(sel_ref, g_ref, l_ref, o32_ref, o16_ref):
        v = g_ref[...] + l_ref[...]
        o32_ref[...] = v
        o16_ref[...] = v.astype(BF16)

    spec_l = pl.BlockSpec((None, tr, ccols), lambda j, i, s: (j, i, 0))
    return pl.pallas_call(
        body, name=name,
        grid_spec=pltpu.PrefetchScalarGridSpec(
            num_scalar_prefetch=1, grid=(n, r // tr),
            in_specs=[pl.BlockSpec((None, tr, ccols), lambda j, i, s: (2 * j + s[0], i, 0)), spec_l],
            out_specs=[spec_l, spec_l]),
        out_shape=[S(l.shape, F32), S(l.shape, BF16)],
        compiler_params=_cp(("parallel", "parallel"), 48),
    )(sel, g, l)


def _chip_sum(own, recv, sel, name):
    r, ccols = own.shape
    tr = min(256, r)

    def body(sel_ref, o_ref, r_ref, out_ref):
        v = o_ref[...]
        for j in range(3):
            v = v + r_ref[j].astype(F32)
        out_ref[...] = v

    return pl.pallas_call(
        body, name=name,
        grid_spec=pltpu.PrefetchScalarGridSpec(
            num_scalar_prefetch=1, grid=(r // tr,),
            in_specs=[pl.BlockSpec((tr, ccols), lambda i, s: (i, 0)), pl.BlockSpec((3, tr, ccols), lambda i, s: (0, i, 0))],
            out_specs=pl.BlockSpec((None, tr, ccols), lambda i, s: (s[0], i, 0))),
        out_shape=S((2, r, ccols), F32),
        compiler_params=_cp(("parallel",), 48),
    )(sel, own, recv)


def _ada_fwd(c_all, w_ada, b_ada_p):
    n = w_ada.shape[1]
    tn = 512

    def body(c_ref, w_ref, b_ref, o_ref):
        cv = c_ref[...]
        ca = cv * _sig(cv)
        o_ref[...] = jnp.dot(ca, w_ref[...], preferred_element_type=F32) + b_ref[...]

    return pl.pallas_call(
        body, name="ada_fwd", grid=(n // tn,),
        in_specs=[pl.BlockSpec((8, D), lambda j: (0, 0)), pl.BlockSpec((D, tn), lambda j: (0, j)),
                  pl.BlockSpec((1, tn), lambda j: (0, j))],
        out_specs=pl.BlockSpec((8, tn), lambda j: (0, j)),
        out_shape=S((8, n), F32),
        compiler_params=_cp(("parallel",), 32),
    )(c_all, w_ada, b_ada_p)


def _build_bias(rel_ref, bk_ref, bias_ref):
    bk = bk_ref[...]
    kj = lax.broadcasted_iota(jnp.int32, (2 * BLK, BLK), 0)
    for hd in range(NQ):
        acc = jnp.full((2 * BLK, BLK), NEG, F32)
        for b in range(N_BUCKETS):
            acc = jnp.where(bk == b, rel_ref[b, hd], acc)
        lanes = slice((hd % GRP) * BLK, (hd % GRP + 1) * BLK)
        bias_ref[1, hd // GRP, :, lanes] = acc
        bias_ref[0, hd // GRP, :, lanes] = jnp.where(kj < BLK, NEG, acc)


def _group_rows(ref, h):
    return jnp.concatenate([ref[:, (GRP * h + g) * HD:(GRP * h + g + 1) * HD] for g in range(GRP)], axis=0)


def _sink_row(sink_ref, h):
    return jnp.concatenate([jnp.full((1, BLK), sink_ref[0, GRP * h + g], F32) for g in range(GRP)], axis=1)


def _attn_fwd(proj, rel_bias, sinks, bucket):
    t = proj.shape[0]

    def body(rel_ref, sink_ref, bk_ref, q_ref, kvc_ref, kvp_ref, glo_ref, ghi_ref, ya_ref, o_ref, lse_ref, bias_ref):
        n = pl.program_id(0)

        @pl.when(n == 0)
        def _():
            _build_bias(rel_ref, bk_ref, bias_ref)

        tbl = jnp.where(n == 0, 0, 1)
        kv = jnp.concatenate([kvp_ref[...], kvc_ref[...]], axis=0)
        for h in range(NKV):
            k_h = kv[:, h * HD:(h + 1) * HD]
            v_h = kv[:, NKV * HD + h * HD:NKV * HD + (h + 1) * HD]
            q4 = _group_rows(q_ref, h) * SCALE
            s = lax.dot_general(k_h, q4, (((1,), (1,)), ((), ())), preferred_element_type=F32) + bias_ref[tbl, h]
            sink = _sink_row(sink_ref, h)
            m = jnp.maximum(jnp.max(s, axis=0, keepdims=True), sink)
            p = jnp.exp(s - m)
            l = jnp.sum(p, axis=0, keepdims=True) + jnp.exp(sink - m)
            pn = (p * (1.0 / l)).astype(BF16)
            o4 = lax.dot_general(pn, v_h, (((0,), (0,)), ((), ())), preferred_element_type=F32)
            lse_ref[h:h + 1, :] = m + jnp.log(l)
            for g in range(GRP):
                hd = GRP * h + g
                cols = slice(hd * HD, (hd + 1) * HD)
                o = o4[g * BLK:(g + 1) * BLK]
                g_ref = glo_ref if hd < NQ // 2 else ghi_ref
                gc = slice((hd % (NQ // 2)) * HD, (hd % (NQ // 2) + 1) * HD)
                gt = g_ref[:, gc].astype(F32)
                o_ref[:, cols] = o.astype(BF16)
                ya_ref[:, cols] = (o * (gt * _sig(gt))).astype(BF16)

    blk = lambda w, cidx: pl.BlockSpec((BLK, w), lambda n: (n, cidx))
    return pl.pallas_call(
        body, name="attn_fwd", grid=(t // BLK,),
        in_specs=[_SMEM, _SMEM, pl.BlockSpec((2 * BLK, BLK), lambda n: (0, 0)),
                  blk(DA, 0), blk(512, 2), pl.BlockSpec((BLK, 512), lambda n: (jnp.maximum(n - 1, 0), 2)),
                  blk(512, 3), blk(512, 4)],
        out_specs=[blk(DA, 0), blk(DA, 0), pl.BlockSpec((None, NKV, GRP * BLK), lambda n: (n, 0, 0)),
                   pl.BlockSpec((2, NKV, 2 * BLK, GRP * BLK), lambda n: (0, 0, 0, 0))],
        out_shape=[S((t, DA), BF16), S((t, DA), BF16), S((t // BLK, NKV, GRP * BLK), F32),
                   S((2, NKV, 2 * BLK, GRP * BLK), F32)],
        compiler_params=_cp(("arbitrary",), 40),
    )(rel_bias, sinks, bucket, proj, proj, proj, proj, proj)


def _conv_rows(t):
    return min(256, t)


def _glu_into(u_scr, i, a_refs, b_refs, ah_refs, bh_refs, tc):
    for cc in range(DA // 128):
        half, lc = cc // 4, slice((cc % 4) * 128, (cc % 4 + 1) * 128)
        uh = ah_refs[half][:, lc].astype(F32) * _sig(bh_refs[half][:, lc].astype(F32))
        u_scr[cc, 0:HALO, :] = jnp.where(i == 0, 0.0, uh)
        u_scr[cc, HALO:HALO + tc, :] = a_refs[half][:, lc].astype(F32) * _sig(b_refs[half][:, lc].astype(F32))


def _conv_fwd(proj, cwb, conv_b, ln_g, ln_b, wpw, b_pw):
    t = proj.shape[0]
    tc = _conv_rows(t)
    rc = min(128, tc)

    def body(alo, ahi, blo, bhi, alo_h, ahi_h, blo_h, bhi_h, glo, ghi, cw_ref, cb_ref, lg_ref, lb_ref, wpw_ref, bpw_ref,
             uc_ref, pw_ref, yc_ref, u_scr):
        i = pl.program_id(0)
        _glu_into(u_scr, i, (alo, ahi), (blo, bhi), (alo_h, ahi_h), (blo_h, bhi_h), tc)

        def rows(r, carry):
            r0 = pl.multiple_of(r * rc, rc)
            for cc in range(DA // 128):
                lanes = pl.ds(cc * 128, 128)
                acc = jnp.zeros((rc // 8, 8, 128), F32)
                for k in range(CW):
                    u = u_scr[cc, pl.ds(r0 + (HALO - CW + 1) + k, rc), :].reshape(rc // 8, 8, 128)
                    acc = acc + u * cw_ref[k, :, lanes]
                uc_ref[pl.ds(r0, rc), lanes] = acc.reshape(rc, 128) + cb_ref[:, lanes]
            return carry

        lax.fori_loop(0, tc // rc, rows, 0)

        uc = uc_ref[...]
        mu = jnp.mean(uc, axis=-1, keepdims=True)
        xc = uc - mu
        rstd = lax.rsqrt(jnp.mean(xc * xc, axis=-1, keepdims=True) + LN_EPS)
        ln = xc * rstd * lg_ref[...] + lb_ref[...]
        sw = (ln * _sig(ln)).astype(BF16)
        pw = jnp.dot(sw, wpw_ref[...], preferred_element_type=F32) + bpw_ref[...]
        pw_ref[...] = pw.astype(BF16)
        gt = jnp.concatenate([glo[...], ghi[...]], axis=1).astype(F32)
        yc_ref[...] = (pw * (gt * _sig(gt))).astype(BF16)

    nh = tc // HALO
    cur = lambda cidx: pl.BlockSpec((tc, 512), lambda i: (i, cidx))
    halo = lambda cidx: pl.BlockSpec((HALO, 512), lambda i: (jnp.maximum(i * nh - 1, 0), cidx))
    row = pl.BlockSpec((1, DA), lambda i: (0, 0))
    full = pl.BlockSpec((tc, DA), lambda i: (i, 0))
    return pl.pallas_call(
        body, name="conv_fwd", grid=(t // tc,),
        in_specs=[cur(5), cur(6), cur(7), cur(8), halo(5), halo(6), halo(7), halo(8), cur(9), cur(10),
                  pl.BlockSpec((CW, 8, DA), lambda i: (0, 0, 0)), row, row, row,
                  pl.BlockSpec((DA, DA), lambda i: (0, 0)), row],
        out_specs=[full, full, full],
        out_shape=[S((t, DA), F32), S((t, DA), BF16), S((t, DA), BF16)],
        scratch_shapes=[pltpu.VMEM((DA // 128, HALO + tc, 128), F32)],
        compiler_params=_cp(("arbitrary",), 48),
    )(*([proj] * 10), cwb, conv_b, ln_g, ln_b, wpw, b_pw)


def _outproj_loss(ya, yc, w_out, x, target, gate, ln_g, ln_b):
    t = x.shape[0]
    tm = min(256, t)

    def body(ya_ref, yc_ref, w_ref, x_ref, t_ref, g_ref, lg_ref, lb_ref, gx_ref, dy_ref, st_ref):
        @pl.when(pl.program_id(0) == 0)
        def _():
            st_ref[...] = jnp.zeros_like(st_ref)

        y = jnp.dot(ya_ref[...], w_ref[0:DA, :], preferred_element_type=F32)
        y = y + jnp.dot(yc_ref[...], w_ref[DA:2 * DA, :], preferred_element_type=F32)
        gate_v = g_ref[...]
        z = ALPHA * x_ref[...] + gate_v * y
        mu = jnp.mean(z, axis=-1, keepdims=True)
        zc = z - mu
        rstd = lax.rsqrt(jnp.mean(zc * zc, axis=-1, keepdims=True) + LN_EPS)
        zh = zc * rstd
        diff = zh * lg_ref[...] + lb_ref[...] - t_ref[...]
        dout = diff * (1.0 / D)
        dzh = dout * lg_ref[...]
        m1 = jnp.mean(dzh, axis=-1, keepdims=True)
        m2 = jnp.mean(dzh * zh, axis=-1, keepdims=True)
        dz = rstd * (dzh - m1 - zh * m2)
        gx_ref[...] = ALPHA * dz
        dy_ref[...] = (dz * gate_v).astype(BF16)
        st_ref[0:1, :] += jnp.sum(dout * zh, axis=0, keepdims=True)
        st_ref[1:2, :] += jnp.sum(dout, axis=0, keepdims=True)
        st_ref[2:3, :] += jnp.sum(dz * y, axis=0, keepdims=True)
        st_ref[3:4, :] += jnp.sum(diff * diff, axis=0, keepdims=True) * (0.5 / D)

        @pl.when(pl.program_id(0) == t // tm - 1)
        def _():
            st_ref[3:4, :] = jnp.broadcast_to(jnp.sum(st_ref[3:4, :], axis=-1, keepdims=True), (1, D))

    row = pl.BlockSpec((1, D), lambda i: (0, 0))
    half = pl.BlockSpec((tm, DA), lambda i: (i, 0))
    full = pl.BlockSpec((tm, D), lambda i: (i, 0))
    return pl.pallas_call(
        body, name="outproj_loss", grid=(t // tm,),
        in_specs=[half, half, pl.BlockSpec((D, D), lambda i: (0, 0)), full, full, row, row, row],
        out_specs=[full, full, pl.BlockSpec((8, D), lambda i: (0, 0))],
        out_shape=[S((t, D), F32), S((t, D), BF16), S((8, D), F32)],
        compiler_params=_cp(("arbitrary",), 56),
    )(ya, yc, w_out, x, target, gate, ln_g, ln_b)


COL_DQ, COL_DKV, COL_DGA, COL_DGLU_A, COL_DGLU_B, COL_DGC = 0, 1024, 1536, 2560, 3584, 4608


def _tile_copy(dst, stage, sems, slot, row0, col0):
    rows, width = stage.shape[1:]
    return pltpu.make_async_copy(stage.at[slot], dst.at[pl.ds(row0, rows), pl.ds(col0, width)], sems.at[slot])


def _stage_slot(dst, stage, sems, step):
    slot = step % 2

    @pl.when(step >= 2)
    def _():
        _tile_copy(dst, stage, sems, slot, 0, 0).wait()

    return slot


def _tile_send(dst, stage, sems, step, nsteps, col0):
    rows = stage.shape[1]
    slot = step % 2
    _tile_copy(dst, stage, sems, slot, pl.multiple_of(step * rows, rows), col0).start()

    @pl.when(step == nsteps - 1)
    def _():
        _tile_copy(dst, stage, sems, slot, 0, col0).wait()
        if nsteps > 1:
            _tile_copy(dst, stage, sems, 1 - slot, 0, col0).wait()


def _dycat_gates(dy, w_out, attn_o, pw, proj, dproj):
    t = dy.shape[0]
    tm = min(256, t)
    nt = t // tm

    def body(dy_ref, w_ref, o_ref, pw_ref, galo, gahi, gclo, gchi, dp_in, dao_ref, dpw_ref, dp_ref,
             st_a, st_c, sem_a, sem_c):
        i = pl.program_id(0)
        sl_a = _stage_slot(dp_ref, st_a, sem_a, i)
        sl_c = _stage_slot(dp_ref, st_c, sem_c, i)
        dyc = lax.dot_general(dy_ref[...], w_ref[...], (((1,), (1,)), ((), ())), preferred_element_type=F32)
        da, dc = dyc[:, 0:DA], dyc[:, DA:2 * DA]
        ga = jnp.concatenate([galo[...], gahi[...]], axis=1).astype(F32)
        sa = _sig(ga)
        dao_ref[...] = (da * (ga * sa)).astype(BF16)
        st_a[sl_a] = (da * o_ref[...].astype(F32) * _dsilu(ga, sa)).astype(BF16)
        _tile_send(dp_ref, st_a, sem_a, i, nt, COL_DGA)
        gc = jnp.concatenate([gclo[...], gchi[...]], axis=1).astype(F32)
        sc = _sig(gc)
        dpw_ref[...] = (dc * (gc * sc)).astype(BF16)
        st_c[sl_c] = (dc * pw_ref[...].astype(F32) * _dsilu(gc, sc)).astype(BF16)
        _tile_send(dp_ref, st_c, sem_c, i, nt, COL_DGC)

    half = pl.BlockSpec((tm, DA), lambda i: (i, 0))
    cur = lambda cidx: pl.BlockSpec((tm, 512), lambda i: (i, cidx))
    return pl.pallas_call(
        body, name="dycat_gates", grid=(nt,),
        in_specs=[pl.BlockSpec((tm, D), lambda i: (i, 0)), pl.BlockSpec((D, D), lambda i: (0, 0)), half, half,
                  cur(3), cur(4), cur(9), cur(10), _ANY],
        out_specs=[half, half, _ANY],
        out_shape=[S((t, DA), BF16), S((t, DA), BF16), S(dproj.shape, dproj.dtype)],
        input_output_aliases={8: 2},
        scratch_shapes=[pltpu.VMEM((2, tm, DA), BF16), pltpu.VMEM((2, tm, DA), BF16),
                        pltpu.SemaphoreType.DMA((2,)), pltpu.SemaphoreType.DMA((2,))],
        compiler_params=_cp(("arbitrary",), 48),
    )(dy, w_out, attn_o, pw, proj, proj, proj, proj, dproj)


def _conv_bwd_ln(dpw, wpw, uc, ln_g, ln_b):
    t = dpw.shape[0]
    tc = min(256, t)

    def body(dpw_ref, w_ref, uc_ref, lg_ref, lb_ref, duc_ref, sw_ref, st_ref):
        @pl.when(pl.program_id(0) == 0)
        def _():
            st_ref[...] = jnp.zeros_like(st_ref)

        dpw_v = dpw_ref[...]
        ds = lax.dot_general(dpw_v, w_ref[...], (((1,), (1,)), ((), ())), preferred_element_type=F32)
        uc = uc_ref[...]
        mu = jnp.mean(uc, axis=-1, keepdims=True)
        xc = uc - mu
        rstd = lax.rsqrt(jnp.mean(xc * xc, axis=-1, keepdims=True) + LN_EPS)
        uh = xc * rstd
        ln = uh * lg_ref[...] + lb_ref[...]
        sg = _sig(ln)
        sw_ref[...] = (ln * sg).astype(BF16)
        dln = ds * _dsilu(ln, sg)
        dxh = dln * lg_ref[...]
        m1 = jnp.mean(dxh, axis=-1, keepdims=True)
        m2 = jnp.mean(dxh * uh, axis=-1, keepdims=True)
        duc = rstd * (dxh - m1 - uh * m2)
        duc_ref[...] = duc
        st_ref[0:1, :] += jnp.sum(dln * uh, axis=0, keepdims=True)
        st_ref[1:2, :] += jnp.sum(dln, axis=0, keepdims=True)
        st_ref[2:3, :] += jnp.sum(duc, axis=0, keepdims=True)
        st_ref[3:4, :] += jnp.sum(dpw_v.astype(F32), axis=0, keepdims=True)

    row = pl.BlockSpec((1, DA), lambda i: (0, 0))
    full = pl.BlockSpec((tc, DA), lambda i: (i, 0))
    return pl.pallas_call(
        body, name="conv_bwd_ln", grid=(t // tc,),
        in_specs=[full, pl.BlockSpec((DA, DA), lambda i: (0, 0)), full, row, row],
        out_specs=[full, full, pl.BlockSpec((8, DA), lambda i: (0, 0))],
        out_shape=[S((t, DA), F32), S((t, DA), BF16), S((8, DA), F32)],
        compiler_params=_cp(("arbitrary",), 48),
    )(dpw, wpw, uc, ln_g, ln_b)


def _conv_bwd_dw(duc, proj, cwb, dproj):
    t = duc.shape[0]
    tc = _conv_rows(t)
    rc = min(128, tc)
    nt = t // tc
    off = HALO - CW + 1

    def body(dcur, dnext, alo, ahi, blo, bhi, alo_h, ahi_h, blo_h, bhi_h, cw_ref, dp_in, gw_ref, dp_ref,
             u_scr, d_scr, g_scr, st_ab, sem_ab):
        i = pl.program_id(0)
        slot = _stage_slot(dp_ref, st_ab, sem_ab, i)

        @pl.when(i == 0)
        def _():
            g_scr[...] = jnp.zeros_like(g_scr)

        _glu_into(u_scr, i, (alo, ahi), (blo, bhi), (alo_h, ahi_h), (blo_h, bhi_h), tc)
        for cc in range(DA // 128):
            d_scr[cc, 0:tc, :] = dcur[:, cc * 128:(cc + 1) * 128]
            d_scr[cc, tc:tc + HALO, :] = jnp.where(i == nt - 1, 0.0, dnext[:, cc * 128:(cc + 1) * 128])

        def rows(r, carry):
            r0 = pl.multiple_of(r * rc, rc)
            for cc in range(DA // 128):
                lanes = pl.ds(cc * 128, 128)
                acc = jnp.zeros((rc // 8, 8, 128), F32)
                for j in range(CW):
                    dv = d_scr[cc, pl.ds(r0 + j, rc), :].reshape(rc // 8, 8, 128)
                    acc = acc + dv * cw_ref[CW - 1 - j, :, lanes]
                du = acc.reshape(rc, 128)
                d0 = d_scr[cc, pl.ds(r0, rc), :].reshape(rc // 8, 8, 128)
                for k in range(CW):
                    u = u_scr[cc, pl.ds(r0 + off + k, rc), :].reshape(rc // 8, 8, 128)
                    g_scr[k, :, lanes] += jnp.sum(d0 * u, axis=0)
                a_ref, b_ref = (alo, blo) if cc < 4 else (ahi, bhi)
                lc = pl.ds((cc % 4) * 128, 128)
                av = a_ref[pl.ds(r0, rc), lc].astype(F32)
                sb = _sig(b_ref[pl.ds(r0, rc), lc].astype(F32))
                st_ab[slot, pl.ds(r0, rc), lanes] = (du * sb).astype(BF16)
                st_ab[slot, pl.ds(r0, rc), pl.ds(DA + cc * 128, 128)] = (du * av * sb * (1.0 - sb)).astype(BF16)
            return carry

        lax.fori_loop(0, tc // rc, rows, 0)
        _tile_send(dp_ref, st_ab, sem_ab, i, nt, COL_DGLU_A)

        @pl.when(i == nt - 1)
        def _():
            gw_ref[0:CW, :] = jnp.sum(g_scr[...], axis=1)
            gw_ref[CW:32, :] = jnp.zeros((32 - CW, DA), F32)

    nh = tc // HALO
    nhb = t // HALO
    cur = lambda cidx: pl.BlockSpec((tc, 512), lambda i: (i, cidx))
    halo = lambda cidx: pl.BlockSpec((HALO, 512), lambda i: (jnp.maximum(i * nh - 1, 0), cidx))
    full = pl.BlockSpec((tc, DA), lambda i: (i, 0))
    return pl.pallas_call(
        body, name="conv_bwd_dw", grid=(nt,),
        in_specs=[full, pl.BlockSpec((HALO, DA), lambda i: (jnp.minimum((i + 1) * nh, nhb - 1), 0)),
                  cur(5), cur(6), cur(7), cur(8), halo(5), halo(6), halo(7), halo(8),
                  pl.BlockSpec((CW, 8, DA), lambda i: (0, 0, 0)), _ANY],
        out_specs=[pl.BlockSpec((32, DA), lambda i: (0, 0)), _ANY],
        out_shape=[S((32, DA), F32), S(dproj.shape, dproj.dtype)],
        input_output_aliases={11: 1},
        scratch_shapes=[pltpu.VMEM((DA // 128, HALO + tc, 128), F32), pltpu.VMEM((DA // 128, tc + HALO, 128), F32),
                        pltpu.VMEM((CW, 8, DA), F32), pltpu.VMEM((2, tc, 2 * DA), BF16), pltpu.SemaphoreType.DMA((2,))],
        compiler_params=_cp(("arbitrary",), 48),
    )(duc, duc, *([proj] * 8), cwb, dproj)


def _attn_bwd(proj, dao, lse, bias, sinks, dproj):
    t = proj.shape[0]

    def body(sink_ref, bias_ref, q_ref, kvc_ref, kvp_ref, do_ref, lse_ref, dp_in,
             dcur_ref, dprev_ref, dsacc_ref, dsk_ref, dp_ref, st_q, sem_q):
        n = pl.program_id(0)
        slot = _stage_slot(dp_ref, st_q, sem_q, n)

        @pl.when(n == 0)
        def _():
            dsacc_ref[...] = jnp.zeros_like(dsacc_ref)
            dsk_ref[...] = jnp.zeros_like(dsk_ref)

        tbl = jnp.where(n == 0, 0, 1)
        kv = jnp.concatenate([kvp_ref[...], kvc_ref[...]], axis=0)
        for h in range(NKV):
            k_h = kv[:, h * HD:(h + 1) * HD]
            v_h = kv[:, NKV * HD + h * HD:NKV * HD + (h + 1) * HD]
            q4 = _group_rows(q_ref, h) * SCALE
            do4 = _group_rows(do_ref, h)
            lse_h = lse_ref[h:h + 1, :]
            s = lax.dot_general(k_h, q4, (((1,), (1,)), ((), ())), preferred_element_type=F32)
            p = jnp.exp(s + bias_ref[tbl, h] - lse_h)
            dp = lax.dot_general(v_h, do4, (((1,), (1,)), ((), ())), preferred_element_type=F32)
            delta = jnp.sum(p * dp, axis=0, keepdims=True)
            ds = p * (dp - delta)
            dsacc_ref[h] += ds
            dsk_ref[h:h + 1, :] += -jnp.exp(_sink_row(sink_ref, h) - lse_h) * delta
            dsb = ds.astype(BF16)
            dq4 = lax.dot_general(dsb, k_h, (((0,), (0,)), ((), ())), preferred_element_type=F32) * SCALE
            for g in range(GRP):
                hd = GRP * h + g
                st_q[slot, :, hd * HD:(hd + 1) * HD] = dq4[g * BLK:(g + 1) * BLK].astype(BF16)
            dk = jnp.dot(dsb, q4, preferred_element_type=F32)
            dv = jnp.dot(p.astype(BF16), do4, preferred_element_type=F32)
            kc = slice(h * HD, (h + 1) * HD)
            vc = slice(NKV * HD + h * HD, NKV * HD + (h + 1) * HD)
            dprev_ref[:, kc] = dk[0:BLK]
            dcur_ref[:, kc] = dk[BLK:2 * BLK]
            dprev_ref[:, vc] = dv[0:BLK]
            dcur_ref[:, vc] = dv[BLK:2 * BLK]

        _tile_send(dp_ref, st_q, sem_q, n, t // BLK, COL_DQ)

        @pl.when(n == t // BLK - 1)
        def _():
            for g in range(GRP):
                lanes = slice(g * BLK, (g + 1) * BLK)
                dsk_ref[:, lanes] = jnp.broadcast_to(jnp.sum(dsk_ref[:, lanes], axis=-1, keepdims=True), (NKV, BLK))

    blk = lambda w, cidx: pl.BlockSpec((BLK, w), lambda n: (n, cidx))
    return pl.pallas_call(
        body, name="attn_bwd", grid=(t // BLK,),
        in_specs=[_SMEM, pl.BlockSpec((2, NKV, 2 * BLK, GRP * BLK), lambda n: (0, 0, 0, 0)),
                  blk(DA, 0), blk(512, 2), pl.BlockSpec((BLK, 512), lambda n: (jnp.maximum(n - 1, 0), 2)),
                  blk(DA, 0), pl.BlockSpec((None, NKV, GRP * BLK), lambda n: (n, 0, 0)), _ANY],
        out_specs=[blk(512, 0), blk(512, 0),
                   pl.BlockSpec((NKV, 2 * BLK, GRP * BLK), lambda n: (0, 0, 0)), pl.BlockSpec((NKV, GRP * BLK), lambda n: (0, 0)),
                   _ANY],
        out_shape=[S((t, 512), F32), S((t, 512), F32), S((NKV, 2 * BLK, GRP * BLK), F32),
                   S((NKV, GRP * BLK), F32), S(dproj.shape, dproj.dtype)],
        input_output_aliases={7: 4},
        scratch_shapes=[pltpu.VMEM((2, BLK, DA), BF16), pltpu.SemaphoreType.DMA((2,))],
        compiler_params=_cp(("arbitrary",), 48),
    )(sinks, bias, proj, proj, proj, dao, lse, dproj)


def _dkv_fix(dcur, dprev, dproj):
    t = dcur.shape[0]
    tb = min(1024, t)
    nt = t // tb
    per = tb // BLK

    def body(c_ref, p_ref, pn_ref, dp_in, dp_ref, st_kv, sem_kv):
        i = pl.program_id(0)
        slot = _stage_slot(dp_ref, st_kv, sem_kv, i)
        if tb > BLK:
            st_kv[slot, 0:tb - BLK, :] = (c_ref[0:tb - BLK, :] + p_ref[BLK:tb, :]).astype(BF16)
        nxt = jnp.where(i == nt - 1, 0.0, pn_ref[...])
        st_kv[slot, tb - BLK:tb, :] = (c_ref[tb - BLK:tb, :] + nxt).astype(BF16)
        _tile_send(dp_ref, st_kv, sem_kv, i, nt, COL_DKV)

    tile = pl.BlockSpec((tb, 512), lambda i: (i, 0))
    return pl.pallas_call(
        body, name="dkv_fix", grid=(nt,),
        in_specs=[tile, tile, pl.BlockSpec((BLK, 512), lambda i: (jnp.minimum((i + 1) * per, t // BLK - 1), 0)), _ANY],
        out_specs=_ANY,
        out_shape=S(dproj.shape, dproj.dtype),
        input_output_aliases={3: 0},
        scratch_shapes=[pltpu.VMEM((2, tb, 512), BF16), pltpu.SemaphoreType.DMA((2,))],
        compiler_params=_cp(("arbitrary",), 32),
    )(dcur, dprev, dprev, dproj)


def _bias_grad(dsacc, bucket):
    def body(ds_ref, bk_ref, o_ref, row_scr):
        bk = bk_ref[...]

        def group(h, carry):
            for g in range(GRP):
                dsv = ds_ref[h, :, g * BLK:(g + 1) * BLK]
                for b in range(N_BUCKETS):
                    row_scr[GRP * h + g, b:b + 1, :] = jnp.sum(jnp.where(bk == b, dsv, 0.0), axis=0, keepdims=True)
            return carry

        lax.fori_loop(0, NKV, group, 0)
        for hd in range(NQ):
            o_ref[hd] = jnp.sum(row_scr[hd], axis=-1, keepdims=True)

    return pl.pallas_call(
        body, name="bias_grad", out_shape=S((NQ, N_BUCKETS, 1), F32),
        in_specs=[_VMEM, _VMEM], out_specs=_VMEM,
        scratch_shapes=[pltpu.VMEM((NQ, N_BUCKETS, BLK), F32)],
        compiler_params=_cp(None, 32),
    )(dsacc, bucket)


def _dh_gradx(dproj, wg, gx0, x, scale1p, p_in, p_out, p_pw):
    t = x.shape[0]
    tm = min(512, t)
    ni = t // tm
    bn = D // 2
    nn = D // bn

    def body(dp_ref, w_ref, gx_ref, x_ref, sc_ref, pi, po, pp, out_ref, st_ref, li, lo, lp, send_sems, recv_sems):
        n, i = pl.program_id(0), pl.program_id(1)

        @pl.when((n == 0) & (i == 0))
        def _():
            for cp in _chip_exchange_copies(pi, po, pp, li, lo, lp, send_sems, recv_sems):
                cp.start()

        @pl.when(i == 0)
        def _():
            st_ref[...] = jnp.zeros_like(st_ref)

        dh = None
        for j in range(NCHIP):
            part = lax.dot_general(dp_ref[:, j * NB:(j + 1) * NB], w_ref[j], (((1,), (1,)), ((), ())),
                                   preferred_element_type=F32)
            dh = part if dh is None else dh + part
        out_ref[...] = gx_ref[...] + dh * sc_ref[...]
        st_ref[0:1, :] += jnp.sum(dh, axis=0, keepdims=True)
        st_ref[1:2, :] += jnp.sum(dh * x_ref[...], axis=0, keepdims=True)

        @pl.when((n == nn - 1) & (i == ni - 1))
        def _():
            cps = _chip_exchange_copies(pi, po, pp, li, lo, lp, send_sems, recv_sems)
            for cp in cps:
                cp.wait_recv()
            for cp in cps:
                cp.wait_send()

    tile = pl.BlockSpec((tm, bn), lambda n, i: (i, n))
    return pl.pallas_call(
        body, name="dh_gradx", grid=(nn, ni),
        in_specs=[pl.BlockSpec((tm, D_IN), lambda n, i: (i, 0)), pl.BlockSpec((NCHIP, bn, NB), lambda n, i: (0, n, 0)),
                  tile, tile, pl.BlockSpec((1, bn), lambda n, i: (0, n)), _ANY, _ANY, _ANY],
        out_specs=[tile, pl.BlockSpec((8, bn), lambda n, i: (0, n)), _ANY, _ANY, _ANY],
        out_shape=[S((t, D), F32), S((8, D), F32), S((3, p_in.shape[0], NB), p_in.dtype),
                   S((3,) + p_out.shape[1:], p_out.dtype), S((3,) + p_pw.shape[1:], p_pw.dtype)],
        scratch_shapes=[pltpu.SemaphoreType.DMA((9,)), pltpu.SemaphoreType.DMA((9,))],
        compiler_params=_cp(("arbitrary", "arbitrary"), 60),
    )(dproj, wg, gx0, x, scale1p, p_in, p_out, p_pw)


def _atb(a_parts, b, bn, name, halves=None):
    bm = DA
    t = b.shape[0]
    n = b.shape[1]
    tk = min(2048, t)
    nk = t // tk
    na = len(a_parts)
    nj = n // bn

    def body(*refs):
        a_refs, b_ref, o_ref = refs[:na], refs[na], refs[na + 1 if halves is None else na + 2]
        mi, k = pl.program_id(0), pl.program_id(2)
        if halves is not None:
            g_ref, l_ref, send_sem, recv_sem = refs[na + 1], refs[na + 3], refs[na + 4], refs[na + 5]
            mx, my, c = _me()

            def exchange(hc):
                return pltpu.make_async_remote_copy(src_ref=g_ref.at[hc], dst_ref=l_ref, send_sem=send_sem, recv_sem=recv_sem,
                                                    device_id=(mx, my, 1 - c), device_id_type=MESH)

            first = (mi == 0) & (pl.program_id(1) == 0) & (k == 0)
            last = (mi == na - 1) & (pl.program_id(1) == nj - 1) & (k == nk - 1)

            @pl.when(first)
            def _():
                exchange(1 - c).start()

        for q in range(na):
            @pl.when(mi == q)
            def _(q=q):
                part = lax.dot_general(a_refs[q][...], b_ref[...], (((0,), (0,)), ((), ())), preferred_element_type=F32)

                @pl.when(k == 0)
                def _():
                    o_ref[...] = part

                @pl.when(k > 0)
                def _():
                    o_ref[...] += part

        if halves is not None:
            @pl.when(last)
            def _():
                exchange(c).wait_recv()
                exchange(1 - c).wait_send()

    a_spec = lambda q, cidx: pl.BlockSpec((tk, bm), lambda mi, j, k: (jnp.where(mi == q, k, 0), cidx))
    in_specs = [a_spec(q, cidx) for q, (_, cidx) in enumerate(a_parts)] + [pl.BlockSpec((tk, bn), lambda mi, j, k: (k, j))]
    out_spec = pl.BlockSpec((bm, bn), lambda mi, j, k: (mi, j))
    out_shape = S((na * bm, n), F32)
    args = [arr for arr, _ in a_parts] + [b]
    if halves is None:
        return pl.pallas_call(
            body, name=name, grid=(na, nj, nk), in_specs=in_specs, out_specs=out_spec, out_shape=out_shape,
            compiler_params=_cp(("parallel", "parallel", "arbitrary"), 56),
        )(*args)
    return pl.pallas_call(
        body, name=name, grid=(na, nj, nk), in_specs=in_specs + [_ANY], out_specs=[out_spec, _ANY],
        out_shape=[out_shape, S(halves.shape[1:], halves.dtype)],
        scratch_shapes=[pltpu.SemaphoreType.DMA, pltpu.SemaphoreType.DMA],
        compiler_params=_cp(("arbitrary", "arbitrary", "arbitrary"), 56),
    )(*args, halves)


def _sum8(parts):
    _, r, n = parts.shape

    def body(p_ref, o_ref):
        v = p_ref[0]
        for d in range(1, 8):
            v = v + p_ref[d]
        o_ref[...] = v

    return pl.pallas_call(body, name="sum8", out_shape=S((r, n), F32), in_specs=[_VMEM], out_specs=_VMEM,
                          compiler_params=_cp(None, 32))(parts)


def _adam_math(w, g, m, v):
    m = B1 * m + (1.0 - B1) * g
    v = B2 * v + (1.0 - B2) * (g * g)
    m_hat = m / (1.0 - B1 ** STEP)
    v_hat = v / (1.0 - B2 ** STEP)
    delta = -LR * (m_hat / (jnp.sqrt(v_hat) + EPS) + WD * w)
    return delta, m, v


def _adamw(w, g, m, v, name):
    r, n = w.shape
    tr = min(256, r)

    def body(w_ref, g_ref, m_ref, v_ref, d_ref, nm_ref, nv_ref):
        d_ref[...], nm_ref[...], nv_ref[...] = _adam_math(w_ref[...], g_ref[...], m_ref[...], v_ref[...])

    spec = pl.BlockSpec((tr, n), lambda i: (i, 0))
    return pl.pallas_call(
        body, name=name, grid=(r // tr,), in_specs=[spec] * 4, out_specs=[spec] * 3,
        out_shape=[S((r, n), F32)] * 3, compiler_params=_cp(("parallel",), 48),
    )(w, g, m, v)


def _adamw_ada(ct, dmod_p, w, m, v):
    r, n = w.shape
    tr = min(256, r)

    def body(ct_ref, dm_ref, w_ref, m_ref, v_ref, g_ref, d_ref, nm_ref, nv_ref):
        cv = ct_ref[...]
        g = jnp.dot(cv * _sig(cv), dm_ref[...], preferred_element_type=F32)
        g_ref[...] = g
        d_ref[...], nm_ref[...], nv_ref[...] = _adam_math(w_ref[...], g, m_ref[...], v_ref[...])

    spec = pl.BlockSpec((tr, n), lambda i: (i, 0))
    return pl.pallas_call(
        body, name="adamw_ada", grid=(r // tr,),
        in_specs=[pl.BlockSpec((tr, 8), lambda i: (i, 0)), pl.BlockSpec((8, n), lambda i: (0, 0)), spec, spec, spec],
        out_specs=[spec] * 4, out_shape=[S((r, n), F32)] * 4, compiler_params=_cp(("parallel",), 48),
    )(ct, dmod_p, w, m, v)


def _adamw_small(ws, gs, ms, vs):
    k = len(ws)

    def body(*refs):
        ins, outs = refs[:4 * k], refs[4 * k:]
        for i in range(k):
            d, nm, nv = _adam_math(ins[i][...], ins[k + i][...], ins[2 * k + i][...], ins[3 * k + i][...])
            outs[i][...] = d
            outs[k + i][...] = nm
            outs[2 * k + i][...] = nv

    shapes = [S(w.shape, F32) for w in ws]
    return pl.pallas_call(body, name="adamw_small", out_shape=shapes * 3, in_specs=[_VMEM] * (4 * k),
                          out_specs=[_VMEM] * (3 * k), compiler_params=_cp(None, 32))(*ws, *gs, *ms, *vs)


def _bucket_map():
    qi = jnp.arange(BLK, dtype=jnp.int32)[None, :]
    kj = jnp.arange(2 * BLK, dtype=jnp.int32)[:, None]
    dist = qi + BLK - kj
    dd = jnp.maximum(dist, 0)
    max_exact = N_BUCKETS // 2
    dfl = jnp.maximum(dd, 1).astype(F32)
    large = max_exact + (jnp.log(dfl / max_exact) / math.log(MAX_DIST / max_exact) * (N_BUCKETS - max_exact)).astype(jnp.int32)
    large = jnp.minimum(large, N_BUCKETS - 1)
    bucket = jnp.where(dd < max_exact, dd, large)
    return jnp.where((dist >= 0) & (dist < BLK), bucket, -1).astype(jnp.int32)


def _pad_rows(a, rows):
    return jnp.pad(a, ((0, rows - a.shape[0]), (0, 0)))


def kernel(x, c, w_ada, b_ada, w_in, rel_bias, sinks, conv_w, conv_b, conv_ln_g, conv_ln_b, w_pw, b_pw, w_out, ln_g, ln_b, loss_target, m_w_ada, m_b_ada, m_w_in, m_rel_bias, m_sinks, m_conv_w, m_conv_b, m_conv_ln_g, m_conv_ln_b, m_w_pw, m_b_pw, m_w_out, m_ln_g, m_ln_b, v_w_ada, v_b_ada, v_w_in, v_rel_bias, v_sinks, v_conv_w, v_conv_b, v_conv_ln_g, v_conv_ln_b, v_w_pw, v_b_pw, v_w_out, v_ln_g, v_ln_b):
    mx, my, mc = _me()
    chip = 2 * mx + my
    dev = 2 * chip + mc
    t = x.shape[1]
    x2 = x.reshape(t, D)
    tgt = loss_target.reshape(t, D)
    n_ada = w_ada.shape[2]
    cw_cols = conv_w.shape[2]

    pack0 = jnp.concatenate([_pad_rows(c, 8), _pad_rows(_pad_rows(conv_w[0], 32).reshape(-1, D), 8)], axis=0)
    g0 = _allgather8(pack0, "gather_c_convw").reshape(8, 16, D)
    c_all = g0[:, 0, :]
    cw_rows = 32 * cw_cols // D
    cw_full = jnp.concatenate([g0[2 * q, 8:8 + cw_rows, :].reshape(32, cw_cols) for q in range(NCHIP)], axis=1)[:CW]
    cwb = jnp.broadcast_to(cw_full[:, None, :], (CW, 8, DA))

    b_ada_p = lax.dynamic_slice(b_ada, (0, chip * n_ada), (1, n_ada))
    mod_part = _ada_fwd(c_all, w_ada[0], b_ada_p)
    mod_all = _allgather8(mod_part, "gather_mod").reshape(8, 8, n_ada)
    mod = jnp.concatenate([lax.dynamic_slice(mod_all[2 * q], (dev, 0), (1, n_ada)) for q in range(NCHIP)], axis=1)
    shift, scale, gate = mod[:, 0:D], mod[:, D:2 * D], mod[:, 2 * D:3 * D]
    scale1p = 1.0 + scale

    bucket = _bucket_map()
    sel_chip = jnp.reshape(chip, (1,)).astype(jnp.int32)
    sel = jnp.reshape(mc, (1,)).astype(jnp.int32)
    proj, h, wg_in, wg_out, wg_pw = _inproj_gather(
        x2, scale1p, shift, w_in[0].astype(BF16), w_out[0].astype(BF16), w_pw[0].astype(BF16), sel_chip)
    w_out_f = wg_out.reshape(D, D)
    w_pw_f = wg_pw.reshape(DA, DA)
    ya, attn_o, lse, bias_tables = _attn_fwd(proj, rel_bias, sinks, bucket)
    uc, pw, yc = _conv_fwd(proj, cwb, conv_b, conv_ln_g, conv_ln_b, w_pw_f, b_pw)
    gx0, dy, st_out = _outproj_loss(ya, yc, w_out_f, x2, tgt, gate, ln_g, ln_b)

    dproj = lax.empty((t, D_IN), BF16)
    dao, dpw, dproj = _dycat_gates(dy, w_out_f, attn_o, pw, proj, dproj)
    duc, sw, st_conv = _conv_bwd_ln(dpw, w_pw_f, uc, conv_ln_g, conv_ln_b)
    gw_conv, dproj = _conv_bwd_dw(duc, proj, cwb, dproj)
    dkv_cur, dkv_prev, dsacc, dsk, dproj = _attn_bwd(proj, dao, lse, bias_tables, sinks, dproj)
    dproj = _dkv_fix(dkv_cur, dkv_prev, dproj)
    g_rel = _bias_grad(dsacc, bucket).reshape(NQ, N_BUCKETS).T
    r_out, r_pw = D // NCHIP // 2, DA // NCHIP // 2
    gp_in = _atb([(h, 0), (h, 1)], dproj, NB, "grad_w_in")
    gi3 = gp_in.reshape(2, D // 2, D_IN)
    gp_out, l_in = _atb([(ya, 0), (yc, 0)], dy, 1024, "grad_w_out", halves=gi3)
    gp_pw = _atb([(sw, 0)], dpw, 1024, "grad_w_pw")
    go3, gp3 = gp_out.reshape(8, r_out, D), gp_pw.reshape(8, r_pw, DA)
    l_out, l_pw = _pair_exchange(go3, gp3)
    pi32, pi16 = _pair_sum(gi3, l_in.reshape(1, D // 2, D_IN), sel, "pair_sum_in")
    po32, po16 = _pair_sum(go3, l_out, sel, "pair_sum_out")
    pp32, pp16 = _pair_sum(gp3, l_pw, sel, "pair_sum_pw")
    grad_x, st_in, rc_in, rc_out, rc_pw = _dh_gradx(dproj, wg_in, gx0, x2, scale1p, pi16[0], po16, pp16)
    own_in = lax.dynamic_slice(pi32[0], (0, chip * NB), (D // 2, NB))
    own_out = lax.dynamic_index_in_dim(po32, chip, 0, keepdims=False)
    own_pw = lax.dynamic_index_in_dim(pp32, chip, 0, keepdims=False)
    h_in = _chip_sum(own_in, rc_in, sel, "chip_sum_in")
    h_out = _chip_sum(own_out, rc_out, sel, "chip_sum_out")
    h_pw = _chip_sum(own_pw, rc_pw, sel, "chip_sum_pw")
    f_in, f_out, f_pw = _pair_share(h_in, h_out, h_pw)
    g_w_in = f_in.reshape(D, NB)
    g_w_out = f_out.reshape(D // NCHIP, D)
    g_w_pw = f_pw.reshape(DA // NCHIP, DA)

    dmod = jnp.concatenate([st_in[0:1], st_in[1:2], st_out[2:3]], axis=1)
    loss_row = st_out[3:4, 0:1]
    small = jnp.concatenate([
        dmod, st_out[0:1], st_out[1:2],
        st_conv[0:1], st_conv[1:2], st_conv[2:3], st_conv[3:4],
        g_rel.reshape(1, N_BUCKETS * NQ), dsk.reshape(NKV, GRP, BLK)[:, :, 0].reshape(1, NQ), loss_row,
        gw_conv[:CW].reshape(1, CW * DA)], axis=1)
    n_small = small.shape[1]
    rows_small = -(-n_small // (8 * D)) * 8
    small = jnp.pad(small, ((0, 0), (0, rows_small * D - n_small))).reshape(rows_small, D)
    parts = _allgather8(small, "gather_small").reshape(8, rows_small, D)
    tot = _sum8(parts).reshape(1, rows_small * D)
    dmod_all = parts.reshape(8, rows_small * D)[:, 0:3 * D]

    o = 3 * D
    def take(nn):
        nonlocal o
        v = tot[:, o:o + nn]
        o += nn
        return v
    g_b_ada = tot[:, 0:3 * D]
    g_ln_g, g_ln_b = take(D), take(D)
    g_cln_g, g_cln_b, g_conv_b, g_b_pw = take(DA), take(DA), take(DA), take(DA)
    g_rel_bias = take(N_BUCKETS * NQ).reshape(N_BUCKETS, NQ)
    g_sinks = take(NQ)
    loss = take(1).reshape(())
    g_conv_w_full = take(CW * DA).reshape(CW, DA)
    g_conv_w = lax.dynamic_slice(g_conv_w_full, (0, chip * cw_cols), (CW, cw_cols))

    dmod_p = lax.dynamic_slice(dmod_all, (0, chip * n_ada), (8, n_ada))
    g_w_ada, d_w_ada, nm_w_ada, nv_w_ada = _adamw_ada(c_all.T, dmod_p, w_ada[0], m_w_ada[0], v_w_ada[0])
    d_w_in, nm_w_in, nv_w_in = _adamw(w_in[0], g_w_in, m_w_in[0], v_w_in[0], "adamw_in")
    d_w_out, nm_w_out, nv_w_out = _adamw(w_out[0], g_w_out, m_w_out[0], v_w_out[0], "adamw_out")
    d_w_pw, nm_w_pw, nv_w_pw = _adamw(w_pw[0], g_w_pw, m_w_pw[0], v_w_pw[0], "adamw_pw")
    small_w = [b_ada, rel_bias, sinks, conv_w[0], conv_b, conv_ln_g, conv_ln_b, b_pw, ln_g, ln_b]
    small_g = [g_b_ada, g_rel_bias, g_sinks, g_conv_w, g_conv_b, g_cln_g, g_cln_b, g_b_pw, g_ln_g, g_ln_b]
    small_m = [m_b_ada, m_rel_bias, m_sinks, m_conv_w[0], m_conv_b, m_conv_ln_g, m_conv_ln_b, m_b_pw, m_ln_g, m_ln_b]
    small_v = [v_b_ada, v_rel_bias, v_sinks, v_conv_w[0], v_conv_b, v_conv_ln_g, v_conv_ln_b, v_b_pw, v_ln_g, v_ln_b]
    res = _adamw_small(small_w, small_g, small_m, small_v)
    ns = len(small_w)
    d_s, nm_s, nv_s = res[:ns], res[ns:2 * ns], res[2 * ns:]

    def ordered(w_ada_, w_in_, w_pw_, w_out_, sm):
        b_ada_, rel_, sinks_, conv_w_, conv_b_, cln_g_, cln_b_, b_pw_, ln_g_, ln_b_ = sm
        return (w_ada_[None], b_ada_, w_in_[None], rel_, sinks_, conv_w_[None], conv_b_, cln_g_, cln_b_,
                w_pw_[None], b_pw_, w_out_[None], ln_g_, ln_b_)

    grads = ordered(g_w_ada, g_w_in, g_w_pw, g_w_out, small_g)
    deltas = ordered(d_w_ada, d_w_in, d_w_pw, d_w_out, d_s)
    new_m = ordered(nm_w_ada, nm_w_in, nm_w_pw, nm_w_out, nm_s)
    new_v = ordered(nv_w_ada, nv_w_in, nv_w_pw, nv_w_out, nv_s)
    return (loss, grad_x.reshape(1, t, D), *grads, *deltas, *new_m, *new_v)
```
